```python
import math
import jax, jax.numpy as jnp
from jax import lax
import numpy as np

D_MODEL = 1024
BATCH = 1
SEQ = 16384
DEPTH = 2
DEC_BATCH = 4
DEC_SEQ = 4096
PAST_LEN = 128

N_MIXERS = 2
FOURIER_GROUPS = 8
FOURIER_GROUP_DIM = D_MODEL // FOURIER_GROUPS
N_HEADS = 8
HEAD_DIM = D_MODEL // N_HEADS // 2
V_DIM = 2 * HEAD_DIM
N_BUCKETS = 32
MAX_DISTANCE = 128
N_EXPERTS = 16
CAPACITY_FACTOR = 2
D_FF = ((8 * D_MODEL // 3 + 127) // 128) * 128
Q_BLOCK = 128
EPS = 1e-6
N_FOURIER_LAYERS = (DEPTH + 1) // 2
N_ATTN_LAYERS = DEPTH // 2

kernel_name = "hybrid_fnet_diffattn_ec_moe_encoder"


def rms_norm(x, g):
    xf = x.astype(jnp.float32)
    y = xf * lax.rsqrt(jnp.mean(xf * xf, axis=-1, keepdims=True) + EPS)
    return (y * g.astype(jnp.float32)).astype(x.dtype)


def fourier_mixer(h, w_out):
    B, S, D = h.shape
    hg = h.astype(jnp.float32).reshape(B, S, FOURIER_GROUPS, FOURIER_GROUP_DIM)
    mixed = jnp.fft.fftn(hg, axes=(1, 3), norm="ortho").real
    mixed = mixed.reshape(B, S, D).astype(h.dtype)
    return mixed @ w_out


def t5_bucket(rel):
    nb = N_BUCKETS // 2
    max_exact = nb // 2
    ret = jnp.where(rel > 0, nb, 0)
    n = jnp.abs(rel)
    nf = jnp.maximum(n, 1).astype(jnp.float32)
    large = max_exact + (jnp.log(nf / max_exact) / math.log(MAX_DISTANCE / max_exact)
                         * (nb - max_exact)).astype(jnp.int32)
    large = jnp.minimum(large, nb - 1)
    return ret + jnp.where(n < max_exact, n, large)


def diff_attention(h, w_qkv, w_o, g_q, g_k, lq1, lk1, lq2, lk2, g_sub, rel_table, lambda_init):
    B, S, D = h.shape
    qkv = h @ w_qkv
    q, k, v = jnp.split(qkv, 3, axis=-1)
    q = rms_norm(q.reshape(B, S, 2 * N_HEADS, HEAD_DIM), g_q)
    k = rms_norm(k.reshape(B, S, 2 * N_HEADS, HEAD_DIM), g_k)
    v = v.reshape(B, S, N_HEADS, V_DIM)
    lam = (jnp.exp(jnp.sum(lq1.astype(jnp.float32) * lk1.astype(jnp.float32)))
           - jnp.exp(jnp.sum(lq2.astype(jnp.float32) * lk2.astype(jnp.float32)))
           + lambda_init)
    scale = HEAD_DIM ** -0.5
    n_blocks = S // Q_BLOCK
    qb = q.reshape(B, n_blocks, Q_BLOCK, 2 * N_HEADS, HEAD_DIM).transpose(1, 0, 2, 3, 4)
    k_pos = jnp.arange(S, dtype=jnp.int32)
    starts = jnp.arange(n_blocks, dtype=jnp.int32) * Q_BLOCK

    def block(args):
        q_blk, start = args
        q_pos = start + jnp.arange(Q_BLOCK, dtype=jnp.int32)
        bias = rel_table[t5_bucket(k_pos[None, :] - q_pos[:, None])]
        bias = bias.transpose(2, 0, 1).astype(jnp.float32)
        s = jnp.einsum('bqhd,bkhd->bhqk', q_blk, k,
                       preferred_element_type=jnp.float32) * scale
        s = s.reshape(B, N_HEADS, 2, Q_BLOCK, S) + bias[None, :, None]
        p = jax.nn.softmax(s, axis=-1)
        a = p[:, :, 0] - lam * p[:, :, 1]
        return jnp.einsum('bhqk,bkhv->bqhv', a.astype(v.dtype), v)

    o = lax.map(block, (qb, starts))
    o = o.transpose(1, 0, 2, 3, 4).reshape(B, S, N_HEADS, V_DIM)
    o = rms_norm(o, g_sub) * (1.0 - lambda_init)
    return o.reshape(B, S, D) @ w_o


def expert_choice_moe(h, w_router, w_gate, w_up, w_down):
    B, S, D = h.shape
    N = B * S
    C = CAPACITY_FACTOR * N // N_EXPERTS
    t = h.reshape(N, D)
    aff = jax.nn.softmax((t @ w_router).astype(jnp.float32), axis=-1)
    gates, idx = lax.top_k(aff.T, C)
    xe = t[idx]
    hid = jax.nn.silu(jnp.einsum('ecd,edf->ecf', xe, w_gate)) * jnp.einsum('ecd,edf->ecf', xe, w_up)
    ye = jnp.einsum('ecf,efd->ecd', hid, w_down) * gates[..., None].astype(h.dtype)
    out = jnp.zeros((N, D), h.dtype).at[idx.reshape(-1)].add(ye.reshape(-1, D))
    return out.reshape(B, S, D)


def trunk(x, rel_bias, g_mix, g_ffn, w_fourier_out, w_qkv, w_attn_out, g_q, g_k,
          lambda_q1, lambda_k1, lambda_q2, lambda_k2, g_sub, w_router, w_gate, w_up, w_down):
    h = x
    for i in range(DEPTH):
        hn = rms_norm(h, g_mix[i])
        j = i // N_MIXERS
        if i % N_MIXERS == 0:
            h = h + fourier_mixer(hn, w_fourier_out[j])
        else:
            lambda_init = 0.8 - 0.6 * math.exp(-0.3 * i)
            h = h + diff_attention(hn, w_qkv[j], w_attn_out[j], g_q[j], g_k[j],
                                   lambda_q1[j], lambda_k1[j], lambda_q2[j], lambda_k2[j],
                                   g_sub[j], rel_bias, lambda_init)
        h = h + expert_choice_moe(rms_norm(h, g_ffn[i]), w_router[i], w_gate[i], w_up[i], w_down[i])
    return h


def setup_inputs(seed: int = 0) -> dict:
    key = jax.random.key(seed)
    ks = jax.random.split(key, 20)

    def nrm(k, shape, scale):
        return jax.random.normal(k, shape, jnp.float32) * scale

    D = D_MODEL
    return {
        "x_prompt": nrm(ks[0], (BATCH, SEQ, D), 1.0),
        "x_sample": nrm(ks[1], (DEC_BATCH, DEC_SEQ, D), 1.0),
        "rel_bias": nrm(ks[2], (N_BUCKETS, N_HEADS), 0.1),
        "g_mix": 1.0 + nrm(ks[3], (DEPTH, D), 0.02),
        "g_ffn": 1.0 + nrm(ks[4], (DEPTH, D), 0.02),
        "w_fourier_out": nrm(ks[5], (N_FOURIER_LAYERS, D, D), D ** -0.5),
        "w_qkv": nrm(ks[6], (N_ATTN_LAYERS, D, 3 * D), D ** -0.5),
        "w_attn_out": nrm(ks[7], (N_ATTN_LAYERS, D, D), D ** -0.5),
        "g_q": 1.0 + nrm(ks[8], (N_ATTN_LAYERS, HEAD_DIM), 0.02),
        "g_k": 1.0 + nrm(ks[9], (N_ATTN_LAYERS, HEAD_DIM), 0.02),
        "lambda_q1": nrm(ks[10], (N_ATTN_LAYERS, HEAD_DIM), 0.1),
        "lambda_k1": nrm(ks[11], (N_ATTN_LAYERS, HEAD_DIM), 0.1),
        "lambda_q2": nrm(ks[12], (N_ATTN_LAYERS, HEAD_DIM), 0.1),
        "lambda_k2": nrm(ks[13], (N_ATTN_LAYERS, HEAD_DIM), 0.1),
        "g_sub": 1.0 + nrm(ks[14], (N_ATTN_LAYERS, V_DIM), 0.02),
        "w_router": nrm(ks[15], (DEPTH, D, N_EXPERTS), D ** -0.5),
        "w_gate": nrm(ks[16], (DEPTH, N_EXPERTS, D, D_FF), D ** -0.5),
        "w_up": nrm(ks[17], (DEPTH, N_EXPERTS, D, D_FF), D ** -0.5),
        "w_down": nrm(ks[18], (DEPTH, N_EXPERTS, D_FF, D), D_FF ** -0.5),
    }


def reference(x_prompt, x_sample, rel_bias, g_mix, g_ffn, w_fourier_out, w_qkv, w_attn_out,
              g_q, g_k, lambda_q1, lambda_k1, lambda_q2, lambda_k2, g_sub,
              w_router, w_gate, w_up, w_down):
    y_prompt = trunk(x_prompt, rel_bias, g_mix, g_ffn, w_fourier_out, w_qkv, w_attn_out,
                     g_q, g_k, lambda_q1, lambda_k1, lambda_q2, lambda_k2, g_sub,
                     w_router, w_gate, w_up, w_down)
    y_sample = trunk(x_sample, rel_bias, g_mix, g_ffn, w_fourier_out, w_qkv, w_attn_out,
                     g_q, g_k, lambda_q1, lambda_k1, lambda_q2, lambda_k2, g_sub,
                     w_router, w_gate, w_up, w_down)
    return (y_prompt, y_sample)
```

```python
import functools
import math

import numpy as np
import jax
import jax.numpy as jnp
from jax import lax
from jax.experimental import pallas as pl
from jax.experimental.pallas import tpu as pltpu

EPS = 1e-6
F32 = jnp.float32
BF16 = jnp.bfloat16
I32 = jnp.int32
HIGHEST = lax.Precision.HIGHEST

LANES = 128
FOURIER_GROUP_DIM = 128
HEAD_BLOCK = 128
HEAD_DIM = 64
N_EXPERTS = 16
CAPACITY_FACTOR = 2
N_BUCKETS = 32
MAX_DISTANCE = 128
VMEM_LIMIT = 56 * 1024 * 1024


def _cparams(sem):
    return pltpu.CompilerParams(dimension_semantics=sem, vmem_limit_bytes=VMEM_LIMIT)


def _rms(x, g):
    return x * lax.rsqrt(jnp.mean(x * x, axis=-1, keepdims=True) + EPS) * g


def _dot(a, b):
    return jnp.dot(a, b, preferred_element_type=F32)


def _cos_sin(n):
    k = np.arange(n)
    ang = 2.0 * np.pi * ((k[:, None] * k[None, :]) % n) / n
    return np.cos(ang), np.sin(ang)


def _fold_w_kernel(cc_ref, sc_ref, w_ref, wa_ref, wb_ref):
    w = w_ref[...]
    wa_ref[...] = jnp.dot(cc_ref[...], w, precision=HIGHEST, preferred_element_type=F32).astype(BF16)
    wb_ref[...] = jnp.dot(sc_ref[...], w, precision=HIGHEST, preferred_element_type=F32).astype(BF16)


def _fold_w(w):
    d = w.shape[0]
    gd = FOURIER_GROUP_DIM
    c, s = _cos_sin(gd)
    cc = jnp.asarray(c / math.sqrt(gd), F32)
    sc = jnp.asarray(s / math.sqrt(gd), F32)
    const = pl.BlockSpec((gd, gd), lambda g: (0, 0))
    blk = pl.BlockSpec((gd, d), lambda g: (g, 0))
    return pl.pallas_call(
        _fold_w_kernel,
        grid=(d // gd,),
        in_specs=[const, const, blk],
        out_specs=[blk, blk],
        out_shape=[jax.ShapeDtypeStruct((d, d), BF16)] * 2,
        compiler_params=_cparams(("arbitrary",)),
    )(cc, sc, w)


def _fourier_a_kernel(x_ref, g_ref, f1_ref, twr_ref, twi_ref, yr_ref, yi_ref, *, nb, d, n1):
    j = pl.program_id(1)
    lane = lax.broadcasted_iota(I32, twr_ref.shape, 1)
    for c in range(nb):
        cols = slice(c * d, (c + 1) * d)
        xn = _rms(x_ref[:, cols], g_ref[...]).astype(BF16)
        y = _dot(f1_ref[...], xn)
        yr, yi = y[:n1], y[n1:]
        sel = lane == (j * nb + c)
        tr = jnp.sum(jnp.where(sel, twr_ref[...], 0.0), axis=1, keepdims=True)
        ti = jnp.sum(jnp.where(sel, twi_ref[...], 0.0), axis=1, keepdims=True)
        yr_ref[:, cols] = (yr * tr - yi * ti).astype(BF16)
        yi_ref[:, cols] = (yr * ti + yi * tr).astype(BF16)


def _fourier_b_kernel(yr_ref, yi_ref, f2_ref, wa_ref, wb_ref, x_ref, o_ref, *, kb, n2, d):
    ars, ais = [], []
    for c in range(kb):
        rows = slice(c * n2, (c + 1) * n2)
        ys = jnp.concatenate([yr_ref[rows, :], yi_ref[rows, :]], axis=0)
        a = _dot(f2_ref[...], ys)
        ars.append(a[:n2].astype(BF16))
        ais.append(a[n2:].astype(BF16))
    ar = jnp.concatenate(ars, axis=0)
    ai = jnp.concatenate(ais, axis=0)
    out = _dot(ar, wa_ref[...]) + _dot(ai, wb_ref[...])
    for c in range(kb):
        cols = slice(c * d, (c + 1) * d)
        o_ref[:, cols] = x_ref[:, cols] + out[c * n2:(c + 1) * n2]


def _split_len(s):
    n1 = 1 << (int(math.log2(s)) // 2)
    assert s % n1 == 0
    return n1, s // n1


def _fourier_layer(x, g, wa, wb):
    b, s, d = x.shape
    n1, n2 = _split_len(s)
    c1, s1 = _cos_sin(n1)
    c2, s2 = _cos_sin(n2)
    f1 = jnp.asarray(np.concatenate([c1, -s1], axis=0) / math.sqrt(n1), BF16)
    f2 = jnp.asarray(np.block([[c2, s2], [-s2, c2]]) / math.sqrt(n2), BF16)
    ang = 2.0 * np.pi * ((np.arange(n1)[:, None] * np.arange(n2)[None, :]) % s) / s
    twr = jnp.asarray(np.cos(ang), F32)
    twi = jnp.asarray(-np.sin(ang), F32)

    nb = min(4, n2)
    xa = x.reshape(b, n1, n2 * d)
    blk_a = pl.BlockSpec((None, n1, nb * d), lambda bi, j: (bi, 0, j))
    full = lambda shape: pl.BlockSpec(shape, lambda bi, j: (0,) * len(shape))
    yr, yi = pl.pallas_call(
        functools.partial(_fourier_a_kernel, nb=nb, d=d, n1=n1),
        grid=(b, n2 // nb),
        in_specs=[blk_a, full((1, d)), full((2 * n1, n1)), full((n1, n2)), full((n1, n2))],
        out_specs=[blk_a, blk_a],
        out_shape=[jax.ShapeDtypeStruct((b, n1, n2 * d), BF16)] * 2,
        compiler_params=_cparams(("parallel", "parallel")),
    )(xa, g.reshape(1, d), f1, twr, twi)

    kb = min(512 // n2, n1)
    yr = yr.reshape(b, s, d)
    yi = yi.reshape(b, s, d)
    xb = x.reshape(b, n2, n1 * d)
    blk_y = pl.BlockSpec((None, kb * n2, d), lambda bi, i: (bi, i, 0))
    blk_x = pl.BlockSpec((None, n2, kb * d), lambda bi, i: (bi, 0, i))
    out = pl.pallas_call(
        functools.partial(_fourier_b_kernel, kb=kb, n2=n2, d=d),
        grid=(b, n1 // kb),
        in_specs=[blk_y, blk_y, full((2 * n2, 2 * n2)), full((d, d)), full((d, d)), blk_x],
        out_specs=blk_x,
        out_shape=jax.ShapeDtypeStruct((b, n2, n1 * d), F32),
        compiler_params=_cparams(("parallel", "parallel")),
    )(yr, yi, f2, wa, wb, xb)
    return out.reshape(b, s, d)


def _router_kernel(h_ref, g_ref, wrt_ref, hn_ref, aff_ref):
    hn = _rms(h_ref[...], g_ref[...])
    hn_ref[...] = hn.astype(BF16)
    logits = lax.dot_general(wrt_ref[...], hn, (((1,), (1,)), ((), ())),
                             precision=HIGHEST, preferred_element_type=F32)
    ex = jnp.exp(logits - jnp.max(logits, axis=0, keepdims=True))
    aff_ref[...] = ex / jnp.sum(ex, axis=0, keepdims=True)


def _select_kernel(a_ref, pos_ref, cum_ref, *, e, nb, k):
    keys = lax.bitcast_convert_type(a_ref[...], I32).reshape(e, nb, LANES)

    def count(mask):
        part = jnp.sum(mask.astype(F32), axis=2, keepdims=True)
        return jnp.sum(part, axis=1, keepdims=True)

    def value_step(i, t):
        cand = t | (jnp.int32(1) << (30 - i))
        return jnp.where(count(keys >= cand) >= k, cand, t)

    thr = lax.fori_loop(0, 31, value_step, jnp.zeros((e, 1, 1), I32))
    gt = keys > thr
    eq = keys == thr
    need = k - count(gt)
    tok = (lax.broadcasted_iota(I32, (e, nb, LANES), 1) * LANES
           + lax.broadcasted_iota(I32, (e, nb, LANES), 2))
    nbits = (nb * LANES - 1).bit_length()

    def index_step(i, v):
        cand = v | (jnp.int32(1) << (nbits - 1 - i))
        return jnp.where(count(eq & (tok < cand)) < need, cand, v)

    last = lax.fori_loop(0, nbits, index_step, jnp.zeros((e, 1, 1), I32))
    sel = (gt | (eq & (tok <= last))).reshape(e * nb, LANES)
    self32 = sel.astype(F32)

    r = lax.broadcasted_iota(I32, (LANES, LANES), 0)
    c = lax.broadcasted_iota(I32, (LANES, LANES), 1)
    inc = _dot(self32.astype(BF16), (r <= c).astype(BF16))
    tot = jnp.broadcast_to(inc[:, LANES - 1:LANES], (e * nb, LANES)).astype(BF16)
    rb = lax.broadcasted_iota(I32, (nb, nb), 0)
    cb = lax.broadcasted_iota(I32, (nb, nb), 1)
    lower = (cb < rb).astype(BF16)
    before = jnp.concatenate([_dot(lower, tot[x * nb:(x + 1) * nb]) for x in range(e)], axis=0)
    pos = inc - self32 + before
    pos_ref[...] = jnp.where(sel, pos, -1.0).astype(I32)
    cum_ref[...] = before.astype(I32)


def _one_hot_mask(pos_ref, row_tile, rt):
    tt = pos_ref.shape[-1]
    rows = lax.broadcasted_iota(I32, (rt, tt), 0) + row_tile * rt
    return pos_ref[...] == rows


def _gather_kernel(se, sr, st, sf, sv, hn_ref, pos_ref, aff_ref, xe_ref, gc_ref, *, rt):
    s = pl.program_id(0)

    @pl.when(sv[s] == 1)
    def _():
        mask = _one_hot_mask(pos_ref, sr[s], rt)
        rows = _dot(jnp.where(mask, 1.0, 0.0).astype(BF16), hn_ref[...])
        gate = jnp.sum(jnp.where(mask, aff_ref[...], 0.0), axis=1, keepdims=True)

        @pl.when(sf[s] == 1)
        def _():
            xe_ref[...] = rows.astype(BF16)
            gc_ref[...] = gate

        @pl.when(sf[s] == 0)
        def _():
            xe_ref[...] = (xe_ref[...].astype(F32) + rows).astype(BF16)
            gc_ref[...] += gate


def _ffn_kernel(xe_ref, wg_ref, wu_ref, wd_ref, gc_ref, ye_ref, acc_ref):
    j = pl.program_id(2)
    x = xe_ref[...]
    g = _dot(x, wg_ref[...].astype(BF16))
    u = _dot(x, wu_ref[...].astype(BF16))
    hid = (g / (1.0 + jnp.exp(-g))) * u
    part = _dot(hid.astype(BF16), wd_ref[...].astype(BF16))

    @pl.when(j == 0)
    def _():
        acc_ref[...] = part

    @pl.when(j > 0)
    def _():
        acc_ref[...] += part

    @pl.when(j == pl.num_programs(2) - 1)
    def _():
        ye_ref[...] = (acc_ref[...] * gc_ref[...]).astype(BF16)


def _combine_kernel(ce, cr, ct, cf, cv, h_ref, pos_ref, ye_ref, o_ref, *, rt):
    s = pl.program_id(0)

    @pl.when(cv[s] == 1)
    def _():
        mask = _one_hot_mask(pos_ref, cr[s], rt)
        back = lax.dot_general(jnp.where(mask, 1.0, 0.0).astype(BF16), ye_ref[...],
                               (((0,), (0,)), ((), ())), preferred_element_type=F32)

        @pl.when(cf[s] == 1)
        def _():
            o_ref[...] = h_ref[...] + back

        @pl.when(cf[s] == 0)
        def _():
            o_ref[...] += back


def _flat_schedule(inter, n_steps):
    flat = inter.reshape(-1)
    idx = jnp.nonzero(flat, size=n_steps, fill_value=0)[0].astype(I32)
    n_valid = jnp.sum(flat.astype(I32))
    step = jnp.arange(n_steps, dtype=I32)
    valid = step < n_valid
    idx = jnp.where(valid, idx, idx[jnp.maximum(n_valid - 1, 0)])
    d1, d2 = inter.shape[1], inter.shape[2]
    return idx // (d1 * d2), (idx // d2) % d1, idx % d2, valid.astype(I32)


def _first_flags(valid, *keys):
    changed = jnp.zeros(valid.shape, bool).at[0].set(True)
    for key in keys:
        changed = changed | jnp.concatenate([jnp.ones((1,), bool), key[1:] != key[:-1]])
    return (changed & (valid == 1)).astype(I32)


def _moe_layer(h, g, w_router, w_gate, w_up, w_down):
    n, d = h.shape
    e = N_EXPERTS
    cap = CAPACITY_FACTOR * n // e
    f = w_gate.shape[-1]
    nb = n // LANES
    tn = min(1024, n)

    hn, aff = pl.pallas_call(
        _router_kernel,
        grid=(n // tn,),
        in_specs=[pl.BlockSpec((tn, d), lambda i: (i, 0)),
                  pl.BlockSpec((1, d), lambda i: (0, 0)),
                  pl.BlockSpec((e, d), lambda i: (0, 0))],
        out_specs=[pl.BlockSpec((tn, d), lambda i: (i, 0)),
                   pl.BlockSpec((e, tn), lambda i: (0, i))],
        out_shape=[jax.ShapeDtypeStruct((n, d), BF16), jax.ShapeDtypeStruct((e, n), F32)],
        compiler_params=_cparams(("parallel",)),
    )(h, g.reshape(1, d), w_router.T)

    whole = pl.BlockSpec((e * nb, LANES), lambda i: (0, 0))
    pos, cum = pl.pallas_call(
        functools.partial(_select_kernel, e=e, nb=nb, k=cap),
        grid=(1,),
        in_specs=[whole],
        out_specs=[whole, whole],
        out_shape=[jax.ShapeDtypeStruct((e * nb, LANES), I32)] * 2,
        compiler_params=_cparams(("arbitrary",)),
    )(aff.reshape(e * nb, LANES))

    rt = min(256, cap)
    tt = min(1024, n)
    n_r, n_t = cap // rt, n // tt
    lo = cum[:, 0].reshape(e, nb)[:, ::tt // LANES]
    hi = jnp.concatenate([lo[:, 1:], jnp.full((e, 1), cap, I32)], axis=1)
    r_lo = (jnp.arange(n_r, dtype=I32) * rt)[None, :, None]
    inter = (lo[:, None, :] < r_lo + rt) & (hi[:, None, :] > r_lo) & (hi > lo)[:, None, :]
    n_gather = e * (n_r + n_t - 1)
    se, sr, st, sv = _flat_schedule(inter, n_gather)
    sf = _first_flags(sv, se, sr)
    r_any = jnp.clip(lo[0] // rt, 0, n_r - 1)
    forced = (jnp.arange(e)[:, None, None] == 0) & (jnp.arange(n_r)[None, :, None] == r_any[None, None, :])
    n_combine = n_gather + n_t
    ct, ce, cr, cv = _flat_schedule(jnp.transpose(inter | forced, (2, 0, 1)), n_combine)
    cf = _first_flags(cv, ct)

    pos3 = pos.reshape(e * n_t, 1, tt)
    aff3 = aff.reshape(e * n_t, 1, tt)
    xe, gc = pl.pallas_call(
        functools.partial(_gather_kernel, rt=rt),
        grid_spec=pltpu.PrefetchScalarGridSpec(
            num_scalar_prefetch=5,
            grid=(n_gather,),
            in_specs=[pl.BlockSpec((tt, d), lambda s, se, sr, st, sf, sv: (st[s], 0)),
                      pl.BlockSpec((None, 1, tt), lambda s, se, sr, st, sf, sv: (se[s] * n_t + st[s], 0, 0)),
                      pl.BlockSpec((None, 1, tt), lambda s, se, sr, st, sf, sv: (se[s] * n_t + st[s], 0, 0))],
            out_specs=[pl.BlockSpec((None, rt, d), lambda s, se, sr, st, sf, sv: (se[s], sr[s], 0)),
                       pl.BlockSpec((None, rt, 1), lambda s, se, sr, st, sf, sv: (se[s], sr[s], 0))],
        ),
        out_shape=[jax.ShapeDtypeStruct((e, cap, d), BF16), jax.ShapeDtypeStruct((e, cap, 1), F32)],
        compiler_params=_cparams(("arbitrary",)),
    )(se, sr, st, sf, sv, hn, pos3, aff3)

    tm = min(2048, cap)
    tf = 256
    ye = pl.pallas_call(
        _ffn_kernel,
        grid=(e, cap // tm, f // tf),
        in_specs=[pl.BlockSpec((None, tm, d), lambda x, r, j: (x, r, 0)),
                  pl.BlockSpec((None, d, tf), lambda x, r, j: (x, 0, j)),
                  pl.BlockSpec((None, d, tf), lambda x, r, j: (x, 0, j)),
                  pl.BlockSpec((None, tf, d), lambda x, r, j: (x, j, 0)),
                  pl.BlockSpec((None, tm, 1), lambda x, r, j: (x, r, 0))],
        out_specs=pl.BlockSpec((None, tm, d), lambda x, r, j: (x, r, 0)),
        out_shape=jax.ShapeDtypeStruct((e, cap, d), BF16),
        scratch_shapes=[pltpu.VMEM((tm, d), F32)],
        compiler_params=_cparams(("parallel", "parallel", "arbitrary")),
    )(xe, w_gate, w_up, w_down, gc)

    return pl.pallas_call(
        functools.partial(_combine_kernel, rt=rt),
        grid_spec=pltpu.PrefetchScalarGridSpec(
            num_scalar_prefetch=5,
            grid=(n_combine,),
            in_specs=[pl.BlockSpec((tt, d), lambda s, ce, cr, ct, cf, cv: (ct[s], 0)),
                      pl.BlockSpec((None, 1, tt), lambda s, ce, cr, ct, cf, cv: (ce[s] * n_t + ct[s], 0, 0)),
                      pl.BlockSpec((None, rt, d), lambda s, ce, cr, ct, cf, cv: (ce[s], cr[s], 0))],
            out_specs=pl.BlockSpec((tt, d), lambda s, ce, cr, ct, cf, cv: (ct[s], 0)),
        ),
        out_shape=jax.ShapeDtypeStruct((n, d), F32),
        compiler_params=_cparams(("arbitrary",)),
    )(ce, cr, ct, cf, cv, h, pos3, ye)


def _t5_bucket_np(rel):
    nb = N_BUCKETS // 2
    max_exact = nb // 2
    ret = np.where(rel > 0, nb, 0)
    n = np.abs(rel)
    nf = np.maximum(n, 1).astype(np.float64)
    large = max_exact + (np.log(nf / max_exact) / math.log(MAX_DISTANCE / max_exact)
                         * (nb - max_exact)).astype(np.int64)
    large = np.minimum(large, nb - 1)
    return ret + np.where(n < max_exact, n, large)


def _bias_tiles(rel_bias, t):
    assert t + 1 >= MAX_DISTANCE
    kk = np.arange(t)[None, :, None]
    qq = np.arange(t)[None, None, :]
    rel = (np.arange(5)[:, None, None] - 2) * t + kk - qq
    return jnp.transpose(rel_bias[_t5_bucket_np(rel)], (3, 0, 1, 2)).astype(F32)


def _qkv_kernel(h_ref, g_ref, w_ref, bd_ref, gq_ref, gk_ref, qt_ref, k_ref, vt_ref, *, d, nh):
    hn = _rms(h_ref[...], g_ref[...]).astype(BF16)
    qkv = _dot(hn, w_ref[...])
    q, k, v = qkv[:, :d], qkv[:, d:2 * d], qkv[:, 2 * d:]

    def head_norm(x, gain):
        ms = _dot((x * x).astype(BF16), bd_ref[...])
        return x * lax.rsqrt(ms + EPS) * gain

    qn = head_norm(q, gq_ref[...]) * (HEAD_DIM ** -0.5)
    k_ref[...] = head_norm(k, gk_ref[...]).astype(BF16)
    for h in range(nh):
        cols = slice(h * HEAD_BLOCK, (h + 1) * HEAD_BLOCK)
        qt_ref[h] = qn[:, cols].T.astype(BF16)
        vt_ref[h] = v[:, cols].T.astype(BF16)


def _attn_kernel(qt_ref, k_ref, vt_ref, bias_ref, lq1_ref, lk1_ref, lq2_ref, lk2_ref, gs_ref, o_ref,
                 m_ref, l_ref, acc_ref, *, t, n_chunks, lambda_init):
    i = pl.program_id(2)
    j = pl.program_id(3)

    @pl.when(j == 0)
    def _():
        m_ref[...] = jnp.full(m_ref.shape, -1e30, F32)
        l_ref[...] = jnp.zeros(l_ref.shape, F32)
        acc_ref[...] = jnp.zeros(acc_ref.shape, F32)

    qt = qt_ref[...]
    row = lax.broadcasted_iota(I32, qt.shape, 0)
    comps = (jnp.where(row < HEAD_DIM, qt, jnp.zeros_like(qt)),
             jnp.where(row >= HEAD_DIM, qt, jnp.zeros_like(qt)))
    for jj in range(n_chunks):
        rows = slice(jj * t, (jj + 1) * t)
        kc = k_ref[rows, :]
        dl = jnp.clip(j * n_chunks + jj - i, -2, 2) + 2
        bias = bias_ref[dl]
        for c in range(2):
            s = _dot(kc, comps[c]) + bias
            m_old = m_ref[c]
            m_new = jnp.maximum(m_old, jnp.max(s, axis=0, keepdims=True))
            p = jnp.exp(s - m_new)
            alpha = jnp.exp(m_old - m_new)
            l_ref[c] = alpha * l_ref[c] + jnp.sum(p, axis=0, keepdims=True)
            acc_ref[c] = alpha * acc_ref[c] + _dot(vt_ref[:, rows], p.astype(BF16))
            m_ref[c] = m_new

    @pl.when(j == pl.num_programs(3) - 1)
    def _():
        lam = (jnp.exp(jnp.sum(lq1_ref[...] * lk1_ref[...], keepdims=True))
               - jnp.exp(jnp.sum(lq2_ref[...] * lk2_ref[...], keepdims=True)) + lambda_init)
        a = acc_ref[0] / l_ref[0] - lam * (acc_ref[1] / l_ref[1])
        a = a * lax.rsqrt(jnp.mean(a * a, axis=0, keepdims=True) + EPS) * gs_ref[...]
        o_ref[...] = (a * (1.0 - lambda_init)).T.astype(BF16)


def _proj_kernel(o_ref, w_ref, h_ref, out_ref):
    out_ref[...] = h_ref[...] + _dot(o_ref[...], w_ref[...])


def _attention_layer(h, g, w_qkv, w_o, g_q, g_k, lq1, lk1, lq2, lk2, g_sub, bias, lambda_init):
    b, s, d = h.shape
    nh = d // HEAD_BLOCK
    t = bias.shape[-1]
    tn = min(512, s)
    hd = np.arange(d) // HEAD_DIM
    bd = jnp.asarray((hd[:, None] == hd[None, :]) / HEAD_DIM, BF16)
    tile_gain = lambda x: jnp.tile(x, d // HEAD_DIM).reshape(1, d)
    const2 = lambda shape: pl.BlockSpec(shape, lambda bi, i: (0,) * len(shape))
    qt, k, vt = pl.pallas_call(
        functools.partial(_qkv_kernel, d=d, nh=nh),
        grid=(b, s // tn),
        in_specs=[pl.BlockSpec((None, tn, d), lambda bi, i: (bi, i, 0)),
                  const2((1, d)), const2((d, 3 * d)), const2((d, d)), const2((1, d)), const2((1, d))],
        out_specs=[pl.BlockSpec((None, nh, HEAD_BLOCK, tn), lambda bi, i: (bi, 0, 0, i)),
                   pl.BlockSpec((None, tn, d), lambda bi, i: (bi, i, 0)),
                   pl.BlockSpec((None, nh, HEAD_BLOCK, tn), lambda bi, i: (bi, 0, 0, i))],
        out_shape=[jax.ShapeDtypeStruct((b, nh, HEAD_BLOCK, s), BF16),
                   jax.ShapeDtypeStruct((b, s, d), BF16),
                   jax.ShapeDtypeStruct((b, nh, HEAD_BLOCK, s), BF16)],
        compiler_params=_cparams(("parallel", "parallel")),
    )(h, g.reshape(1, d), w_qkv.astype(BF16), bd, tile_gain(g_q), tile_gain(g_k))

    tko = min(2048, s)
    n_chunks = tko // t
    const4 = lambda shape: pl.BlockSpec(shape, lambda bi, hi, i, j: (0,) * len(shape))
    lam_spec = const4((1, HEAD_DIM))
    o = pl.pallas_call(
        functools.partial(_attn_kernel, t=t, n_chunks=n_chunks, lambda_init=lambda_init),
        grid=(b, nh, s // t, s // tko),
        in_specs=[pl.BlockSpec((None, None, HEAD_BLOCK, t), lambda bi, hi, i, j: (bi, hi, 0, i)),
                  pl.BlockSpec((None, tko, HEAD_BLOCK), lambda bi, hi, i, j: (bi, j, hi)),
                  pl.BlockSpec((None, None, HEAD_BLOCK, tko), lambda bi, hi, i, j: (bi, hi, 0, j)),
                  pl.BlockSpec((None, 5, t, t), lambda bi, hi, i, j: (hi, 0, 0, 0)),
                  lam_spec, lam_spec, lam_spec, lam_spec,
                  const4((HEAD_BLOCK, 1))],
        out_specs=pl.BlockSpec((None, t, HEAD_BLOCK), lambda bi, hi, i, j: (bi, i, hi)),
        out_shape=jax.ShapeDtypeStruct((b, s, d), BF16),
        scratch_shapes=[pltpu.VMEM((2, 1, t), F32), pltpu.VMEM((2, 1, t), F32),
                        pltpu.VMEM((2, HEAD_BLOCK, t), F32)],
        compiler_params=_cparams(("parallel", "parallel", "parallel", "arbitrary")),
    )(qt, k, vt, bias, lq1.reshape(1, -1), lk1.reshape(1, -1), lq2.reshape(1, -1), lk2.reshape(1, -1),
      g_sub.reshape(HEAD_BLOCK, 1))

    n = b * s
    tp = min(1024, n)
    out = pl.pallas_call(
        _proj_kernel,
        grid=(n // tp,),
        in_specs=[pl.BlockSpec((tp, d), lambda i: (i, 0)),
                  pl.BlockSpec((d, d), lambda i: (0, 0)),
                  pl.BlockSpec((tp, d), lambda i: (i, 0))],
        out_specs=pl.BlockSpec((tp, d), lambda i: (i, 0)),
        out_shape=jax.ShapeDtypeStruct((n, d), F32),
        compiler_params=_cparams(("parallel",)),
    )(o.reshape(n, d), w_o.astype(BF16), h.reshape(n, d))
    return out.reshape(b, s, d)


def _trunk(x, bias, wa, wb, g_mix, g_ffn, w_qkv, w_attn_out, g_q, g_k, lq1, lk1, lq2, lk2, g_sub,
           w_router, w_gate, w_up, w_down):
    b, s, d = x.shape
    moe = lambda h, i: _moe_layer(h.reshape(b * s, d), g_ffn[i], w_router[i], w_gate[i], w_up[i],
                                  w_down[i]).reshape(b, s, d)
    h = _fourier_layer(x, g_mix[0], wa, wb)
    h = moe(h, 0)
    lambda_init = 0.8 - 0.6 * math.exp(-0.3 * 1)
    h = _attention_layer(h, g_mix[1], w_qkv[0], w_attn_out[0], g_q[0], g_k[0], lq1[0], lk1[0], lq2[0],
                         lk2[0], g_sub[0], bias, lambda_init)
    return moe(h, 1)


def kernel(x_prompt, x_sample, rel_bias, g_mix, g_ffn, w_fourier_out, w_qkv, w_attn_out, g_q, g_k,
           lambda_q1, lambda_k1, lambda_q2, lambda_k2, g_sub, w_router, w_gate, w_up, w_down):
    assert g_mix.shape[0] == 2, "one Fourier layer followed by one attention layer"
    wa, wb = _fold_w(w_fourier_out[0])
    t = min(512, x_prompt.shape[1], x_sample.shape[1])
    bias = _bias_tiles(rel_bias, t)
    run = lambda x: _trunk(x, bias, wa, wb, g_mix, g_ffn, w_qkv, w_attn_out, g_q, g_k, lambda_q1, lambda_k1,
                           lambda_q2, lambda_k2, g_sub, w_router, w_gate, w_up, w_down)
    return (run(x_prompt), run(x_sample))
```

```python
import functools
import math

import numpy as np
import jax
import jax.numpy as jnp
from jax import lax
from jax.experimental import pallas as pl
from jax.experimental.pallas import tpu as pltpu

EPS = 1e-6
F32 = jnp.float32
BF16 = jnp.bfloat16
I32 = jnp.int32
HIGHEST = lax.Precision.HIGHEST
LOG2E = math.log2(math.e)

LANES = 128
FOURIER_GROUP_DIM = 128
HEAD_BLOCK = 128
HEAD_DIM = 64
ONES_ROWS = 16
N_EXPERTS = 16
CAPACITY_FACTOR = 2
N_BUCKETS = 32
MAX_DISTANCE = 128
VMEM_LIMIT = 56 * 1024 * 1024


def _cparams(sem):
    return pltpu.CompilerParams(dimension_semantics=sem, vmem_limit_bytes=VMEM_LIMIT)


def _rms(x, g):
    return x * lax.rsqrt(jnp.mean(x * x, axis=-1, keepdims=True) + EPS) * g


def _dot(a, b):
    return jnp.dot(a, b, preferred_element_type=F32)


def _cos_sin(n):
    k = np.arange(n)
    ang = 2.0 * np.pi * ((k[:, None] * k[None, :]) % n) / n
    return np.cos(ang), np.sin(ang)


def _fold_w_kernel(cc_ref, sc_ref, w_ref, wa_ref, wb_ref):
    w = w_ref[...]
    wa_ref[...] = jnp.dot(cc_ref[...], w, precision=HIGHEST, preferred_element_type=F32).astype(BF16)
    wb_ref[...] = jnp.dot(sc_ref[...], w, precision=HIGHEST, preferred_element_type=F32).astype(BF16)


def _fold_w(w):
    d = w.shape[0]
    gd = FOURIER_GROUP_DIM
    c, s = _cos_sin(gd)
    cc = jnp.asarray(c / math.sqrt(gd), F32)
    sc = jnp.asarray(s / math.sqrt(gd), F32)
    const = pl.BlockSpec((gd, gd), lambda g: (0, 0))
    blk = pl.BlockSpec((gd, d), lambda g: (g, 0))
    return pl.pallas_call(
        _fold_w_kernel,
        grid=(d // gd,),
        in_specs=[const, const, blk],
        out_specs=[blk, blk],
        out_shape=[jax.ShapeDtypeStruct((d, d), BF16)] * 2,
        compiler_params=_cparams(("arbitrary",)),
    )(cc, sc, w)


def _fourier_a_kernel(x_ref, g_ref, f1_ref, twr_ref, twi_ref, yr_ref, yi_ref, *, nb, d, n1):
    j = pl.program_id(1)
    lane = lax.broadcasted_iota(I32, twr_ref.shape, 1)
    for c in range(nb):
        cols = slice(c * d, (c + 1) * d)
        xn = _rms(x_ref[:, cols], g_ref[...]).astype(BF16)
        y = _dot(f1_ref[...], xn)
        yr, yi = y[:n1], y[n1:]
        sel = lane == (j * nb + c)
        tr = jnp.sum(jnp.where(sel, twr_ref[...], 0.0), axis=1, keepdims=True)
        ti = jnp.sum(jnp.where(sel, twi_ref[...], 0.0), axis=1, keepdims=True)
        yr_ref[:, cols] = (yr * tr - yi * ti).astype(BF16)
        yi_ref[:, cols] = (yr * ti + yi * tr).astype(BF16)


def _fourier_b_kernel(yr_ref, yi_ref, f2_ref, wa_ref, wb_ref, x_ref, o_ref, *, kb, n2, d):
    ars, ais = [], []
    for c in range(kb):
        rows = slice(c * n2, (c + 1) * n2)
        ys = jnp.concatenate([yr_ref[rows, :], yi_ref[rows, :]], axis=0)
        a = _dot(f2_ref[...], ys)
        ars.append(a[:n2].astype(BF16))
        ais.append(a[n2:].astype(BF16))
    ar = jnp.concatenate(ars, axis=0)
    ai = jnp.concatenate(ais, axis=0)
    out = _dot(ar, wa_ref[...]) + _dot(ai, wb_ref[...])
    for c in range(kb):
        cols = slice(c * d, (c + 1) * d)
        o_ref[:, cols] = x_ref[:, cols] + out[c * n2:(c + 1) * n2]


def _split_len(s):
    n1 = 1 << (int(math.log2(s)) // 2)
    assert s % n1 == 0
    return n1, s // n1


def _fourier_layer(x, g, wa, wb):
    b, s, d = x.shape
    n1, n2 = _split_len(s)
    c1, s1 = _cos_sin(n1)
    c2, s2 = _cos_sin(n2)
    f1 = jnp.asarray(np.concatenate([c1, -s1], axis=0) / math.sqrt(n1), BF16)
    f2 = jnp.asarray(np.block([[c2, s2], [-s2, c2]]) / math.sqrt(n2), BF16)
    ang = 2.0 * np.pi * ((np.arange(n1)[:, None] * np.arange(n2)[None, :]) % s) / s
    twr = jnp.asarray(np.cos(ang), F32)
    twi = jnp.asarray(-np.sin(ang), F32)

    nb = min(4, n2)
    xa = x.reshape(b, n1, n2 * d)
    blk_a = pl.BlockSpec((None, n1, nb * d), lambda bi, j: (bi, 0, j))
    full = lambda shape: pl.BlockSpec(shape, lambda bi, j: (0,) * len(shape))
    yr, yi = pl.pallas_call(
        functools.partial(_fourier_a_kernel, nb=nb, d=d, n1=n1),
        grid=(b, n2 // nb),
        in_specs=[blk_a, full((1, d)), full((2 * n1, n1)), full((n1, n2)), full((n1, n2))],
        out_specs=[blk_a, blk_a],
        out_shape=[jax.ShapeDtypeStruct((b, n1, n2 * d), BF16)] * 2,
        compiler_params=_cparams(("parallel", "parallel")),
    )(xa, g.reshape(1, d), f1, twr, twi)

    kb = min(512 // n2, n1)
    yr = yr.reshape(b, s, d)
    yi = yi.reshape(b, s, d)
    xb = x.reshape(b, n2, n1 * d)
    blk_y = pl.BlockSpec((None, kb * n2, d), lambda bi, i: (bi, i, 0))
    blk_x = pl.BlockSpec((None, n2, kb * d), lambda bi, i: (bi, 0, i))
    out = pl.pallas_call(
        functools.partial(_fourier_b_kernel, kb=kb, n2=n2, d=d),
        grid=(b, n1 // kb),
        in_specs=[blk_y, blk_y, full((2 * n2, 2 * n2)), full((d, d)), full((d, d)), blk_x],
        out_specs=blk_x,
        out_shape=jax.ShapeDtypeStruct((b, n2, n1 * d), F32),
        compiler_params=_cparams(("parallel", "parallel")),
    )(yr, yi, f2, wa, wb, xb)
    return out.reshape(b, s, d)


def _router_kernel(h_ref, g_ref, wrt_ref, hn_ref, aff_ref):
    hn = _rms(h_ref[...], g_ref[...])
    hn_ref[...] = hn.astype(BF16)
    logits = lax.dot_general(wrt_ref[...], hn, (((1,), (1,)), ((), ())),
                             precision=HIGHEST, preferred_element_type=F32)
    ex = jnp.exp(logits - jnp.max(logits, axis=0, keepdims=True))
    aff_ref[...] = ex / jnp.sum(ex, axis=0, keepdims=True)


def _select_kernel(a_ref, pos_ref, cum_ref, *, e, nb, k):
    keys = lax.bitcast_convert_type(a_ref[...], I32).reshape(e, nb, LANES)

    def count(mask):
        part = jnp.sum(mask.astype(F32), axis=2, keepdims=True)
        return jnp.sum(part, axis=1, keepdims=True)

    def value_step(i, t):
        cand = t | (jnp.int32(1) << (30 - i))
        return jnp.where(count(keys >= cand) >= k, cand, t)

    thr = lax.fori_loop(0, 31, value_step, jnp.zeros((e, 1, 1), I32))
    gt = keys > thr
    eq = keys == thr
    need = k - count(gt)
    tok = (lax.broadcasted_iota(I32, (e, nb, LANES), 1) * LANES
           + lax.broadcasted_iota(I32, (e, nb, LANES), 2))
    nbits = (nb * LANES - 1).bit_length()

    def index_step(i, v):
        cand = v | (jnp.int32(1) << (nbits - 1 - i))
        return jnp.where(count(eq & (tok < cand)) < need, cand, v)

    last = lax.fori_loop(0, nbits, index_step, jnp.zeros((e, 1, 1), I32))
    sel = (gt | (eq & (tok <= last))).reshape(e * nb, LANES)
    self32 = sel.astype(F32)

    r = lax.broadcasted_iota(I32, (LANES, LANES), 0)
    c = lax.broadcasted_iota(I32, (LANES, LANES), 1)
    inc = _dot(self32.astype(BF16), (r <= c).astype(BF16))
    tot = jnp.broadcast_to(inc[:, LANES - 1:LANES], (e * nb, LANES)).astype(BF16)
    rb = lax.broadcasted_iota(I32, (nb, nb), 0)
    cb = lax.broadcasted_iota(I32, (nb, nb), 1)
    lower = (cb < rb).astype(BF16)
    before = jnp.concatenate([_dot(lower, tot[x * nb:(x + 1) * nb]) for x in range(e)], axis=0)
    pos = inc - self32 + before
    pos_ref[...] = jnp.where(sel, pos, -1.0).astype(I32)
    cum_ref[...] = before.astype(I32)


def _one_hot_mask(pos_ref, row_tile, rt):
    tt = pos_ref.shape[-1]
    rows = lax.broadcasted_iota(I32, (rt, tt), 0) + row_tile * rt
    return pos_ref[...] == rows


def _gather_kernel(se, sr, st, sf, sv, hn_ref, pos_ref, aff_ref, xe_ref, gc_ref, *, rt):
    s = pl.program_id(0)

    @pl.when(sv[s] == 1)
    def _():
        mask = _one_hot_mask(pos_ref, sr[s], rt)
        rows = _dot(jnp.where(mask, 1.0, 0.0).astype(BF16), hn_ref[...])
        gate = jnp.sum(jnp.where(mask, aff_ref[...], 0.0), axis=1, keepdims=True)

        @pl.when(sf[s] == 1)
        def _():
            xe_ref[...] = rows.astype(BF16)
            gc_ref[...] = gate

        @pl.when(sf[s] == 0)
        def _():
            xe_ref[...] = (xe_ref[...].astype(F32) + rows).astype(BF16)
            gc_ref[...] += gate


def _ffn_kernel(xe_ref, wg_ref, wu_ref, wd_ref, gc_ref, ye_ref, acc_ref):
    j = pl.program_id(2)
    x = xe_ref[...]
    g = _dot(x, wg_ref[...].astype(BF16))
    u = _dot(x, wu_ref[...].astype(BF16))
    hid = (g / (1.0 + jnp.exp(-g))) * u
    part = _dot(hid.astype(BF16), wd_ref[...].astype(BF16))

    @pl.when(j == 0)
    def _():
        acc_ref[...] = part

    @pl.when(j > 0)
    def _():
        acc_ref[...] += part

    @pl.when(j == pl.num_programs(2) - 1)
    def _():
        ye_ref[...] = (acc_ref[...] * gc_ref[...]).astype(BF16)


def _combine_kernel(ce, cr, ct, cf, cv, h_ref, pos_ref, ye_ref, o_ref, *, rt):
    s = pl.program_id(0)

    @pl.when(cv[s] == 1)
    def _():
        mask = _one_hot_mask(pos_ref, cr[s], rt)
        back = lax.dot_general(jnp.where(mask, 1.0, 0.0).astype(BF16), ye_ref[...],
                               (((0,), (0,)), ((), ())), preferred_element_type=F32)

        @pl.when(cf[s] == 1)
        def _():
            o_ref[...] = h_ref[...] + back

        @pl.when(cf[s] == 0)
        def _():
            o_ref[...] += back


def _flat_schedule(inter, n_steps):
    flat = inter.reshape(-1)
    idx = jnp.nonzero(flat, size=n_steps, fill_value=0)[0].astype(I32)
    n_valid = jnp.sum(flat.astype(I32))
    step = jnp.arange(n_steps, dtype=I32)
    valid = step < n_valid
    idx = jnp.where(valid, idx, idx[jnp.maximum(n_valid - 1, 0)])
    d1, d2 = inter.shape[1], inter.shape[2]
    return idx // (d1 * d2), (idx // d2) % d1, idx % d2, valid.astype(I32)


def _first_flags(valid, *keys):
    changed = jnp.zeros(valid.shape, bool).at[0].set(True)
    for key in keys:
        changed = changed | jnp.concatenate([jnp.ones((1,), bool), key[1:] != key[:-1]])
    return (changed & (valid == 1)).astype(I32)


def _moe_layer(h, g, w_router, w_gate, w_up, w_down, layer):
    n, d = h.shape
    e = N_EXPERTS
    cap = CAPACITY_FACTOR * n // e
    f = w_gate.shape[-1]
    nb = n // LANES
    tn = min(1024, n)

    hn, aff = pl.pallas_call(
        _router_kernel,
        grid=(n // tn,),
        in_specs=[pl.BlockSpec((tn, d), lambda i: (i, 0)),
                  pl.BlockSpec((1, d), lambda i: (0, 0)),
                  pl.BlockSpec((e, d), lambda i: (0, 0))],
        out_specs=[pl.BlockSpec((tn, d), lambda i: (i, 0)),
                   pl.BlockSpec((e, tn), lambda i: (0, i))],
        out_shape=[jax.ShapeDtypeStruct((n, d), BF16), jax.ShapeDtypeStruct((e, n), F32)],
        compiler_params=_cparams(("parallel",)),
    )(h, g.reshape(1, d), w_router.T)

    whole = pl.BlockSpec((e * nb, LANES), lambda i: (0, 0))
    pos, cum = pl.pallas_call(
        functools.partial(_select_kernel, e=e, nb=nb, k=cap),
        grid=(1,),
        in_specs=[whole],
        out_specs=[whole, whole],
        out_shape=[jax.ShapeDtypeStruct((e * nb, LANES), I32)] * 2,
        compiler_params=_cparams(("arbitrary",)),
    )(aff.reshape(e * nb, LANES))

    rt = min(256, cap)
    tt = min(1024, n)
    n_r, n_t = cap // rt, n // tt
    lo = cum[:, 0].reshape(e, nb)[:, ::tt // LANES]
    hi = jnp.concatenate([lo[:, 1:], jnp.full((e, 1), cap, I32)], axis=1)
    r_lo = (jnp.arange(n_r, dtype=I32) * rt)[None, :, None]
    inter = (lo[:, None, :] < r_lo + rt) & (hi[:, None, :] > r_lo) & (hi > lo)[:, None, :]
    n_gather = e * (n_r + n_t - 1)
    se, sr, st, sv = _flat_schedule(inter, n_gather)
    sf = _first_flags(sv, se, sr)
    r_any = jnp.clip(lo[0] // rt, 0, n_r - 1)
    forced = (jnp.arange(e)[:, None, None] == 0) & (jnp.arange(n_r)[None, :, None] == r_any[None, None, :])
    n_combine = n_gather + n_t
    ct, ce, cr, cv = _flat_schedule(jnp.transpose(inter | forced, (2, 0, 1)), n_combine)
    cf = _first_flags(cv, ct)

    pos3 = pos.reshape(e * n_t, 1, tt)
    aff3 = aff.reshape(e * n_t, 1, tt)
    xe, gc = pl.pallas_call(
        functools.partial(_gather_kernel, rt=rt),
        grid_spec=pltpu.PrefetchScalarGridSpec(
            num_scalar_prefetch=5,
            grid=(n_gather,),
            in_specs=[pl.BlockSpec((tt, d), lambda s, se, sr, st, sf, sv: (st[s], 0)),
                      pl.BlockSpec((None, 1, tt), lambda s, se, sr, st, sf, sv: (se[s] * n_t + st[s], 0, 0)),
                      pl.BlockSpec((None, 1, tt), lambda s, se, sr, st, sf, sv: (se[s] * n_t + st[s], 0, 0))],
            out_specs=[pl.BlockSpec((None, rt, d), lambda s, se, sr, st, sf, sv: (se[s], sr[s], 0)),
                       pl.BlockSpec((None, rt, 1), lambda s, se, sr, st, sf, sv: (se[s], sr[s], 0))],
        ),
        out_shape=[jax.ShapeDtypeStruct((e, cap, d), BF16), jax.ShapeDtypeStruct((e, cap, 1), F32)],
        compiler_params=_cparams(("arbitrary",)),
    )(se, sr, st, sf, sv, hn, pos3, aff3)

    tm = min(2048, cap)
    tf = 256
    ye = pl.pallas_call(
        _ffn_kernel,
        grid=(e, cap // tm, f // tf),
        in_specs=[pl.BlockSpec((None, tm, d), lambda x, r, j: (x, r, 0)),
                  pl.BlockSpec((None, None, d, tf), lambda x, r, j: (layer, x, 0, j)),
                  pl.BlockSpec((None, None, d, tf), lambda x, r, j: (layer, x, 0, j)),
                  pl.BlockSpec((None, None, tf, d), lambda x, r, j: (layer, x, j, 0)),
                  pl.BlockSpec((None, tm, 1), lambda x, r, j: (x, r, 0))],
        out_specs=pl.BlockSpec((None, tm, d), lambda x, r, j: (x, r, 0)),
        out_shape=jax.ShapeDtypeStruct((e, cap, d), BF16),
        scratch_shapes=[pltpu.VMEM((tm, d), F32)],
        compiler_params=_cparams(("parallel", "parallel", "arbitrary")),
    )(xe, w_gate, w_up, w_down, gc)

    return pl.pallas_call(
        functools.partial(_combine_kernel, rt=rt),
        grid_spec=pltpu.PrefetchScalarGridSpec(
            num_scalar_prefetch=5,
            grid=(n_combine,),
            in_specs=[pl.BlockSpec((tt, d), lambda s, ce, cr, ct, cf, cv: (ct[s], 0)),
                      pl.BlockSpec((None, 1, tt), lambda s, ce, cr, ct, cf, cv: (ce[s] * n_t + ct[s], 0, 0)),
                      pl.BlockSpec((None, rt, d), lambda s, ce, cr, ct, cf, cv: (ce[s], cr[s], 0))],
            out_specs=pl.BlockSpec((tt, d), lambda s, ce, cr, ct, cf, cv: (ct[s], 0)),
        ),
        out_shape=jax.ShapeDtypeStruct((n, d), F32),
        compiler_params=_cparams(("arbitrary",)),
    )(ce, cr, ct, cf, cv, h, pos3, ye)


def _t5_bucket_np(rel):
    nb = N_BUCKETS // 2
    max_exact = nb // 2
    ret = np.where(rel > 0, nb, 0)
    n = np.abs(rel)
    nf = np.maximum(n, 1).astype(np.float64)
    large = max_exact + (np.log(nf / max_exact) / math.log(MAX_DISTANCE / max_exact)
                         * (nb - max_exact)).astype(np.int64)
    large = np.minimum(large, nb - 1)
    return ret + np.where(n < max_exact, n, large)


def _far_distance():
    n = np.arange(1, 4 * MAX_DISTANCE)
    last = _t5_bucket_np(-n)
    return int(n[np.nonzero(last != last[-1])[0][-1]] + 1)


FAR = _far_distance()


def _bias_band(rel_bias, tq, tc):
    rows = tq + 2 * tc
    period = rows + tq
    r = np.arange(period)
    rel = np.where(r < rows, r, r - period) - tc
    vec = (rel_bias[_t5_bucket_np(rel)] * LOG2E).T
    nh = vec.shape[0]
    skew = jnp.tile(vec, (1, tq))[:, :tq * (period - 1)].reshape(nh, tq, period - 1)
    band = jnp.swapaxes(skew[:, :, :rows], 1, 2).astype(F32)
    return jnp.concatenate([band, jnp.zeros((nh, tc, tq), F32)], axis=1)


def _qkv_kernel(h_ref, g_ref, w_ref, bd_ref, gq_ref, gk_ref, qt_ref, k_ref, vt_ref, *, d, nh):
    hn = _rms(h_ref[...], g_ref[...]).astype(BF16)
    qkv = _dot(hn, w_ref[...])
    q, k, v = qkv[:, :d], qkv[:, d:2 * d], qkv[:, 2 * d:]

    def head_norm(x, gain):
        ms = _dot((x * x).astype(BF16), bd_ref[...])
        return x * lax.rsqrt(ms + EPS) * gain

    qn = head_norm(q, gq_ref[...]) * (HEAD_DIM ** -0.5 * LOG2E)
    k_ref[...] = head_norm(k, gk_ref[...]).astype(BF16)
    for h in range(nh):
        cols = slice(h * HEAD_BLOCK, (h + 1) * HEAD_BLOCK)
        qt_ref[h] = qn[:, cols].T.astype(BF16)
        vt_ref[h] = v[:, cols].T.astype(BF16)


def _attn_kernel(far_ref, qt_ref, k_ref, vt_ref, band_ref, lq1_ref, lk1_ref, lq2_ref, lk2_ref, gs_ref, o_ref,
                 m_ref, l_ref, acc_ref, *, tq, tc, n_chunks, lambda_init):
    h = pl.program_id(1)
    i = pl.program_id(2)
    j = pl.program_id(3)

    @pl.when(j == 0)
    def _():
        m_ref[...] = jnp.full(m_ref.shape, -1e30, F32)
        l_ref[...] = jnp.zeros(l_ref.shape, F32)
        acc_ref[...] = jnp.zeros(acc_ref.shape, F32)

    qt = qt_ref[...]
    row = lax.broadcasted_iota(I32, qt.shape, 0)
    comps = (jnp.where(row < HEAD_DIM, qt, jnp.zeros_like(qt)),
             jnp.where(row >= HEAD_DIM, qt, jnp.zeros_like(qt)))
    q0 = i * tq
    k_first = j * n_chunks * tc
    ones = jnp.ones((ONES_ROWS, tc), BF16)

    def far_side(k0):
        before = k0 + tc - 1 - q0 <= -FAR
        after = k0 - (q0 + tq - 1) >= FAR
        return before, after

    def scores(jj, with_band):
        kc = k_ref[jj * tc:(jj + 1) * tc, :]
        s = [_dot(kc, comps[c]) for c in range(2)]
        if with_band:
            k0 = k_first + jj * tc
            before, after = far_side(k0)
            start = jnp.where(before | after, tq + 2 * tc, k0 - q0 + tc)
            bias = band_ref[pl.ds(pl.multiple_of(start, tc), tc), :]
            s = [x + bias for x in s]
        return s

    def run(with_band):
        state = [(m_ref[c], l_ref[c], acc_ref[c]) for c in range(2)]
        s_next = scores(0, with_band)
        for jj in range(n_chunks):
            s_now = s_next
            if jj + 1 < n_chunks:
                s_next = scores(jj + 1, with_band)
            rows = slice(jj * tc, (jj + 1) * tc)
            before, after = far_side(k_first + jj * tc)
            const = jnp.where(before, far_ref[h, 0], jnp.where(after, far_ref[h, 1], 0.0))
            vt1 = jnp.concatenate([vt_ref[:, rows], ones], axis=0)
            for c in range(2):
                s = s_now[c]
                m_old, l_old, acc_old = state[c]
                m_new = jnp.maximum(m_old, jnp.max(s, axis=0, keepdims=True) + const)
                p = jnp.exp2(s - (m_new - const)).astype(BF16)
                alpha = jnp.exp2(m_old - m_new)
                pv = _dot(vt1, p)
                state[c] = (m_new, alpha * l_old + pv[HEAD_BLOCK:HEAD_BLOCK + 1],
                            alpha * acc_old + pv[:HEAD_BLOCK])
        for c in range(2):
            m_ref[c], l_ref[c], acc_ref[c] = state[c]

    last_before, _ = far_side(k_first + (n_chunks - 1) * tc)
    _, first_after = far_side(k_first)
    all_far = last_before | first_after

    @pl.when(all_far)
    def _():
        run(False)

    @pl.when(jnp.logical_not(all_far))
    def _():
        run(True)

    @pl.when(j == pl.num_programs(3) - 1)
    def _():
        lam = (jnp.exp(jnp.sum(lq1_ref[...] * lk1_ref[...], keepdims=True))
               - jnp.exp(jnp.sum(lq2_ref[...] * lk2_ref[...], keepdims=True)) + lambda_init)
        a = acc_ref[0] / l_ref[0] - lam * (acc_ref[1] / l_ref[1])
        a = a * lax.rsqrt(jnp.mean(a * a, axis=0, keepdims=True) + EPS) * gs_ref[...]
        o_ref[...] = (a * (1.0 - lambda_init)).T.astype(BF16)


def _proj_kernel(o_ref, w_ref, h_ref, out_ref):
    out_ref[...] = h_ref[...] + _dot(o_ref[...], w_ref[...])


def _attn_tiles(s):
    tq = min(512, s)
    tc = min(512, s)
    assert tc >= FAR and tq % tc == 0
    return tq, tc


def _attention_layer(h, g, w_qkv, w_o, g_q, g_k, lq1, lk1, lq2, lk2, g_sub, rel_bias, lambda_init):
    b, s, d = h.shape
    nh = d // HEAD_BLOCK
    tn = min(512, s)
    hd = np.arange(d) // HEAD_DIM
    bd = jnp.asarray((hd[:, None] == hd[None, :]) / HEAD_DIM, BF16)
    tile_gain = lambda x: jnp.tile(x, d // HEAD_DIM).reshape(1, d)
    const2 = lambda shape: pl.BlockSpec(shape, lambda bi, i: (0,) * len(shape))
    qt, k, vt = pl.pallas_call(
        functools.partial(_qkv_kernel, d=d, nh=nh),
        grid=(b, s // tn),
        in_specs=[pl.BlockSpec((None, tn, d), lambda bi, i: (bi, i, 0)),
                  const2((1, d)), const2((d, 3 * d)), const2((d, d)), const2((1, d)), const2((1, d))],
        out_specs=[pl.BlockSpec((None, nh, HEAD_BLOCK, tn), lambda bi, i: (bi, 0, 0, i)),
                   pl.BlockSpec((None, tn, d), lambda bi, i: (bi, i, 0)),
                   pl.BlockSpec((None, nh, HEAD_BLOCK, tn), lambda bi, i: (bi, 0, 0, i))],
        out_shape=[jax.ShapeDtypeStruct((b, nh, HEAD_BLOCK, s), BF16),
                   jax.ShapeDtypeStruct((b, s, d), BF16),
                   jax.ShapeDtypeStruct((b, nh, HEAD_BLOCK, s), BF16)],
        compiler_params=_cparams(("parallel", "parallel")),
    )(h, g.reshape(1, d), w_qkv.astype(BF16), bd, tile_gain(g_q), tile_gain(g_k))

    tq, tc = _attn_tiles(s)
    tko = min(2048, s)
    n_chunks = tko // tc
    band = _bias_band(rel_bias, tq, tc)
    far = jnp.stack([rel_bias[N_BUCKETS // 2 - 1], rel_bias[N_BUCKETS - 1]], axis=1) * LOG2E
    const4 = lambda shape: pl.BlockSpec(shape, lambda bi, hi, i, j: (0,) * len(shape))
    lam_spec = const4((1, HEAD_DIM))
    o = pl.pallas_call(
        functools.partial(_attn_kernel, tq=tq, tc=tc, n_chunks=n_chunks, lambda_init=lambda_init),
        grid=(b, nh, s // tq, s // tko),
        in_specs=[pl.BlockSpec(memory_space=pltpu.SMEM),
                  pl.BlockSpec((None, None, HEAD_BLOCK, tq), lambda bi, hi, i, j: (bi, hi, 0, i)),
                  pl.BlockSpec((None, tko, HEAD_BLOCK), lambda bi, hi, i, j: (bi, j, hi)),
                  pl.BlockSpec((None, None, HEAD_BLOCK, tko), lambda bi, hi, i, j: (bi, hi, 0, j)),
                  pl.BlockSpec((None, tq + 3 * tc, tq), lambda bi, hi, i, j: (hi, 0, 0)),
                  lam_spec, lam_spec, lam_spec, lam_spec,
                  const4((HEAD_BLOCK, 1))],
        out_specs=pl.BlockSpec((None, tq, HEAD_BLOCK), lambda bi, hi, i, j: (bi, i, hi)),
        out_shape=jax.ShapeDtypeStruct((b, s, d), BF16),
        scratch_shapes=[pltpu.VMEM((2, 1, tq), F32), pltpu.VMEM((2, 1, tq), F32),
                        pltpu.VMEM((2, HEAD_BLOCK, tq), F32)],
        compiler_params=_cparams(("parallel", "parallel", "parallel", "arbitrary")),
    )(far, qt, k, vt, band, lq1.reshape(1, -1), lk1.reshape(1, -1), lq2.reshape(1, -1), lk2.reshape(1, -1),
      g_sub.reshape(HEAD_BLOCK, 1))

    n = b * s
    tp = min(1024, n)
    out = pl.pallas_call(
        _proj_kernel,
        grid=(n // tp,),
        in_specs=[pl.BlockSpec((tp, d), lambda i: (i, 0)),
                  pl.BlockSpec((d, d), lambda i: (0, 0)),
                  pl.BlockSpec((tp, d), lambda i: (i, 0))],
        out_specs=pl.BlockSpec((tp, d), lambda i: (i, 0)),
        out_shape=jax.ShapeDtypeStruct((n, d), F32),
        compiler_params=_cparams(("parallel",)),
    )(o.reshape(n, d), w_o.astype(BF16), h.reshape(n, d))
    return out.reshape(b, s, d)


def _trunk(x, rel_bias, wa, wb, g_mix, g_ffn, w_qkv, w_attn_out, g_q, g_k, lq1, lk1, lq2, lk2, g_sub,
           w_router, w_gate, w_up, w_down):
    b, s, d = x.shape
    moe = lambda h, i: _moe_layer(h.reshape(b * s, d), g_ffn[i], w_router[i], w_gate, w_up, w_down,
                                  i).reshape(b, s, d)
    h = _fourier_layer(x, g_mix[0], wa, wb)
    h = moe(h, 0)
    lambda_init = 0.8 - 0.6 * math.exp(-0.3 * 1)
    h = _attention_layer(h, g_mix[1], w_qkv[0], w_attn_out[0], g_q[0], g_k[0], lq1[0], lk1[0], lq2[0],
                         lk2[0], g_sub[0], rel_bias, lambda_init)
    return moe(h, 1)


def kernel(x_prompt, x_sample, rel_bias, g_mix, g_ffn, w_fourier_out, w_qkv, w_attn_out, g_q, g_k,
           lambda_q1, lambda_k1, lambda_q2, lambda_k2, g_sub, w_router, w_gate, w_up, w_down):
    assert g_mix.shape[0] == 2, "one Fourier layer followed by one attention layer"
    wa, wb = _fold_w(w_fourier_out[0])
    run = lambda x: _trunk(x, rel_bias, wa, wb, g_mix, g_ffn, w_qkv, w_attn_out, g_q, g_k, lambda_q1, lambda_k1,
                           lambda_q2, lambda_k2, g_sub, w_router, w_gate, w_up, w_down)
    return (run(x_prompt), run(x_sample))
```

```python
import functools
import math

import numpy as np
import jax
import jax.numpy as jnp
from jax import lax
from jax.experimental import pallas as pl
from jax.experimental.pallas import tpu as pltpu

EPS = 1e-6
F32 = jnp.float32
BF16 = jnp.bfloat16
I32 = jnp.int32
HIGHEST = lax.Precision.HIGHEST
LOG2E = math.log2(math.e)

LANES = 128
SUBLANES = 8
FOURIER_GROUP_DIM = 128
HEAD_BLOCK = 128
HEAD_DIM = 64
ONES_ROWS = 16
N_EXPERTS = 16
CAPACITY_FACTOR = 2
N_BUCKETS = 32
MAX_DISTANCE = 128
VMEM_LIMIT = 56 * 1024 * 1024


def _cparams(sem):
    return pltpu.CompilerParams(dimension_semantics=sem, vmem_limit_bytes=VMEM_LIMIT)


def _rms(x, g):
    return x * lax.rsqrt(jnp.mean(x * x, axis=-1, keepdims=True) + EPS) * g


def _dot(a, b):
    return jnp.dot(a, b, preferred_element_type=F32)


def _cos_sin(n):
    k = np.arange(n)
    ang = 2.0 * np.pi * ((k[:, None] * k[None, :]) % n) / n
    return np.cos(ang), np.sin(ang)


def _fold_w_kernel(cc_ref, sc_ref, w_ref, wa_ref, wb_ref):
    w = w_ref[...]
    wa_ref[...] = jnp.dot(cc_ref[...], w, precision=HIGHEST, preferred_element_type=F32).astype(BF16)
    wb_ref[...] = jnp.dot(sc_ref[...], w, precision=HIGHEST, preferred_element_type=F32).astype(BF16)


def _fold_w(w):
    d = w.shape[0]
    gd = FOURIER_GROUP_DIM
    c, s = _cos_sin(gd)
    cc = jnp.asarray(c / math.sqrt(gd), F32)
    sc = jnp.asarray(s / math.sqrt(gd), F32)
    const = pl.BlockSpec((gd, gd), lambda g: (0, 0))
    blk = pl.BlockSpec((gd, d), lambda g: (g, 0))
    return pl.pallas_call(
        _fold_w_kernel,
        grid=(d // gd,),
        in_specs=[const, const, blk],
        out_specs=[blk, blk],
        out_shape=[jax.ShapeDtypeStruct((d, d), BF16)] * 2,
        compiler_params=_cparams(("arbitrary",)),
    )(cc, sc, w)


def _fourier_a_kernel(x_ref, g_ref, f1_ref, twr_ref, twi_ref, yr_ref, yi_ref, *, nb, n1, d):
    j = pl.program_id(1)
    lane = lax.broadcasted_iota(I32, twr_ref.shape, 1)
    for c in range(nb):
        xn = _rms(x_ref[:, c, :], g_ref[...]).astype(BF16)
        y = _dot(f1_ref[...], xn)
        yr, yi = y[:n1], y[n1:]
        sel = lane == (j * nb + c)
        tr = jnp.sum(jnp.where(sel, twr_ref[...], 0.0), axis=1, keepdims=True)
        ti = jnp.sum(jnp.where(sel, twi_ref[...], 0.0), axis=1, keepdims=True)
        cols = slice(c * d, (c + 1) * d)
        yr_ref[:, cols] = (yr * tr - yi * ti).astype(BF16)
        yi_ref[:, cols] = (yr * ti + yi * tr).astype(BF16)


def _fourier_b_kernel(yr_ref, yi_ref, f2_ref, wa_ref, wb_ref, x_ref, o_ref, *, kb, n2):
    ars, ais = [], []
    for c in range(kb):
        rows = slice(c * n2, (c + 1) * n2)
        ys = jnp.concatenate([yr_ref[rows, :], yi_ref[rows, :]], axis=0)
        a = _dot(f2_ref[...], ys)
        ars.append(a[:n2].astype(BF16))
        ais.append(a[n2:].astype(BF16))
    ar = jnp.concatenate(ars, axis=0)
    ai = jnp.concatenate(ais, axis=0)
    out = _dot(ar, wa_ref[...]) + _dot(ai, wb_ref[...])
    for c in range(kb):
        o_ref[:, c, :] = x_ref[:, c, :] + out[c * n2:(c + 1) * n2]


def _split_len(s):
    n1 = 1 << (int(math.log2(s)) // 2)
    assert s % n1 == 0
    return n1, s // n1


def _fourier_layer(x, g, wa, wb):
    b, s, d = x.shape
    n1, n2 = _split_len(s)
    c1, s1 = _cos_sin(n1)
    c2, s2 = _cos_sin(n2)
    f1 = jnp.asarray(np.concatenate([c1, -s1], axis=0) / math.sqrt(n1), BF16)
    f2 = jnp.asarray(np.block([[c2, s2], [-s2, c2]]) / math.sqrt(n2), BF16)
    ang = 2.0 * np.pi * ((np.arange(n1)[:, None] * np.arange(n2)[None, :]) % s) / s
    twr = jnp.asarray(np.cos(ang), F32)
    twi = jnp.asarray(-np.sin(ang), F32)

    nb = SUBLANES
    blk_a = pl.BlockSpec((None, n1, nb, d), lambda bi, j: (bi, 0, j, 0))
    blk_ya = pl.BlockSpec((None, n1, nb * d), lambda bi, j: (bi, 0, j))
    full = lambda shape: pl.BlockSpec(shape, lambda bi, j: (0,) * len(shape))
    yr, yi = pl.pallas_call(
        functools.partial(_fourier_a_kernel, nb=nb, n1=n1, d=d),
        grid=(b, n2 // nb),
        in_specs=[blk_a, full((1, d)), full((2 * n1, n1)), full((n1, n2)), full((n1, n2))],
        out_specs=[blk_ya, blk_ya],
        out_shape=[jax.ShapeDtypeStruct((b, n1, n2 * d), BF16)] * 2,
        compiler_params=_cparams(("parallel", "parallel")),
    )(x.reshape(b, n1, n2, d), g.reshape(1, d), f1, twr, twi)

    kb = SUBLANES
    blk_y = pl.BlockSpec((None, kb * n2, d), lambda bi, i: (bi, i, 0))
    blk_x = pl.BlockSpec((None, n2, kb, d), lambda bi, i: (bi, 0, i, 0))
    out = pl.pallas_call(
        functools.partial(_fourier_b_kernel, kb=kb, n2=n2),
        grid=(b, n1 // kb),
        in_specs=[blk_y, blk_y, full((2 * n2, 2 * n2)), full((d, d)), full((d, d)), blk_x],
        out_specs=blk_x,
        out_shape=jax.ShapeDtypeStruct((b, n2, n1, d), F32),
        compiler_params=_cparams(("parallel", "parallel")),
    )(yr.reshape(b, s, d), yi.reshape(b, s, d), f2, wa, wb, x.reshape(b, n2, n1, d))
    return out.reshape(b, s, d)


def _router_kernel(h_ref, g_ref, wrt_ref, hn_ref, aff_ref):
    hn = _rms(h_ref[...], g_ref[...])
    hn_ref[...] = hn.astype(BF16)
    logits = lax.dot_general(wrt_ref[...], hn, (((1,), (1,)), ((), ())),
                             precision=HIGHEST, preferred_element_type=F32)
    ex = jnp.exp(logits - jnp.max(logits, axis=0, keepdims=True))
    aff_ref[...] = ex / jnp.sum(ex, axis=0, keepdims=True)


def _select_kernel(a_ref, pos_ref, cum_ref, *, e, nb, k):
    keys = lax.bitcast_convert_type(a_ref[...], I32).reshape(e, nb, LANES)

    def count(mask):
        part = jnp.sum(mask.astype(F32), axis=2, keepdims=True)
        return jnp.sum(part, axis=1, keepdims=True)

    def value_step(i, t):
        cand = t | (jnp.int32(1) << (30 - i))
        return jnp.where(count(keys >= cand) >= k, cand, t)

    thr = lax.fori_loop(0, 31, value_step, jnp.zeros((e, 1, 1), I32))
    gt = keys > thr
    eq = keys == thr
    need = k - count(gt)
    tok = (lax.broadcasted_iota(I32, (e, nb, LANES), 1) * LANES
           + lax.broadcasted_iota(I32, (e, nb, LANES), 2))
    nbits = (nb * LANES - 1).bit_length()

    def index_step(i, v):
        cand = v | (jnp.int32(1) << (nbits - 1 - i))
        return jnp.where(count(eq & (tok < cand)) < need, cand, v)

    last = lax.fori_loop(0, nbits, index_step, jnp.zeros((e, 1, 1), I32))
    sel = (gt | (eq & (tok <= last))).reshape(e * nb, LANES)
    self32 = sel.astype(F32)

    r = lax.broadcasted_iota(I32, (LANES, LANES), 0)
    c = lax.broadcasted_iota(I32, (LANES, LANES), 1)
    inc = _dot(self32.astype(BF16), (r <= c).astype(BF16))
    tot = jnp.broadcast_to(inc[:, LANES - 1:LANES], (e * nb, LANES)).astype(BF16)
    rb = lax.broadcasted_iota(I32, (nb, nb), 0)
    cb = lax.broadcasted_iota(I32, (nb, nb), 1)
    lower = (cb < rb).astype(BF16)
    before = jnp.concatenate([_dot(lower, tot[x * nb:(x + 1) * nb]) for x in range(e)], axis=0)
    pos = inc - self32 + before
    pos_ref[...] = jnp.where(sel, pos, -1.0).astype(I32)
    cum_ref[...] = before.astype(I32)


def _one_hot_mask(pos_ref, row_tile, rt):
    tt = pos_ref.shape[-1]
    rows = lax.broadcasted_iota(I32, (rt, tt), 0) + row_tile * rt
    return pos_ref[...] == rows


def _gather_kernel(se, sr, st, sf, sv, hn_ref, pos_ref, aff_ref, xe_ref, gc_ref, *, rt):
    s = pl.program_id(0)

    @pl.when(sv[s] == 1)
    def _():
        mask = _one_hot_mask(pos_ref, sr[s], rt)
        tt = mask.shape[1]
        tokens = hn_ref[pl.ds(pl.multiple_of(st[s] * tt, tt), tt), :]
        rows = _dot(jnp.where(mask, 1.0, 0.0).astype(BF16), tokens)
        gate = jnp.sum(jnp.where(mask, aff_ref[...], 0.0), axis=1, keepdims=True)

        @pl.when(sf[s] == 1)
        def _():
            xe_ref[...] = rows.astype(BF16)
            gc_ref[...] = gate

        @pl.when(sf[s] == 0)
        def _():
            xe_ref[...] = (xe_ref[...].astype(F32) + rows).astype(BF16)
            gc_ref[...] += gate


def _ffn_kernel(xe_ref, wg_ref, wu_ref, wd_ref, gc_ref, ye_ref, acc_ref):
    j = pl.program_id(2)
    x = xe_ref[...]
    g = _dot(x, wg_ref[...].astype(BF16))
    u = _dot(x, wu_ref[...].astype(BF16))
    hid = (g / (1.0 + jnp.exp(-g))) * u
    part = _dot(hid.astype(BF16), wd_ref[...].astype(BF16))

    @pl.when(j == 0)
    def _():
        acc_ref[...] = part

    @pl.when(j > 0)
    def _():
        acc_ref[...] += part

    @pl.when(j == pl.num_programs(2) - 1)
    def _():
        ye_ref[...] = (acc_ref[...] * gc_ref[...]).astype(BF16)


def _combine_kernel(ce, cr, ct, cf, cv, h_ref, pos_ref, ye_ref, o_ref, *, rt):
    s = pl.program_id(0)

    @pl.when(cv[s] == 1)
    def _():
        mask = _one_hot_mask(pos_ref, cr[s], rt)
        back = lax.dot_general(jnp.where(mask, 1.0, 0.0).astype(BF16), ye_ref[...],
                               (((0,), (0,)), ((), ())), preferred_element_type=F32)

        @pl.when(cf[s] == 1)
        def _():
            o_ref[...] = h_ref[...] + back

        @pl.when(cf[s] == 0)
        def _():
            o_ref[...] += back


def _flat_schedule(inter, n_steps):
    flat = inter.reshape(-1)
    idx = jnp.nonzero(flat, size=n_steps, fill_value=0)[0].astype(I32)
    n_valid = jnp.sum(flat.astype(I32))
    step = jnp.arange(n_steps, dtype=I32)
    valid = step < n_valid
    idx = jnp.where(valid, idx, idx[jnp.maximum(n_valid - 1, 0)])
    d1, d2 = inter.shape[1], inter.shape[2]
    return idx // (d1 * d2), (idx // d2) % d1, idx % d2, valid.astype(I32)


def _first_flags(valid, *keys):
    changed = jnp.zeros(valid.shape, bool).at[0].set(True)
    for key in keys:
        changed = changed | jnp.concatenate([jnp.ones((1,), bool), key[1:] != key[:-1]])
    return (changed & (valid == 1)).astype(I32)


def _moe_layer(h, g, w_router, w_gate, w_up, w_down, layer):
    n, d = h.shape
    e = N_EXPERTS
    cap = CAPACITY_FACTOR * n // e
    f = w_gate.shape[-1]
    nb = n // LANES
    tn = min(1024, n)

    hn, aff = pl.pallas_call(
        _router_kernel,
        grid=(n // tn,),
        in_specs=[pl.BlockSpec((tn, d), lambda i: (i, 0)),
                  pl.BlockSpec((1, d), lambda i: (0, 0)),
                  pl.BlockSpec((e, d), lambda i: (0, 0))],
        out_specs=[pl.BlockSpec((tn, d), lambda i: (i, 0)),
                   pl.BlockSpec((e, tn), lambda i: (0, i))],
        out_shape=[jax.ShapeDtypeStruct((n, d), BF16), jax.ShapeDtypeStruct((e, n), F32)],
        compiler_params=_cparams(("parallel",)),
    )(h, g.reshape(1, d), w_router.T)

    whole = pl.BlockSpec((e * nb, LANES), lambda i: (0, 0))
    pos, cum = pl.pallas_call(
        functools.partial(_select_kernel, e=e, nb=nb, k=cap),
        grid=(1,),
        in_specs=[whole],
        out_specs=[whole, whole],
        out_shape=[jax.ShapeDtypeStruct((e * nb, LANES), I32)] * 2,
        compiler_params=_cparams(("arbitrary",)),
    )(aff.reshape(e * nb, LANES))

    rt = min(256, cap)
    tt = min(1024, n)
    n_r, n_t = cap // rt, n // tt
    lo = cum[:, 0].reshape(e, nb)[:, ::tt // LANES]
    hi = jnp.concatenate([lo[:, 1:], jnp.full((e, 1), cap, I32)], axis=1)
    r_lo = (jnp.arange(n_r, dtype=I32) * rt)[None, :, None]
    inter = (lo[:, None, :] < r_lo + rt) & (hi[:, None, :] > r_lo) & (hi > lo)[:, None, :]
    n_gather = e * (n_r + n_t - 1)
    se, sr, st, sv = _flat_schedule(inter, n_gather)
    sf = _first_flags(sv, se, sr)
    r_any = jnp.clip(lo[0] // rt, 0, n_r - 1)
    forced = (jnp.arange(e)[:, None, None] == 0) & (jnp.arange(n_r)[None, :, None] == r_any[None, None, :])
    n_combine = n_gather + n_t
    ct, ce, cr, cv = _flat_schedule(jnp.transpose(inter | forced, (2, 0, 1)), n_combine)
    cf = _first_flags(cv, ct)

    pos3 = pos.reshape(e * n_t, 1, tt)
    aff3 = aff.reshape(e * n_t, 1, tt)
    xe, gc = pl.pallas_call(
        functools.partial(_gather_kernel, rt=rt),
        grid_spec=pltpu.PrefetchScalarGridSpec(
            num_scalar_prefetch=5,
            grid=(n_gather,),
            in_specs=[pl.BlockSpec((n, d), lambda s, se, sr, st, sf, sv: (0, 0), pipeline_mode=pl.Buffered(1)),
                      pl.BlockSpec((None, 1, tt), lambda s, se, sr, st, sf, sv: (se[s] * n_t + st[s], 0, 0)),
                      pl.BlockSpec((None, 1, tt), lambda s, se, sr, st, sf, sv: (se[s] * n_t + st[s], 0, 0))],
            out_specs=[pl.BlockSpec((None, rt, d), lambda s, se, sr, st, sf, sv: (se[s], sr[s], 0)),
                       pl.BlockSpec((None, rt, 1), lambda s, se, sr, st, sf, sv: (se[s], sr[s], 0))],
        ),
        out_shape=[jax.ShapeDtypeStruct((e, cap, d), BF16), jax.ShapeDtypeStruct((e, cap, 1), F32)],
        compiler_params=_cparams(("arbitrary",)),
    )(se, sr, st, sf, sv, hn, pos3, aff3)

    tm = min(2048, cap)
    tf = 256
    ye = pl.pallas_call(
        _ffn_kernel,
        grid=(e, cap // tm, f // tf),
        in_specs=[pl.BlockSpec((None, tm, d), lambda x, r, j: (x, r, 0)),
                  pl.BlockSpec((None, None, d, tf), lambda x, r, j: (layer, x, 0, j)),
                  pl.BlockSpec((None, None, d, tf), lambda x, r, j: (layer, x, 0, j)),
                  pl.BlockSpec((None, None, tf, d), lambda x, r, j: (layer, x, j, 0)),
                  pl.BlockSpec((None, tm, 1), lambda x, r, j: (x, r, 0))],
        out_specs=pl.BlockSpec((None, tm, d), lambda x, r, j: (x, r, 0)),
        out_shape=jax.ShapeDtypeStruct((e, cap, d), BF16),
        scratch_shapes=[pltpu.VMEM((tm, d), F32)],
        compiler_params=_cparams(("parallel", "parallel", "arbitrary")),
    )(xe, w_gate, w_up, w_down, gc)

    return pl.pallas_call(
        functools.partial(_combine_kernel, rt=rt),
        grid_spec=pltpu.PrefetchScalarGridSpec(
            num_scalar_prefetch=5,
            grid=(n_combine,),
            in_specs=[pl.BlockSpec((tt, d), lambda s, ce, cr, ct, cf, cv: (ct[s], 0)),
                      pl.BlockSpec((None, 1, tt), lambda s, ce, cr, ct, cf, cv: (ce[s] * n_t + ct[s], 0, 0)),
                      pl.BlockSpec((None, rt, d), lambda s, ce, cr, ct, cf, cv: (ce[s], cr[s], 0))],
            out_specs=pl.BlockSpec((tt, d), lambda s, ce, cr, ct, cf, cv: (ct[s], 0)),
        ),
        out_shape=jax.ShapeDtypeStruct((n, d), F32),
        compiler_params=_cparams(("arbitrary",)),
    )(ce, cr, ct, cf, cv, h, pos3, ye)


def _t5_bucket_np(rel):
    nb = N_BUCKETS // 2
    max_exact = nb // 2
    ret = np.where(rel > 0, nb, 0)
    n = np.abs(rel)
    nf = np.maximum(n, 1).astype(np.float64)
    large = max_exact + (np.log(nf / max_exact) / math.log(MAX_DISTANCE / max_exact)
                         * (nb - max_exact)).astype(np.int64)
    large = np.minimum(large, nb - 1)
    return ret + np.where(n < max_exact, n, large)


def _far_distance():
    n = np.arange(1, 4 * MAX_DISTANCE)
    last = _t5_bucket_np(-n)
    return int(n[np.nonzero(last != last[-1])[0][-1]] + 1)


FAR = _far_distance()


LOOKAHEAD = 1
N_BIAS_TILES = 5
assert FAR <= LANES + 1


def _bias_tiles_kernel(near_ref, far_ref, t_ref):
    h = pl.program_id(0)
    d = lax.broadcasted_iota(I32, (LANES, LANES), 0) - lax.broadcasted_iota(I32, (LANES, LANES), 1)
    for idx in range(N_BIAS_TILES):
        rel = d + LANES * (idx - 2)
        tile = jnp.where(rel < 0, far_ref[h, 0], far_ref[h, 1])
        if abs(idx - 2) <= 1:
            tile = lax.fori_loop(
                0, 2 * FAR - 1, lambda r, t: jnp.where(rel == r - (FAR - 1), near_ref[h, r], t), tile)
        t_ref[idx] = tile


def _bias_tiles(rel_bias):
    nh = rel_bias.shape[1]
    near = (rel_bias[_t5_bucket_np(np.arange(-(FAR - 1), FAR))] * LOG2E).T
    far = jnp.stack([rel_bias[N_BUCKETS // 2 - 1], rel_bias[N_BUCKETS - 1]], axis=1) * LOG2E
    smem = pl.BlockSpec(memory_space=pltpu.SMEM)
    tiles = pl.pallas_call(
        _bias_tiles_kernel,
        grid=(nh,),
        in_specs=[smem, smem],
        out_specs=pl.BlockSpec((None, N_BIAS_TILES, LANES, LANES), lambda h: (h, 0, 0, 0)),
        out_shape=jax.ShapeDtypeStruct((nh, N_BIAS_TILES, LANES, LANES), F32),
        compiler_params=_cparams(("arbitrary",)),
    )(near, far)
    return tiles, far


def _qkv_kernel(h_ref, g_ref, w_ref, bd_ref, gq_ref, gk_ref, qt_ref, k_ref, vt_ref, *, d, nh):
    hn = _rms(h_ref[...], g_ref[...]).astype(BF16)
    qkv = _dot(hn, w_ref[...])
    q, k, v = qkv[:, :d], qkv[:, d:2 * d], qkv[:, 2 * d:]

    def head_norm(x, gain):
        ms = _dot((x * x).astype(BF16), bd_ref[...])
        return x * lax.rsqrt(ms + EPS) * gain

    qn = head_norm(q, gq_ref[...]) * (HEAD_DIM ** -0.5 * LOG2E)
    k_ref[...] = head_norm(k, gk_ref[...]).astype(BF16)
    for h in range(nh):
        cols = slice(h * HEAD_BLOCK, (h + 1) * HEAD_BLOCK)
        qt_ref[h] = qn[:, cols].T.astype(BF16)
        vt_ref[h] = v[:, cols].T.astype(BF16)


def _attn_kernel(far_ref, qt_ref, k_ref, vt_ref, tiles_ref, lq1_ref, lk1_ref, lq2_ref, lk2_ref, gs_ref, o_ref,
                 m_ref, l_ref, acc_ref, *, tq, tc, n_chunks, lambda_init):
    h = pl.program_id(1)
    i = pl.program_id(2)
    j = pl.program_id(3)

    @pl.when(j == 0)
    def _():
        m_ref[...] = jnp.full(m_ref.shape, -1e30, F32)
        l_ref[...] = jnp.zeros(l_ref.shape, F32)
        acc_ref[...] = jnp.zeros(acc_ref.shape, F32)

    qt = qt_ref[...]
    row = lax.broadcasted_iota(I32, qt.shape, 0)
    comps = (jnp.where(row < HEAD_DIM, qt, jnp.zeros_like(qt)),
             jnp.where(row >= HEAD_DIM, qt, jnp.zeros_like(qt)))
    q0 = i * tq
    k_first = j * n_chunks * tc
    ones = jnp.ones((ONES_ROWS, tc), BF16)

    def far_side(k0):
        before = k0 + tc - 1 - q0 <= -FAR
        after = k0 - (q0 + tq - 1) >= FAR
        return before, after

    def scores(jj, with_band):
        kc = k_ref[jj * tc:(jj + 1) * tc, :]
        s = [_dot(kc, comps[c]) for c in range(2)]
        if with_band:
            blk = (k_first + jj * tc - q0) // LANES
            mid = N_BIAS_TILES // 2
            bias = jnp.concatenate(
                [jnp.concatenate([tiles_ref[jnp.clip(blk + a - b, -mid, mid) + mid]
                                  for b in range(tq // LANES)], axis=1)
                 for a in range(tc // LANES)], axis=0)
            s = [x + bias for x in s]
        return s

    def run(with_band):
        state = [(m_ref[c], l_ref[c], acc_ref[c]) for c in range(2)]
        ahead = [scores(jj, with_band) for jj in range(min(LOOKAHEAD, n_chunks))]
        for jj in range(n_chunks):
            s_now = ahead.pop(0)
            if jj + LOOKAHEAD < n_chunks:
                ahead.append(scores(jj + LOOKAHEAD, with_band))
            rows = slice(jj * tc, (jj + 1) * tc)
            if with_band:
                const = 0.0
            else:
                before, _ = far_side(k_first + jj * tc)
                const = jnp.where(before, far_ref[h, 0], far_ref[h, 1])
            vt1 = jnp.concatenate([vt_ref[:, rows], ones], axis=0)
            for c in range(2):
                s = s_now[c]
                m_old, l_old, acc_old = state[c]
                m_new = jnp.maximum(m_old, jnp.max(s, axis=0, keepdims=True) + const)
                p = jnp.exp2(s - (m_new - const)).astype(BF16)
                alpha = jnp.exp2(m_old - m_new)
                pv = _dot(vt1, p)
                state[c] = (m_new, alpha * l_old + pv[HEAD_BLOCK:HEAD_BLOCK + 1],
                            alpha * acc_old + pv[:HEAD_BLOCK])
        for c in range(2):
            m_ref[c], l_ref[c], acc_ref[c] = state[c]

    last_before, _ = far_side(k_first + (n_chunks - 1) * tc)
    _, first_after = far_side(k_first)
    all_far = last_before | first_after

    @pl.when(all_far)
    def _():
        run(False)

    @pl.when(jnp.logical_not(all_far))
    def _():
        run(True)

    @pl.when(j == pl.num_programs(3) - 1)
    def _():
        lam = (jnp.exp(jnp.sum(lq1_ref[...] * lk1_ref[...], keepdims=True))
               - jnp.exp(jnp.sum(lq2_ref[...] * lk2_ref[...], keepdims=True)) + lambda_init)
        a = acc_ref[0] / l_ref[0] - lam * (acc_ref[1] / l_ref[1])
        a = a * lax.rsqrt(jnp.mean(a * a, axis=0, keepdims=True) + EPS) * gs_ref[...]
        o_ref[...] = (a * (1.0 - lambda_init)).T.astype(BF16)


def _proj_kernel(o_ref, w_ref, h_ref, out_ref):
    out_ref[...] = h_ref[...] + _dot(o_ref[...], w_ref[...])


def _attn_tiles(s):
    tq = min(1024, s)
    tc = min(512, s)
    assert tq % LANES == 0 and tc % LANES == 0
    return tq, tc


def _attention_layer(h, g, w_qkv, w_o, g_q, g_k, lq1, lk1, lq2, lk2, g_sub, rel_bias, lambda_init):
    b, s, d = h.shape
    nh = d // HEAD_BLOCK
    tn = min(512, s)
    hd = np.arange(d) // HEAD_DIM
    bd = jnp.asarray((hd[:, None] == hd[None, :]) / HEAD_DIM, BF16)
    tile_gain = lambda x: jnp.tile(x, d // HEAD_DIM).reshape(1, d)
    const2 = lambda shape: pl.BlockSpec(shape, lambda bi, i: (0,) * len(shape))
    qt, k, vt = pl.pallas_call(
        functools.partial(_qkv_kernel, d=d, nh=nh),
        grid=(b, s // tn),
        in_specs=[pl.BlockSpec((None, tn, d), lambda bi, i: (bi, i, 0)),
                  const2((1, d)), const2((d, 3 * d)), const2((d, d)), const2((1, d)), const2((1, d))],
        out_specs=[pl.BlockSpec((None, nh, HEAD_BLOCK, tn), lambda bi, i: (bi, 0, 0, i)),
                   pl.BlockSpec((None, tn, d), lambda bi, i: (bi, i, 0)),
                   pl.BlockSpec((None, nh, HEAD_BLOCK, tn), lambda bi, i: (bi, 0, 0, i))],
        out_shape=[jax.ShapeDtypeStruct((b, nh, HEAD_BLOCK, s), BF16),
                   jax.ShapeDtypeStruct((b, s, d), BF16),
                   jax.ShapeDtypeStruct((b, nh, HEAD_BLOCK, s), BF16)],
        compiler_params=_cparams(("parallel", "parallel")),
    )(h, g.reshape(1, d), w_qkv.astype(BF16), bd, tile_gain(g_q), tile_gain(g_k))

    tq, tc = _attn_tiles(s)
    tko = min(2048, s)
    n_chunks = tko // tc
    tiles, far = _bias_tiles(rel_bias)
    const4 = lambda shape: pl.BlockSpec(shape, lambda bi, hi, i, j: (0,) * len(shape))
    lam_spec = const4((1, HEAD_DIM))
    o = pl.pallas_call(
        functools.partial(_attn_kernel, tq=tq, tc=tc, n_chunks=n_chunks, lambda_init=lambda_init),
        grid=(b, nh, s // tq, s // tko),
        in_specs=[pl.BlockSpec(memory_space=pltpu.SMEM),
                  pl.BlockSpec((None, None, HEAD_BLOCK, tq), lambda bi, hi, i, j: (bi, hi, 0, i)),
                  pl.BlockSpec((None, tko, HEAD_BLOCK), lambda bi, hi, i, j: (bi, j, hi)),
                  pl.BlockSpec((None, None, HEAD_BLOCK, tko), lambda bi, hi, i, j: (bi, hi, 0, j)),
                  pl.BlockSpec((None, N_BIAS_TILES, LANES, LANES), lambda bi, hi, i, j: (hi, 0, 0, 0)),
                  lam_spec, lam_spec, lam_spec, lam_spec,
                  const4((HEAD_BLOCK, 1))],
        out_specs=pl.BlockSpec((None, tq, HEAD_BLOCK), lambda bi, hi, i, j: (bi, i, hi)),
        out_shape=jax.ShapeDtypeStruct((b, s, d), BF16),
        scratch_shapes=[pltpu.VMEM((2, 1, tq), F32), pltpu.VMEM((2, 1, tq), F32),
                        pltpu.VMEM((2, HEAD_BLOCK, tq), F32)],
        compiler_params=_cparams(("parallel", "parallel", "parallel", "arbitrary")),
    )(far, qt, k, vt, tiles, lq1.reshape(1, -1), lk1.reshape(1, -1), lq2.reshape(1, -1), lk2.reshape(1, -1),
      g_sub.reshape(HEAD_BLOCK, 1))

    n = b * s
    tp = min(1024, n)
    out = pl.pallas_call(
        _proj_kernel,
        grid=(n // tp,),
        in_specs=[pl.BlockSpec((tp, d), lambda i: (i, 0)),
                  pl.BlockSpec((d, d), lambda i: (0, 0)),
                  pl.BlockSpec((tp, d), lambda i: (i, 0))],
        out_specs=pl.BlockSpec((tp, d), lambda i: (i, 0)),
        out_shape=jax.ShapeDtypeStruct((n, d), F32),
        compiler_params=_cparams(("parallel",)),
    )(o.reshape(n, d), w_o.astype(BF16), h.reshape(n, d))
    return out.reshape(b, s, d)


def _trunk(x, rel_bias, wa, wb, g_mix, g_ffn, w_qkv, w_attn_out, g_q, g_k, lq1, lk1, lq2, lk2, g_sub,
           w_router, w_gate, w_up, w_down):
    b, s, d = x.shape
    moe = lambda h, i: _moe_layer(h.reshape(b * s, d), g_ffn[i], w_router[i], w_gate, w_up, w_down,
                                  i).reshape(b, s, d)
    h = _fourier_layer(x, g_mix[0], wa, wb)
    h = moe(h, 0)
    lambda_init = 0.8 - 0.6 * math.exp(-0.3 * 1)
    h = _attention_layer(h, g_mix[1], w_qkv[0], w_attn_out[0], g_q[0], g_k[0], lq1[0], lk1[0], lq2[0],
                         lk2[0], g_sub[0], rel_bias, lambda_init)
    return moe(h, 1)


def kernel(x_prompt, x_sample, rel_bias, g_mix, g_ffn, w_fourier_out, w_qkv, w_attn_out, g_q, g_k,
           lambda_q1, lambda_k1, lambda_q2, lambda_k2, g_sub, w_router, w_gate, w_up, w_down):
    assert g_mix.shape[0] == 2, "one Fourier layer followed by one attention layer"
    wa, wb = _fold_w(w_fourier_out[0])
    run = lambda x: _trunk(x, rel_bias, wa, wb, g_mix, g_ffn, w_qkv, w_attn_out, g_q, g_k, lambda_q1, lambda_k1,
                           lambda_q2, lambda_k2, g_sub, w_router, w_gate, w_up, w_down)
    return (run(x_prompt), run(x_sample))
```

```python
import functools
import math

import numpy as np
import jax
import jax.numpy as jnp
from jax import lax
from jax.experimental import pallas as pl
from jax.experimental.pallas import tpu as pltpu

EPS = 1e-6
F32 = jnp.float32
BF16 = jnp.bfloat16
I32 = jnp.int32
HIGHEST = lax.Precision.HIGHEST
LOG2E = math.log2(math.e)

LANES = 128
SUBLANES = 8
FOURIER_GROUP_DIM = 128
HEAD_BLOCK = 128
HEAD_DIM = 64
ONES_ROWS = 16
N_EXPERTS = 16
CAPACITY_FACTOR = 2
MOE_GROUP = 4
STEP_NEXT, STEP_FIRST, STEP_IDLE = 0, 1, 2
N_BUCKETS = 32
MAX_DISTANCE = 128
VMEM_LIMIT = 56 * 1024 * 1024


def _cparams(sem):
    return pltpu.CompilerParams(dimension_semantics=sem, vmem_limit_bytes=VMEM_LIMIT)


def _rms(x, g):
    return x * lax.rsqrt(jnp.mean(x * x, axis=-1, keepdims=True) + EPS) * g


def _dot(a, b):
    return jnp.dot(a, b, preferred_element_type=F32)


def _cos_sin(n):
    k = np.arange(n)
    ang = 2.0 * np.pi * ((k[:, None] * k[None, :]) % n) / n
    return np.cos(ang), np.sin(ang)


def _fold_w_kernel(cc_ref, sc_ref, w_ref, wa_ref, wb_ref):
    w = w_ref[...]
    wa_ref[...] = jnp.dot(cc_ref[...], w, precision=HIGHEST, preferred_element_type=F32).astype(BF16)
    wb_ref[...] = jnp.dot(sc_ref[...], w, precision=HIGHEST, preferred_element_type=F32).astype(BF16)


def _fold_w(w):
    d = w.shape[0]
    gd = FOURIER_GROUP_DIM
    c, s = _cos_sin(gd)
    cc = jnp.asarray(c / math.sqrt(gd), F32)
    sc = jnp.asarray(s / math.sqrt(gd), F32)
    const = pl.BlockSpec((gd, gd), lambda g: (0, 0))
    blk = pl.BlockSpec((gd, d), lambda g: (g, 0))
    return pl.pallas_call(
        _fold_w_kernel,
        grid=(d // gd,),
        in_specs=[const, const, blk],
        out_specs=[blk, blk],
        out_shape=[jax.ShapeDtypeStruct((d, d), BF16)] * 2,
        compiler_params=_cparams(("arbitrary",)),
    )(cc, sc, w)


def _fourier_a_kernel(x_ref, g_ref, f1_ref, twr_ref, twi_ref, yr_ref, yi_ref, *, nb, n1, d):
    j = pl.program_id(1)
    lane = lax.broadcasted_iota(I32, twr_ref.shape, 1)
    for c in range(nb):
        xn = _rms(x_ref[:, c, :], g_ref[...]).astype(BF16)
        y = _dot(f1_ref[...], xn)
        yr, yi = y[:n1], y[n1:]
        sel = lane == (j * nb + c)
        tr = jnp.sum(jnp.where(sel, twr_ref[...], 0.0), axis=1, keepdims=True)
        ti = jnp.sum(jnp.where(sel, twi_ref[...], 0.0), axis=1, keepdims=True)
        cols = slice(c * d, (c + 1) * d)
        yr_ref[:, cols] = (yr * tr - yi * ti).astype(BF16)
        yi_ref[:, cols] = (yr * ti + yi * tr).astype(BF16)


def _fourier_b_kernel(yr_ref, yi_ref, f2_ref, wa_ref, wb_ref, x_ref, o_ref, *, kb, n2):
    ars, ais = [], []
    for c in range(kb):
        rows = slice(c * n2, (c + 1) * n2)
        ys = jnp.concatenate([yr_ref[rows, :], yi_ref[rows, :]], axis=0)
        a = _dot(f2_ref[...], ys)
        ars.append(a[:n2].astype(BF16))
        ais.append(a[n2:].astype(BF16))
    ar = jnp.concatenate(ars, axis=0)
    ai = jnp.concatenate(ais, axis=0)
    out = _dot(ar, wa_ref[...]) + _dot(ai, wb_ref[...])
    for c in range(kb):
        o_ref[:, c, :] = x_ref[:, c, :] + out[c * n2:(c + 1) * n2]


def _split_len(s):
    n1 = 1 << (int(math.log2(s)) // 2)
    assert s % n1 == 0
    return n1, s // n1


def _fourier_layer(x, g, wa, wb):
    b, s, d = x.shape
    n1, n2 = _split_len(s)
    c1, s1 = _cos_sin(n1)
    c2, s2 = _cos_sin(n2)
    f1 = jnp.asarray(np.concatenate([c1, -s1], axis=0) / math.sqrt(n1), BF16)
    f2 = jnp.asarray(np.block([[c2, s2], [-s2, c2]]) / math.sqrt(n2), BF16)
    ang = 2.0 * np.pi * ((np.arange(n1)[:, None] * np.arange(n2)[None, :]) % s) / s
    twr = jnp.asarray(np.cos(ang), F32)
    twi = jnp.asarray(-np.sin(ang), F32)

    nb = SUBLANES
    blk_a = pl.BlockSpec((None, n1, nb, d), lambda bi, j: (bi, 0, j, 0))
    blk_ya = pl.BlockSpec((None, n1, nb * d), lambda bi, j: (bi, 0, j))
    full = lambda shape: pl.BlockSpec(shape, lambda bi, j: (0,) * len(shape))
    yr, yi = pl.pallas_call(
        functools.partial(_fourier_a_kernel, nb=nb, n1=n1, d=d),
        grid=(b, n2 // nb),
        in_specs=[blk_a, full((1, d)), full((2 * n1, n1)), full((n1, n2)), full((n1, n2))],
        out_specs=[blk_ya, blk_ya],
        out_shape=[jax.ShapeDtypeStruct((b, n1, n2 * d), BF16)] * 2,
        compiler_params=_cparams(("parallel", "parallel")),
    )(x.reshape(b, n1, n2, d), g.reshape(1, d), f1, twr, twi)

    kb = SUBLANES
    blk_y = pl.BlockSpec((None, kb * n2, d), lambda bi, i: (bi, i, 0))
    blk_x = pl.BlockSpec((None, n2, kb, d), lambda bi, i: (bi, 0, i, 0))
    out = pl.pallas_call(
        functools.partial(_fourier_b_kernel, kb=kb, n2=n2),
        grid=(b, n1 // kb),
        in_specs=[blk_y, blk_y, full((2 * n2, 2 * n2)), full((d, d)), full((d, d)), blk_x],
        out_specs=blk_x,
        out_shape=jax.ShapeDtypeStruct((b, n2, n1, d), F32),
        compiler_params=_cparams(("parallel", "parallel")),
    )(yr.reshape(b, s, d), yi.reshape(b, s, d), f2, wa, wb, x.reshape(b, n2, n1, d))
    return out.reshape(b, s, d)


def _router_kernel(h_ref, g_ref, wrt_ref, hn_ref, aff_ref):
    hn = _rms(h_ref[...], g_ref[...])
    hn_ref[...] = hn.astype(BF16)
    logits = lax.dot_general(wrt_ref[...], hn, (((1,), (1,)), ((), ())),
                             precision=HIGHEST, preferred_element_type=F32)
    ex = jnp.exp(logits - jnp.max(logits, axis=0, keepdims=True))
    aff_ref[...] = ex / jnp.sum(ex, axis=0, keepdims=True)


def _select_kernel(a_ref, pos_ref, cum_ref, *, e, nb, k):
    keys = lax.bitcast_convert_type(a_ref[...], I32).reshape(e, nb, LANES)

    def count(mask):
        part = jnp.sum(mask.astype(F32), axis=2, keepdims=True)
        return jnp.sum(part, axis=1, keepdims=True)

    def value_step(i, t):
        cand = t | (jnp.int32(1) << (30 - i))
        return jnp.where(count(keys >= cand) >= k, cand, t)

    thr = lax.fori_loop(0, 31, value_step, jnp.zeros((e, 1, 1), I32))
    gt = keys > thr
    eq = keys == thr
    need = k - count(gt)
    tok = (lax.broadcasted_iota(I32, (e, nb, LANES), 1) * LANES
           + lax.broadcasted_iota(I32, (e, nb, LANES), 2))
    nbits = (nb * LANES - 1).bit_length()

    def index_step(i, v):
        cand = v | (jnp.int32(1) << (nbits - 1 - i))
        return jnp.where(count(eq & (tok < cand)) < need, cand, v)

    last = lax.fori_loop(0, nbits, index_step, jnp.zeros((e, 1, 1), I32))
    sel = (gt | (eq & (tok <= last))).reshape(e * nb, LANES)
    self32 = sel.astype(F32)

    r = lax.broadcasted_iota(I32, (LANES, LANES), 0)
    c = lax.broadcasted_iota(I32, (LANES, LANES), 1)
    inc = _dot(self32.astype(BF16), (r <= c).astype(BF16))
    tot = jnp.broadcast_to(inc[:, LANES - 1:LANES], (e * nb, LANES)).astype(BF16)
    rb = lax.broadcasted_iota(I32, (nb, nb), 0)
    cb = lax.broadcasted_iota(I32, (nb, nb), 1)
    lower = (cb < rb).astype(BF16)
    before = jnp.concatenate([_dot(lower, tot[x * nb:(x + 1) * nb]) for x in range(e)], axis=0)
    pos = inc - self32 + before
    pos_ref[...] = jnp.where(sel, pos, -1.0).astype(I32)
    cum_ref[...] = before.astype(I32)


def _one_hot_mask(pos_ref, row_tile, rt):
    tt = pos_ref.shape[-1]
    rows = lax.broadcasted_iota(I32, (rt, tt), 0) + row_tile * rt
    return pos_ref[...] == rows


def _gather_kernel(gt, gv, gk, gf, hn_ref, *refs, rt, n_r, group):
    pos_refs, aff_refs, xe_ref, gc_ref = refs[:group], refs[group:2 * group], refs[2 * group], refs[2 * group + 1]
    s = pl.program_id(0)
    tt = pos_refs[0].shape[-1]

    @pl.when(gf[s] != STEP_IDLE)
    def _():
        rows, gate = 0.0, 0.0
        for k in range(group):
            mask = _one_hot_mask(pos_refs[k], gk[s] % n_r, rt) & (gv[s * group + k] == 1)
            tokens = hn_ref[pl.ds(pl.multiple_of(gt[s * group + k] * tt, tt), tt), :]
            rows = rows + _dot(jnp.where(mask, 1.0, 0.0).astype(BF16), tokens)
            gate = gate + jnp.sum(jnp.where(mask, aff_refs[k][...], 0.0), axis=1, keepdims=True)

        @pl.when(gf[s] == STEP_FIRST)
        def _():
            xe_ref[...] = rows.astype(BF16)
            gc_ref[...] = gate

        @pl.when(gf[s] == STEP_NEXT)
        def _():
            xe_ref[...] = (xe_ref[...].astype(F32) + rows).astype(BF16)
            gc_ref[...] += gate


def _ffn_kernel(xe_ref, wg_ref, wu_ref, wd_ref, gc_ref, ye_ref, acc_ref):
    j = pl.program_id(2)
    x = xe_ref[...]
    g = _dot(x, wg_ref[...].astype(BF16))
    u = _dot(x, wu_ref[...].astype(BF16))
    hid = (g / (1.0 + jnp.exp(-g))) * u
    part = _dot(hid.astype(BF16), wd_ref[...].astype(BF16))

    @pl.when(j == 0)
    def _():
        acc_ref[...] = part

    @pl.when(j > 0)
    def _():
        acc_ref[...] += part

    @pl.when(j == pl.num_programs(2) - 1)
    def _():
        ye_ref[...] = (acc_ref[...] * gc_ref[...]).astype(BF16)


def _combine_kernel(gi, gv, gk, gf, h_ref, *refs, rt, n_r, group):
    pos_refs, ye_refs, o_ref = refs[:group], refs[group:2 * group], refs[2 * group]
    s = pl.program_id(0)

    @pl.when(gf[s] != STEP_IDLE)
    def _():
        masks = [_one_hot_mask(pos_refs[k], gi[s * group + k] % n_r, rt) & (gv[s * group + k] == 1)
                 for k in range(group)]
        onehot = jnp.where(jnp.concatenate(masks, axis=0), 1.0, 0.0).astype(BF16)
        ye = jnp.concatenate([ye_refs[k][...] for k in range(group)], axis=0)
        back = lax.dot_general(onehot, ye, (((0,), (0,)), ((), ())), preferred_element_type=F32)

        @pl.when(gf[s] == STEP_FIRST)
        def _():
            o_ref[...] = h_ref[...] + back

        @pl.when(gf[s] == STEP_NEXT)
        def _():
            o_ref[...] += back


def _group_schedule(member, n_true, group):
    n_keys, n_items = member.shape
    n_steps = (n_true + (group - 1) * n_keys) // group
    flat = member.reshape(-1)
    idx = jnp.nonzero(flat, size=n_true, fill_value=0)[0].astype(I32)
    valid = jnp.arange(n_true, dtype=I32) < jnp.sum(flat.astype(I32))
    key_of, item_of = idx // n_items, idx % n_items
    count = jnp.sum(member, axis=1).astype(I32)
    steps_k = (count + group - 1) // group
    step_base = jnp.cumsum(steps_k) - steps_k
    rank = jnp.arange(n_true, dtype=I32) - (jnp.cumsum(count) - count)[key_of]
    slot = jnp.where(valid, (step_base[key_of] + rank // group) * group + rank % group, n_steps * group)
    fill = lambda x: jnp.zeros((n_steps * group,), I32).at[slot].set(x, mode="drop")
    step = jnp.arange(n_steps, dtype=I32)
    key = jnp.minimum(jnp.searchsorted(step_base, step, side="right").astype(I32) - 1, n_keys - 1)
    flag = jnp.where(step >= jnp.sum(steps_k), STEP_IDLE,
                     jnp.where(step == step_base[key], STEP_FIRST, STEP_NEXT)).astype(I32)
    return fill(item_of), fill(valid.astype(I32)), key, flag


def _moe_layer(h, g, w_router, w_gate, w_up, w_down, layer):
    n, d = h.shape
    e = N_EXPERTS
    cap = CAPACITY_FACTOR * n // e
    f = w_gate.shape[-1]
    nb = n // LANES
    tn = min(1024, n)

    hn, aff = pl.pallas_call(
        _router_kernel,
        grid=(n // tn,),
        in_specs=[pl.BlockSpec((tn, d), lambda i: (i, 0)),
                  pl.BlockSpec((1, d), lambda i: (0, 0)),
                  pl.BlockSpec((e, d), lambda i: (0, 0))],
        out_specs=[pl.BlockSpec((tn, d), lambda i: (i, 0)),
                   pl.BlockSpec((e, tn), lambda i: (0, i))],
        out_shape=[jax.ShapeDtypeStruct((n, d), BF16), jax.ShapeDtypeStruct((e, n), F32)],
        compiler_params=_cparams(("parallel",)),
    )(h, g.reshape(1, d), w_router.T)

    whole = pl.BlockSpec((e * nb, LANES), lambda i: (0, 0))
    pos, cum = pl.pallas_call(
        functools.partial(_select_kernel, e=e, nb=nb, k=cap),
        grid=(1,),
        in_specs=[whole],
        out_specs=[whole, whole],
        out_shape=[jax.ShapeDtypeStruct((e * nb, LANES), I32)] * 2,
        compiler_params=_cparams(("arbitrary",)),
    )(aff.reshape(e * nb, LANES))

    rt = min(256, cap)
    tt = min(1024, n)
    n_r, n_t = cap // rt, n // tt
    lo = cum[:, 0].reshape(e, nb)[:, ::tt // LANES]
    hi = jnp.concatenate([lo[:, 1:], jnp.full((e, 1), cap, I32)], axis=1)
    r_lo = (jnp.arange(n_r, dtype=I32) * rt)[None, :, None]
    inter = (lo[:, None, :] < r_lo + rt) & (hi[:, None, :] > r_lo) & (hi > lo)[:, None, :]
    n_true = e * (n_r + n_t - 1)
    r_any = jnp.clip(lo[0] // rt, 0, n_r - 1)
    forced = (jnp.arange(e)[:, None, None] == 0) & (jnp.arange(n_r)[None, :, None] == r_any[None, None, :])
    grp = MOE_GROUP
    at, av, ak, af = _group_schedule(inter.reshape(e * n_r, n_t), n_true, grp)
    ci, cv, ck, cf = _group_schedule(jnp.transpose(inter | forced, (2, 0, 1)).reshape(n_t, e * n_r),
                                     n_true + n_t, grp)

    pos3 = pos.reshape(e * n_t, 1, tt)
    aff3 = aff.reshape(e * n_t, 1, tt)
    row_spec = lambda k: pl.BlockSpec(
        (None, 1, tt), lambda s, at, av, ak, af: (ak[s] // n_r * n_t + at[s * grp + k], 0, 0))
    xe, gc = pl.pallas_call(
        functools.partial(_gather_kernel, rt=rt, n_r=n_r, group=grp),
        grid_spec=pltpu.PrefetchScalarGridSpec(
            num_scalar_prefetch=4,
            grid=(ak.shape[0],),
            in_specs=[pl.BlockSpec((n, d), lambda s, at, av, ak, af: (0, 0), pipeline_mode=pl.Buffered(1))]
                     + [row_spec(k) for k in range(grp)] * 2,
            out_specs=[pl.BlockSpec((None, rt, d), lambda s, at, av, ak, af: (ak[s] // n_r, ak[s] % n_r, 0)),
                       pl.BlockSpec((None, rt, 1), lambda s, at, av, ak, af: (ak[s] // n_r, ak[s] % n_r, 0))],
        ),
        out_shape=[jax.ShapeDtypeStruct((e, cap, d), BF16), jax.ShapeDtypeStruct((e, cap, 1), F32)],
        compiler_params=_cparams(("arbitrary",)),
    )(at, av, ak, af, hn, *([pos3] * grp), *([aff3] * grp))

    tm = min(2048, cap)
    tf = 256
    ye = pl.pallas_call(
        _ffn_kernel,
        grid=(e, cap // tm, f // tf),
        in_specs=[pl.BlockSpec((None, tm, d), lambda x, r, j: (x, r, 0)),
                  pl.BlockSpec((None, None, d, tf), lambda x, r, j: (layer, x, 0, j)),
                  pl.BlockSpec((None, None, d, tf), lambda x, r, j: (layer, x, 0, j)),
                  pl.BlockSpec((None, None, tf, d), lambda x, r, j: (layer, x, j, 0)),
                  pl.BlockSpec((None, tm, 1), lambda x, r, j: (x, r, 0))],
        out_specs=pl.BlockSpec((None, tm, d), lambda x, r, j: (x, r, 0)),
        out_shape=jax.ShapeDtypeStruct((e, cap, d), BF16),
        scratch_shapes=[pltpu.VMEM((tm, d), F32)],
        compiler_params=_cparams(("parallel", "parallel", "arbitrary")),
    )(xe, w_gate, w_up, w_down, gc)

    pos_spec = lambda k: pl.BlockSpec(
        (None, 1, tt), lambda s, ci, cv, ck, cf: (ci[s * grp + k] // n_r * n_t + ck[s], 0, 0))
    ye_spec = lambda k: pl.BlockSpec(
        (None, rt, d), lambda s, ci, cv, ck, cf: (ci[s * grp + k] // n_r, ci[s * grp + k] % n_r, 0))
    tile_spec = pl.BlockSpec((tt, d), lambda s, ci, cv, ck, cf: (ck[s], 0))
    return pl.pallas_call(
        functools.partial(_combine_kernel, rt=rt, n_r=n_r, group=grp),
        grid_spec=pltpu.PrefetchScalarGridSpec(
            num_scalar_prefetch=4,
            grid=(ck.shape[0],),
            in_specs=[tile_spec] + [pos_spec(k) for k in range(grp)] + [ye_spec(k) for k in range(grp)],
            out_specs=tile_spec,
        ),
        out_shape=jax.ShapeDtypeStruct((n, d), F32),
        compiler_params=_cparams(("arbitrary",)),
    )(ci, cv, ck, cf, h, *([pos3] * grp), *([ye] * grp))


def _t5_bucket_np(rel):
    nb = N_BUCKETS // 2
    max_exact = nb // 2
    ret = np.where(rel > 0, nb, 0)
    n = np.abs(rel)
    nf = np.maximum(n, 1).astype(np.float64)
    large = max_exact + (np.log(nf / max_exact) / math.log(MAX_DISTANCE / max_exact)
                         * (nb - max_exact)).astype(np.int64)
    large = np.minimum(large, nb - 1)
    return ret + np.where(n < max_exact, n, large)


def _far_distance():
    n = np.arange(1, 4 * MAX_DISTANCE)
    last = _t5_bucket_np(-n)
    return int(n[np.nonzero(last != last[-1])[0][-1]] + 1)


FAR = _far_distance()


LOOKAHEAD = 1
N_BIAS_TILES = 5
assert FAR <= LANES + 1


def _bias_tiles_kernel(near_ref, far_ref, t_ref):
    h = pl.program_id(0)
    d = lax.broadcasted_iota(I32, (LANES, LANES), 0) - lax.broadcasted_iota(I32, (LANES, LANES), 1)
    for idx in range(N_BIAS_TILES):
        rel = d + LANES * (idx - 2)
        tile = jnp.where(rel < 0, far_ref[h, 0], far_ref[h, 1])
        if abs(idx - 2) <= 1:
            tile = lax.fori_loop(
                0, 2 * FAR - 1, lambda r, t: jnp.where(rel == r - (FAR - 1), near_ref[h, r], t), tile)
        t_ref[idx] = tile


def _bias_tiles(rel_bias):
    nh = rel_bias.shape[1]
    near = (rel_bias[_t5_bucket_np(np.arange(-(FAR - 1), FAR))] * LOG2E).T
    far = jnp.stack([rel_bias[N_BUCKETS // 2 - 1], rel_bias[N_BUCKETS - 1]], axis=1) * LOG2E
    smem = pl.BlockSpec(memory_space=pltpu.SMEM)
    tiles = pl.pallas_call(
        _bias_tiles_kernel,
        grid=(nh,),
        in_specs=[smem, smem],
        out_specs=pl.BlockSpec((None, N_BIAS_TILES, LANES, LANES), lambda h: (h, 0, 0, 0)),
        out_shape=jax.ShapeDtypeStruct((nh, N_BIAS_TILES, LANES, LANES), F32),
        compiler_params=_cparams(("arbitrary",)),
    )(near, far)
    return tiles, far


def _qkv_kernel(h_ref, g_ref, w_ref, bd_ref, gq_ref, gk_ref, qt_ref, k_ref, vt_ref, *, d, nh):
    hn = _rms(h_ref[...], g_ref[...]).astype(BF16)
    qkv = _dot(hn, w_ref[...])
    q, k, v = qkv[:, :d], qkv[:, d:2 * d], qkv[:, 2 * d:]

    def head_norm(x, gain):
        ms = _dot((x * x).astype(BF16), bd_ref[...])
        return x * lax.rsqrt(ms + EPS) * gain

    qn = head_norm(q, gq_ref[...]) * (HEAD_DIM ** -0.5 * LOG2E)
    k_ref[...] = head_norm(k, gk_ref[...]).astype(BF16)
    for h in range(nh):
        cols = slice(h * HEAD_BLOCK, (h + 1) * HEAD_BLOCK)
        qt_ref[h] = qn[:, cols].T.astype(BF16)
        vt_ref[h] = v[:, cols].T.astype(BF16)


def _attn_kernel(far_ref, qt_ref, k_ref, vt_ref, tiles_ref, lq1_ref, lk1_ref, lq2_ref, lk2_ref, gs_ref, o_ref,
                 m_ref, l_ref, acc_ref, *, tq, tc, n_chunks, lambda_init):
    h = pl.program_id(1)
    i = pl.program_id(2)
    j = pl.program_id(3)

    @pl.when(j == 0)
    def _():
        m_ref[...] = jnp.full(m_ref.shape, -1e30, F32)
        l_ref[...] = jnp.zeros(l_ref.shape, F32)
        acc_ref[...] = jnp.zeros(acc_ref.shape, F32)

    qt = qt_ref[...]
    row = lax.broadcasted_iota(I32, qt.shape, 0)
    comps = (jnp.where(row < HEAD_DIM, qt, jnp.zeros_like(qt)),
             jnp.where(row >= HEAD_DIM, qt, jnp.zeros_like(qt)))
    q0 = i * tq
    k_first = j * n_chunks * tc
    ones = jnp.ones((ONES_ROWS, tc), BF16)

    def far_side(k0):
        before = k0 + tc - 1 - q0 <= -FAR
        after = k0 - (q0 + tq - 1) >= FAR
        return before, after

    def scores(jj, with_band):
        kc = k_ref[jj * tc:(jj + 1) * tc, :]
        s = [_dot(kc, comps[c]) for c in range(2)]
        if with_band:
            blk = (k_first + jj * tc - q0) // LANES
            mid = N_BIAS_TILES // 2
            bias = jnp.concatenate(
                [jnp.concatenate([tiles_ref[jnp.clip(blk + a - b, -mid, mid) + mid]
                                  for b in range(tq // LANES)], axis=1)
                 for a in range(tc // LANES)], axis=0)
            s = [x + bias for x in s]
        return s

    def run(with_band):
        state = [(m_ref[c], l_ref[c], acc_ref[c]) for c in range(2)]
        ahead = [scores(jj, with_band) for jj in range(min(LOOKAHEAD, n_chunks))]
        for jj in range(n_chunks):
            s_now = ahead.pop(0)
            if jj + LOOKAHEAD < n_chunks:
                ahead.append(scores(jj + LOOKAHEAD, with_band))
            rows = slice(jj * tc, (jj + 1) * tc)
            if with_band:
                const = 0.0
            else:
                before, _ = far_side(k_first + jj * tc)
                const = jnp.where(before, far_ref[h, 0], far_ref[h, 1])
            vt1 = jnp.concatenate([vt_ref[:, rows], ones], axis=0)
            for c in range(2):
                s = s_now[c]
                m_old, l_old, acc_old = state[c]
                m_new = jnp.maximum(m_old, jnp.max(s, axis=0, keepdims=True) + const)
                p = jnp.exp2(s - (m_new - const)).astype(BF16)
                alpha = jnp.exp2(m_old - m_new)
                pv = _dot(vt1, p)
                state[c] = (m_new, alpha * l_old + pv[HEAD_BLOCK:HEAD_BLOCK + 1],
                            alpha * acc_old + pv[:HEAD_BLOCK])
        for c in range(2):
            m_ref[c], l_ref[c], acc_ref[c] = state[c]

    last_before, _ = far_side(k_first + (n_chunks - 1) * tc)
    _, first_after = far_side(k_first)
    all_far = last_before | first_after

    @pl.when(all_far)
    def _():
        run(False)

    @pl.when(jnp.logical_not(all_far))
    def _():
        run(True)

    @pl.when(j == pl.num_programs(3) - 1)
    def _():
        lam = (jnp.exp(jnp.sum(lq1_ref[...] * lk1_ref[...], keepdims=True))
               - jnp.exp(jnp.sum(lq2_ref[...] * lk2_ref[...], keepdims=True)) + lambda_init)
        a = acc_ref[0] / l_ref[0] - lam * (acc_ref[1] / l_ref[1])
        a = a * lax.rsqrt(jnp.mean(a * a, axis=0, keepdims=True) + EPS) * gs_ref[...]
        o_ref[...] = (a * (1.0 - lambda_init)).T.astype(BF16)


def _proj_kernel(o_ref, w_ref, h_ref, out_ref):
    out_ref[...] = h_ref[...] + _dot(o_ref[...], w_ref[...])


def _attn_tiles(s):
    tq = min(1024, s)
    tc = min(512, s)
    assert tq % LANES == 0 and tc % LANES == 0
    return tq, tc


def _attention_layer(h, g, w_qkv, w_o, g_q, g_k, lq1, lk1, lq2, lk2, g_sub, rel_bias, lambda_init):
    b, s, d = h.shape
    nh = d // HEAD_BLOCK
    tn = min(512, s)
    hd = np.arange(d) // HEAD_DIM
    bd = jnp.asarray((hd[:, None] == hd[None, :]) / HEAD_DIM, BF16)
    tile_gain = lambda x: jnp.tile(x, d // HEAD_DIM).reshape(1, d)
    const2 = lambda shape: pl.BlockSpec(shape, lambda bi, i: (0,) * len(shape))
    qt, k, vt = pl.pallas_call(
        functools.partial(_qkv_kernel, d=d, nh=nh),
        grid=(b, s // tn),
        in_specs=[pl.BlockSpec((None, tn, d), lambda bi, i: (bi, i, 0)),
                  const2((1, d)), const2((d, 3 * d)), const2((d, d)), const2((1, d)), const2((1, d))],
        out_specs=[pl.BlockSpec((None, nh, HEAD_BLOCK, tn), lambda bi, i: (bi, 0, 0, i)),
                   pl.BlockSpec((None, tn, d), lambda bi, i: (bi, i, 0)),
                   pl.BlockSpec((None, nh, HEAD_BLOCK, tn), lambda bi, i: (bi, 0, 0, i))],
        out_shape=[jax.ShapeDtypeStruct((b, nh, HEAD_BLOCK, s), BF16),
                   jax.ShapeDtypeStruct((b, s, d), BF16),
                   jax.ShapeDtypeStruct((b, nh, HEAD_BLOCK, s), BF16)],
        compiler_params=_cparams(("parallel", "parallel")),
    )(h, g.reshape(1, d), w_qkv.astype(BF16), bd, tile_gain(g_q), tile_gain(g_k))

    tq, tc = _attn_tiles(s)
    tko = min(2048, s)
    n_chunks = tko // tc
    tiles, far = _bias_tiles(rel_bias)
    const4 = lambda shape: pl.BlockSpec(shape, lambda bi, hi, i, j: (0,) * len(shape))
    lam_spec = const4((1, HEAD_DIM))
    o = pl.pallas_call(
        functools.partial(_attn_kernel, tq=tq, tc=tc, n_chunks=n_chunks, lambda_init=lambda_init),
        grid=(b, nh, s // tq, s // tko),
        in_specs=[pl.BlockSpec(memory_space=pltpu.SMEM),
                  pl.BlockSpec((None, None, HEAD_BLOCK, tq), lambda bi, hi, i, j: (bi, hi, 0, i)),
                  pl.BlockSpec((None, tko, HEAD_BLOCK), lambda bi, hi, i, j: (bi, j, hi)),
                  pl.BlockSpec((None, None, HEAD_BLOCK, tko), lambda bi, hi, i, j: (bi, hi, 0, j)),
                  pl.BlockSpec((None, N_BIAS_TILES, LANES, LANES), lambda bi, hi, i, j: (hi, 0, 0, 0)),
                  lam_spec, lam_spec, lam_spec, lam_spec,
                  const4((HEAD_BLOCK, 1))],
        out_specs=pl.BlockSpec((None, tq, HEAD_BLOCK), lambda bi, hi, i, j: (bi, i, hi)),
        out_shape=jax.ShapeDtypeStruct((b, s, d), BF16),
        scratch_shapes=[pltpu.VMEM((2, 1, tq), F32), pltpu.VMEM((2, 1, tq), F32),
                        pltpu.VMEM((2, HEAD_BLOCK, tq), F32)],
        compiler_params=_cparams(("parallel", "parallel", "parallel", "arbitrary")),
    )(far, qt, k, vt, tiles, lq1.reshape(1, -1), lk1.reshape(1, -1), lq2.reshape(1, -1), lk2.reshape(1, -1),
      g_sub.reshape(HEAD_BLOCK, 1))

    n = b * s
    tp = min(1024, n)
    out = pl.pallas_call(
        _proj_kernel,
        grid=(n // tp,),
        in_specs=[pl.BlockSpec((tp, d), lambda i: (i, 0)),
                  pl.BlockSpec((d, d), lambda i: (0, 0)),
                  pl.BlockSpec((tp, d), lambda i: (i, 0))],
        out_specs=pl.BlockSpec((tp, d), lambda i: (i, 0)),
        out_shape=jax.ShapeDtypeStruct((n, d), F32),
        compiler_params=_cparams(("parallel",)),
    )(o.reshape(n, d), w_o.astype(BF16), h.reshape(n, d))
    return out.reshape(b, s, d)


def _trunk(x, rel_bias, wa, wb, g_mix, g_ffn, w_qkv, w_attn_out, g_q, g_k, lq1, lk1, lq2, lk2, g_sub,
           w_router, w_gate, w_up, w_down):
    b, s, d = x.shape
    moe = lambda h, i: _moe_layer(h.reshape(b * s, d), g_ffn[i], w_router[i], w_gate, w_up, w_down,
                                  i).reshape(b, s, d)
    h = _fourier_layer(x, g_mix[0], wa, wb)
    h = moe(h, 0)
    lambda_init = 0.8 - 0.6 * math.exp(-0.3 * 1)
    h = _attention_layer(h, g_mix[1], w_qkv[0], w_attn_out[0], g_q[0], g_k[0], lq1[0], lk1[0], lq2[0],
                         lk2[0], g_sub[0], rel_bias, lambda_init)
    return moe(h, 1)


def kernel(x_prompt, x_sample, rel_bias, g_mix, g_ffn, w_fourier_out, w_qkv, w_attn_out, g_q, g_k,
           lambda_q1, lambda_k1, lambda_q2, lambda_k2, g_sub, w_router, w_gate, w_up, w_down):
    assert g_mix.shape[0] == 2, "one Fourier layer followed by one attention layer"
    wa, wb = _fold_w(w_fourier_out[0])
    run = lambda x: _trunk(x, rel_bias, wa, wb, g_mix, g_ffn, w_qkv, w_attn_out, g_q, g_k, lambda_q1, lambda_k1,
                           lambda_q2, lambda_k2, g_sub, w_router, w_gate, w_up, w_down)
    return (run(x_prompt), run(x_sample))
```

```python
import functools
import math

import numpy as np
import jax
import jax.numpy as jnp
from jax import lax
from jax.experimental import pallas as pl
from jax.experimental.pallas import tpu as pltpu

EPS = 1e-6
F32 = jnp.float32
BF16 = jnp.bfloat16
I32 = jnp.int32
HIGHEST = lax.Precision.HIGHEST
LOG2E = math.log2(math.e)

LANES = 128
SUBLANES = 8
FOURIER_GROUP_DIM = 128
HEAD_BLOCK = 128
HEAD_DIM = 64
ONES_ROWS = 16
N_EXPERTS = 16
CAPACITY_FACTOR = 2
GATHER_GROUP = 1
COMBINE_GROUP = 4
STEP_NEXT, STEP_FIRST, STEP_IDLE = 0, 1, 2
N_BUCKETS = 32
MAX_DISTANCE = 128
VMEM_LIMIT = 56 * 1024 * 1024


def _cparams(sem):
    return pltpu.CompilerParams(dimension_semantics=sem, vmem_limit_bytes=VMEM_LIMIT)


def _rms(x, g):
    return x * lax.rsqrt(jnp.mean(x * x, axis=-1, keepdims=True) + EPS) * g


def _dot(a, b):
    return jnp.dot(a, b, preferred_element_type=F32)


def _cos_sin(n):
    k = np.arange(n)
    ang = 2.0 * np.pi * ((k[:, None] * k[None, :]) % n) / n
    return np.cos(ang), np.sin(ang)


def _fold_w_kernel(cc_ref, sc_ref, w_ref, wa_ref, wb_ref):
    w = w_ref[...]
    wa_ref[...] = jnp.dot(cc_ref[...], w, precision=HIGHEST, preferred_element_type=F32).astype(BF16)
    wb_ref[...] = jnp.dot(sc_ref[...], w, precision=HIGHEST, preferred_element_type=F32).astype(BF16)


def _fold_w(w):
    d = w.shape[0]
    gd = FOURIER_GROUP_DIM
    c, s = _cos_sin(gd)
    cc = jnp.asarray(c / math.sqrt(gd), F32)
    sc = jnp.asarray(s / math.sqrt(gd), F32)
    const = pl.BlockSpec((gd, gd), lambda g: (0, 0))
    blk = pl.BlockSpec((gd, d), lambda g: (g, 0))
    return pl.pallas_call(
        _fold_w_kernel,
        grid=(d // gd,),
        in_specs=[const, const, blk],
        out_specs=[blk, blk],
        out_shape=[jax.ShapeDtypeStruct((d, d), BF16)] * 2,
        compiler_params=_cparams(("arbitrary",)),
    )(cc, sc, w)


def _fourier_a_kernel(x_ref, g_ref, f1_ref, twr_ref, twi_ref, yr_ref, yi_ref, *, nb, n1, d):
    j = pl.program_id(1)
    lane = lax.broadcasted_iota(I32, twr_ref.shape, 1)
    for c in range(nb):
        xn = _rms(x_ref[:, c, :], g_ref[...]).astype(BF16)
        y = _dot(f1_ref[...], xn)
        yr, yi = y[:n1], y[n1:]
        sel = lane == (j * nb + c)
        tr = jnp.sum(jnp.where(sel, twr_ref[...], 0.0), axis=1, keepdims=True)
        ti = jnp.sum(jnp.where(sel, twi_ref[...], 0.0), axis=1, keepdims=True)
        cols = slice(c * d, (c + 1) * d)
        yr_ref[:, cols] = (yr * tr - yi * ti).astype(BF16)
        yi_ref[:, cols] = (yr * ti + yi * tr).astype(BF16)


def _fourier_b_kernel(yr_ref, yi_ref, f2_ref, wa_ref, wb_ref, x_ref, o_ref, *, kb, n2):
    ars, ais = [], []
    for c in range(kb):
        rows = slice(c * n2, (c + 1) * n2)
        ys = jnp.concatenate([yr_ref[rows, :], yi_ref[rows, :]], axis=0)
        a = _dot(f2_ref[...], ys)
        ars.append(a[:n2].astype(BF16))
        ais.append(a[n2:].astype(BF16))
    ar = jnp.concatenate(ars, axis=0)
    ai = jnp.concatenate(ais, axis=0)
    out = _dot(ar, wa_ref[...]) + _dot(ai, wb_ref[...])
    for c in range(kb):
        o_ref[:, c, :] = x_ref[:, c, :] + out[c * n2:(c + 1) * n2]


def _split_len(s):
    n1 = 1 << (int(math.log2(s)) // 2)
    assert s % n1 == 0
    return n1, s // n1


def _fourier_layer(x, g, wa, wb):
    b, s, d = x.shape
    n1, n2 = _split_len(s)
    c1, s1 = _cos_sin(n1)
    c2, s2 = _cos_sin(n2)
    f1 = jnp.asarray(np.concatenate([c1, -s1], axis=0) / math.sqrt(n1), BF16)
    f2 = jnp.asarray(np.block([[c2, s2], [-s2, c2]]) / math.sqrt(n2), BF16)
    ang = 2.0 * np.pi * ((np.arange(n1)[:, None] * np.arange(n2)[None, :]) % s) / s
    twr = jnp.asarray(np.cos(ang), F32)
    twi = jnp.asarray(-np.sin(ang), F32)

    nb = SUBLANES
    blk_a = pl.BlockSpec((None, n1, nb, d), lambda bi, j: (bi, 0, j, 0))
    blk_ya = pl.BlockSpec((None, n1, nb * d), lambda bi, j: (bi, 0, j))
    full = lambda shape: pl.BlockSpec(shape, lambda bi, j: (0,) * len(shape))
    yr, yi = pl.pallas_call(
        functools.partial(_fourier_a_kernel, nb=nb, n1=n1, d=d),
        grid=(b, n2 // nb),
        in_specs=[blk_a, full((1, d)), full((2 * n1, n1)), full((n1, n2)), full((n1, n2))],
        out_specs=[blk_ya, blk_ya],
        out_shape=[jax.ShapeDtypeStruct((b, n1, n2 * d), BF16)] * 2,
        compiler_params=_cparams(("parallel", "parallel")),
    )(x.reshape(b, n1, n2, d), g.reshape(1, d), f1, twr, twi)

    kb = SUBLANES
    blk_y = pl.BlockSpec((None, kb * n2, d), lambda bi, i: (bi, i, 0))
    blk_x = pl.BlockSpec((None, n2, kb, d), lambda bi, i: (bi, 0, i, 0))
    out = pl.pallas_call(
        functools.partial(_fourier_b_kernel, kb=kb, n2=n2),
        grid=(b, n1 // kb),
        in_specs=[blk_y, blk_y, full((2 * n2, 2 * n2)), full((d, d)), full((d, d)), blk_x],
        out_specs=blk_x,
        out_shape=jax.ShapeDtypeStruct((b, n2, n1, d), F32),
        compiler_params=_cparams(("parallel", "parallel")),
    )(yr.reshape(b, s, d), yi.reshape(b, s, d), f2, wa, wb, x.reshape(b, n2, n1, d))
    return out.reshape(b, s, d)


def _router_kernel(h_ref, g_ref, wrt_ref, hn_ref, aff_ref):
    hn = _rms(h_ref[...], g_ref[...])
    hn_ref[...] = hn.astype(BF16)
    logits = lax.dot_general(wrt_ref[...], hn, (((1,), (1,)), ((), ())),
                             precision=HIGHEST, preferred_element_type=F32)
    ex = jnp.exp(logits - jnp.max(logits, axis=0, keepdims=True))
    aff_ref[...] = ex / jnp.sum(ex, axis=0, keepdims=True)


def _select_kernel(a_ref, pos_ref, cum_ref, *, e, nb, k):
    keys = lax.bitcast_convert_type(a_ref[...], I32).reshape(e, nb, LANES)

    def count(mask):
        part = jnp.sum(mask.astype(F32), axis=2, keepdims=True)
        return jnp.sum(part, axis=1, keepdims=True)

    def value_step(i, t):
        cand = t | (jnp.int32(1) << (30 - i))
        return jnp.where(count(keys >= cand) >= k, cand, t)

    thr = lax.fori_loop(0, 31, value_step, jnp.zeros((e, 1, 1), I32))
    gt = keys > thr
    eq = keys == thr
    need = k - count(gt)
    tok = (lax.broadcasted_iota(I32, (e, nb, LANES), 1) * LANES
           + lax.broadcasted_iota(I32, (e, nb, LANES), 2))
    nbits = (nb * LANES - 1).bit_length()

    def index_step(i, v):
        cand = v | (jnp.int32(1) << (nbits - 1 - i))
        return jnp.where(count(eq & (tok < cand)) < need, cand, v)

    last = lax.fori_loop(0, nbits, index_step, jnp.zeros((e, 1, 1), I32))
    sel = (gt | (eq & (tok <= last))).reshape(e * nb, LANES)
    self32 = sel.astype(F32)

    r = lax.broadcasted_iota(I32, (LANES, LANES), 0)
    c = lax.broadcasted_iota(I32, (LANES, LANES), 1)
    inc = _dot(self32.astype(BF16), (r <= c).astype(BF16))
    tot = jnp.broadcast_to(inc[:, LANES - 1:LANES], (e * nb, LANES)).astype(BF16)
    rb = lax.broadcasted_iota(I32, (nb, nb), 0)
    cb = lax.broadcasted_iota(I32, (nb, nb), 1)
    lower = (cb < rb).astype(BF16)
    before = jnp.concatenate([_dot(lower, tot[x * nb:(x + 1) * nb]) for x in range(e)], axis=0)
    pos = inc - self32 + before
    pos_ref[...] = jnp.where(sel, pos, -1.0).astype(I32)
    cum_ref[...] = before.astype(I32)


def _one_hot_mask(pos_ref, row_tile, rt):
    tt = pos_ref.shape[-1]
    rows = lax.broadcasted_iota(I32, (rt, tt), 0) + row_tile * rt
    return pos_ref[...] == rows


def _gather_kernel(gt, gv, gk, gf, hn_ref, *refs, rt, n_r, group):
    pos_refs, aff_refs, xe_ref, gc_ref = refs[:group], refs[group:2 * group], refs[2 * group], refs[2 * group + 1]
    s = pl.program_id(0)
    tt = pos_refs[0].shape[-1]

    @pl.when(gf[s] != STEP_IDLE)
    def _():
        rows, gate = 0.0, 0.0
        for k in range(group):
            mask = _one_hot_mask(pos_refs[k], gk[s] % n_r, rt) & (gv[s * group + k] == 1)
            tokens = hn_ref[pl.ds(pl.multiple_of(gt[s * group + k] * tt, tt), tt), :]
            rows = rows + _dot(jnp.where(mask, 1.0, 0.0).astype(BF16), tokens)
            gate = gate + jnp.sum(jnp.where(mask, aff_refs[k][...], 0.0), axis=1, keepdims=True)

        @pl.when(gf[s] == STEP_FIRST)
        def _():
            xe_ref[...] = rows.astype(BF16)
            gc_ref[...] = gate

        @pl.when(gf[s] == STEP_NEXT)
        def _():
            xe_ref[...] = (xe_ref[...].astype(F32) + rows).astype(BF16)
            gc_ref[...] += gate


def _ffn_kernel(xe_ref, wg_ref, wu_ref, wd_ref, gc_ref, ye_ref, acc_ref):
    j = pl.program_id(2)
    x = xe_ref[...]
    g = _dot(x, wg_ref[...].astype(BF16))
    u = _dot(x, wu_ref[...].astype(BF16))
    hid = (g / (1.0 + jnp.exp(-g))) * u
    part = _dot(hid.astype(BF16), wd_ref[...].astype(BF16))

    @pl.when(j == 0)
    def _():
        acc_ref[...] = part

    @pl.when(j > 0)
    def _():
        acc_ref[...] += part

    @pl.when(j == pl.num_programs(2) - 1)
    def _():
        ye_ref[...] = (acc_ref[...] * gc_ref[...]).astype(BF16)


def _combine_kernel(gi, gv, gk, gf, h_ref, *refs, rt, n_r, group):
    pos_refs, ye_refs, o_ref = refs[:group], refs[group:2 * group], refs[2 * group]
    s = pl.program_id(0)

    @pl.when(gf[s] != STEP_IDLE)
    def _():
        masks = [_one_hot_mask(pos_refs[k], gi[s * group + k] % n_r, rt) & (gv[s * group + k] == 1)
                 for k in range(group)]
        onehot = jnp.where(jnp.concatenate(masks, axis=0), 1.0, 0.0).astype(BF16)
        ye = jnp.concatenate([ye_refs[k][...] for k in range(group)], axis=0)
        back = lax.dot_general(onehot, ye, (((0,), (0,)), ((), ())), preferred_element_type=F32)

        @pl.when(gf[s] == STEP_FIRST)
        def _():
            o_ref[...] = h_ref[...] + back

        @pl.when(gf[s] == STEP_NEXT)
        def _():
            o_ref[...] += back


def _schedule_kernel(lo_ref, *out_refs, n_e, n_r, n_t, rt, cap, groups):
    def emitter(item_ref, valid_ref, key_ref, flag_ref, group):
        for i_ref, fill in ((item_ref, 0), (valid_ref, 0), (key_ref, 0), (flag_ref, STEP_IDLE)):
            def clear(i, c, i_ref=i_ref, fill=fill):
                i_ref[i] = fill
                return c
            lax.fori_loop(0, i_ref.shape[0], clear, 0)

        def emit(state, key, item):
            step, cnt, cur = state
            fresh = key != cur
            step = step + (fresh & (cnt > 0)).astype(I32)
            cnt = jnp.where(fresh, 0, cnt)
            item_ref[step * group + cnt] = item
            valid_ref[step * group + cnt] = 1

            @pl.when(cnt == 0)
            def _():
                key_ref[step] = key
                flag_ref[step] = jnp.where(fresh, STEP_FIRST, STEP_NEXT)

            full = cnt + 1 == group
            return step + full.astype(I32), jnp.where(full, 0, cnt + 1), key

        def finish(state):
            step, cnt, cur = state
            last = step + (cnt > 0).astype(I32)

            def pad(i, c):
                key_ref[i] = cur
                return c
            lax.fori_loop(last, key_ref.shape[0], pad, 0)

        return emit, finish

    def rows_of(e, t):
        lo = lo_ref[e, t]
        hi = jnp.where(t + 1 < n_t, lo_ref[e, jnp.minimum(t + 1, n_t - 1)], cap)
        return lo, hi

    start = (jnp.int32(0), jnp.int32(0), jnp.int32(-1))
    emit_g, finish_g = emitter(*out_refs[:4], groups[0])

    def gather_tile(i, state):
        e, t = i // n_t, i % n_t
        lo, hi = rows_of(e, t)
        return lax.fori_loop(lo // rt, jnp.where(hi > lo, (hi - 1) // rt + 1, lo // rt),
                             lambda r, st: emit_g(st, e * n_r + r, t), state)

    finish_g(lax.fori_loop(0, n_e * n_t, gather_tile, start))

    emit_c, finish_c = emitter(*out_refs[4:], groups[1])

    def combine_tile(i, state):
        t, e = i // n_e, i % n_e
        lo, hi = rows_of(e, t)
        first = jnp.minimum(lo // rt, n_r - 1)
        stop = jnp.where(hi > lo, (hi - 1) // rt + 1, jnp.where(e == 0, first + 1, first))
        return lax.fori_loop(first, stop, lambda r, st: emit_c(st, t, e * n_r + r), state)

    finish_c(lax.fori_loop(0, n_t * n_e, combine_tile, start))


def _schedules(lo, cap, rt, groups):
    n_e, n_t = lo.shape
    n_r = cap // rt
    n_true = n_e * (n_r + n_t - 1)
    steps = ((n_true + (groups[0] - 1) * n_e * n_r) // groups[0],
             (n_true + n_t + (groups[1] - 1) * n_t) // groups[1])
    sizes = [steps[0] * groups[0]] * 2 + [steps[0]] * 2 + [steps[1] * groups[1]] * 2 + [steps[1]] * 2
    smem = pl.BlockSpec(memory_space=pltpu.SMEM)
    out = pl.pallas_call(
        functools.partial(_schedule_kernel, n_e=n_e, n_r=n_r, n_t=n_t, rt=rt, cap=cap, groups=groups),
        in_specs=[smem],
        out_specs=[smem] * 8,
        out_shape=[jax.ShapeDtypeStruct((k,), I32) for k in sizes],
    )(lo)
    return out[:4], out[4:]


def _moe_layer(h, g, w_router, w_gate, w_up, w_down, layer):
    n, d = h.shape
    e = N_EXPERTS
    cap = CAPACITY_FACTOR * n // e
    f = w_gate.shape[-1]
    nb = n // LANES
    tn = min(1024, n)

    hn, aff = pl.pallas_call(
        _router_kernel,
        grid=(n // tn,),
        in_specs=[pl.BlockSpec((tn, d), lambda i: (i, 0)),
                  pl.BlockSpec((1, d), lambda i: (0, 0)),
                  pl.BlockSpec((e, d), lambda i: (0, 0))],
        out_specs=[pl.BlockSpec((tn, d), lambda i: (i, 0)),
                   pl.BlockSpec((e, tn), lambda i: (0, i))],
        out_shape=[jax.ShapeDtypeStruct((n, d), BF16), jax.ShapeDtypeStruct((e, n), F32)],
        compiler_params=_cparams(("parallel",)),
    )(h, g.reshape(1, d), w_router.T)

    whole = pl.BlockSpec((e * nb, LANES), lambda i: (0, 0))
    pos, cum = pl.pallas_call(
        functools.partial(_select_kernel, e=e, nb=nb, k=cap),
        grid=(1,),
        in_specs=[whole],
        out_specs=[whole, whole],
        out_shape=[jax.ShapeDtypeStruct((e * nb, LANES), I32)] * 2,
        compiler_params=_cparams(("arbitrary",)),
    )(aff.reshape(e * nb, LANES))

    rt = min(256, cap)
    tt = min(1024, n)
    n_r, n_t = cap // rt, n // tt
    lo = cum[:, 0].reshape(e, nb)[:, ::tt // LANES]
    (at, av, ak, af), (ci, cv, ck, cf) = _schedules(lo, cap, rt, (GATHER_GROUP, COMBINE_GROUP))

    pos3 = pos.reshape(e * n_t, 1, tt)
    aff3 = aff.reshape(e * n_t, 1, tt)
    grp = GATHER_GROUP
    row_spec = lambda k: pl.BlockSpec(
        (None, 1, tt), lambda s, at, av, ak, af: (ak[s] // n_r * n_t + at[s * grp + k], 0, 0))
    xe, gc = pl.pallas_call(
        functools.partial(_gather_kernel, rt=rt, n_r=n_r, group=grp),
        grid_spec=pltpu.PrefetchScalarGridSpec(
            num_scalar_prefetch=4,
            grid=(ak.shape[0],),
            in_specs=[pl.BlockSpec((n, d), lambda s, at, av, ak, af: (0, 0), pipeline_mode=pl.Buffered(1))]
                     + [row_spec(k) for k in range(grp)] * 2,
            out_specs=[pl.BlockSpec((None, rt, d), lambda s, at, av, ak, af: (ak[s] // n_r, ak[s] % n_r, 0)),
                       pl.BlockSpec((None, rt, 1), lambda s, at, av, ak, af: (ak[s] // n_r, ak[s] % n_r, 0))],
        ),
        out_shape=[jax.ShapeDtypeStruct((e, cap, d), BF16), jax.ShapeDtypeStruct((e, cap, 1), F32)],
        compiler_params=_cparams(("arbitrary",)),
    )(at, av, ak, af, hn, *([pos3] * grp), *([aff3] * grp))

    tm = min(2048, cap)
    tf = 256
    ye = pl.pallas_call(
        _ffn_kernel,
        grid=(e, cap // tm, f // tf),
        in_specs=[pl.BlockSpec((None, tm, d), lambda x, r, j: (x, r, 0)),
                  pl.BlockSpec((None, None, d, tf), lambda x, r, j: (layer, x, 0, j)),
                  pl.BlockSpec((None, None, d, tf), lambda x, r, j: (layer, x, 0, j)),
                  pl.BlockSpec((None, None, tf, d), lambda x, r, j: (layer, x, j, 0)),
                  pl.BlockSpec((None, tm, 1), lambda x, r, j: (x, r, 0))],
        out_specs=pl.BlockSpec((None, tm, d), lambda x, r, j: (x, r, 0)),
        out_shape=jax.ShapeDtypeStruct((e, cap, d), BF16),
        scratch_shapes=[pltpu.VMEM((tm, d), F32)],
        compiler_params=_cparams(("parallel", "parallel", "arbitrary")),
    )(xe, w_gate, w_up, w_down, gc)

    grp = COMBINE_GROUP
    pos_spec = lambda k: pl.BlockSpec(
        (None, 1, tt), lambda s, ci, cv, ck, cf: (ci[s * grp + k] // n_r * n_t + ck[s], 0, 0))
    ye_spec = lambda k: pl.BlockSpec(
        (None, rt, d), lambda s, ci, cv, ck, cf: (ci[s * grp + k] // n_r, ci[s * grp + k] % n_r, 0))
    tile_spec = pl.BlockSpec((tt, d), lambda s, ci, cv, ck, cf: (ck[s], 0))
    return pl.pallas_call(
        functools.partial(_combine_kernel, rt=rt, n_r=n_r, group=grp),
        grid_spec=pltpu.PrefetchScalarGridSpec(
            num_scalar_prefetch=4,
            grid=(ck.shape[0],),
            in_specs=[tile_spec] + [pos_spec(k) for k in range(grp)] + [ye_spec(k) for k in range(grp)],
            out_specs=tile_spec,
        ),
        out_shape=jax.ShapeDtypeStruct((n, d), F32),
        compiler_params=_cparams(("arbitrary",)),
    )(ci, cv, ck, cf, h, *([pos3] * grp), *([ye] * grp))


def _t5_bucket_np(rel):
    nb = N_BUCKETS // 2
    max_exact = nb // 2
    ret = np.where(rel > 0, nb, 0)
    n = np.abs(rel)
    nf = np.maximum(n, 1).astype(np.float64)
    large = max_exact + (np.log(nf / max_exact) / math.log(MAX_DISTANCE / max_exact)
                         * (nb - max_exact)).astype(np.int64)
    large = np.minimum(large, nb - 1)
    return ret + np.where(n < max_exact, n, large)


def _far_distance():
    n = np.arange(1, 4 * MAX_DISTANCE)
    last = _t5_bucket_np(-n)
    return int(n[np.nonzero(last != last[-1])[0][-1]] + 1)


FAR = _far_distance()


LOOKAHEAD = 1
N_BIAS_TILES = 5
assert FAR <= LANES + 1


def _bias_tiles_kernel(near_ref, far_ref, t_ref):
    h = pl.program_id(0)
    d = lax.broadcasted_iota(I32, (LANES, LANES), 0) - lax.broadcasted_iota(I32, (LANES, LANES), 1)
    for idx in range(N_BIAS_TILES):
        rel = d + LANES * (idx - 2)
        tile = jnp.where(rel < 0, far_ref[h, 0], far_ref[h, 1])
        if abs(idx - 2) <= 1:
            tile = lax.fori_loop(
                0, 2 * FAR - 1, lambda r, t: jnp.where(rel == r - (FAR - 1), near_ref[h, r], t), tile)
        t_ref[idx] = tile


def _bias_tiles(rel_bias):
    nh = rel_bias.shape[1]
    near = (rel_bias[_t5_bucket_np(np.arange(-(FAR - 1), FAR))] * LOG2E).T
    far = jnp.stack([rel_bias[N_BUCKETS // 2 - 1], rel_bias[N_BUCKETS - 1]], axis=1) * LOG2E
    smem = pl.BlockSpec(memory_space=pltpu.SMEM)
    tiles = pl.pallas_call(
        _bias_tiles_kernel,
        grid=(nh,),
        in_specs=[smem, smem],
        out_specs=pl.BlockSpec((None, N_BIAS_TILES, LANES, LANES), lambda h: (h, 0, 0, 0)),
        out_shape=jax.ShapeDtypeStruct((nh, N_BIAS_TILES, LANES, LANES), F32),
        compiler_params=_cparams(("arbitrary",)),
    )(near, far)
    return tiles, far


def _qkv_kernel(h_ref, g_ref, w_ref, bd_ref, gq_ref, gk_ref, qt_ref, k_ref, vt_ref, *, d, nh):
    hn = _rms(h_ref[...], g_ref[...]).astype(BF16)
    qkv = _dot(hn, w_ref[...])
    q, k, v = qkv[:, :d], qkv[:, d:2 * d], qkv[:, 2 * d:]

    def head_norm(x, gain):
        ms = _dot((x * x).astype(BF16), bd_ref[...])
        return x * lax.rsqrt(ms + EPS) * gain

    qn = head_norm(q, gq_ref[...]) * (HEAD_DIM ** -0.5 * LOG2E)
    k_ref[...] = head_norm(k, gk_ref[...]).astype(BF16)
    for h in range(nh):
        cols = slice(h * HEAD_BLOCK, (h + 1) * HEAD_BLOCK)
        qt_ref[h] = qn[:, cols].T.astype(BF16)
        vt_ref[h] = v[:, cols].T.astype(BF16)


def _attn_kernel(far_ref, qt_ref, k_ref, vt_ref, tiles_ref, lq1_ref, lk1_ref, lq2_ref, lk2_ref, gs_ref, o_ref,
                 m_ref, l_ref, acc_ref, *, tq, tc, n_chunks, lambda_init):
    h = pl.program_id(1)
    i = pl.program_id(2)
    j = pl.program_id(3)

    @pl.when(j == 0)
    def _():
        m_ref[...] = jnp.full(m_ref.shape, -1e30, F32)
        l_ref[...] = jnp.zeros(l_ref.shape, F32)
        acc_ref[...] = jnp.zeros(acc_ref.shape, F32)

    qt = qt_ref[...]
    row = lax.broadcasted_iota(I32, qt.shape, 0)
    comps = (jnp.where(row < HEAD_DIM, qt, jnp.zeros_like(qt)),
             jnp.where(row >= HEAD_DIM, qt, jnp.zeros_like(qt)))
    q0 = i * tq
    k_first = j * n_chunks * tc
    ones = jnp.ones((ONES_ROWS, tc), BF16)

    def far_side(k0):
        before = k0 + tc - 1 - q0 <= -FAR
        after = k0 - (q0 + tq - 1) >= FAR
        return before, after

    def scores(jj, with_band):
        kc = k_ref[jj * tc:(jj + 1) * tc, :]
        s = [_dot(kc, comps[c]) for c in range(2)]
        if with_band:
            blk = (k_first + jj * tc - q0) // LANES
            mid = N_BIAS_TILES // 2
            bias = jnp.concatenate(
                [jnp.concatenate([tiles_ref[jnp.clip(blk + a - b, -mid, mid) + mid]
                                  for b in range(tq // LANES)], axis=1)
                 for a in range(tc // LANES)], axis=0)
            s = [x + bias for x in s]
        return s

    def run(with_band):
        state = [(m_ref[c], l_ref[c], acc_ref[c]) for c in range(2)]
        ahead = [scores(jj, with_band) for jj in range(min(LOOKAHEAD, n_chunks))]
        for jj in range(n_chunks):
            s_now = ahead.pop(0)
            if jj + LOOKAHEAD < n_chunks:
                ahead.append(scores(jj + LOOKAHEAD, with_band))
            rows = slice(jj * tc, (jj + 1) * tc)
            if with_band:
                const = 0.0
            else:
                before, _ = far_side(k_first + jj * tc)
                const = jnp.where(before, far_ref[h, 0], far_ref[h, 1])
            vt1 = jnp.concatenate([vt_ref[:, rows], ones], axis=0)
            for c in range(2):
                s = s_now[c]
                m_old, l_old, acc_old = state[c]
                m_new = jnp.maximum(m_old, jnp.max(s, axis=0, keepdims=True) + const)
                p = jnp.exp2(s - (m_new - const)).astype(BF16)
                alpha = jnp.exp2(m_old - m_new)
                pv = _dot(vt1, p)
                state[c] = (m_new, alpha * l_old + pv[HEAD_BLOCK:HEAD_BLOCK + 1],
                            alpha * acc_old + pv[:HEAD_BLOCK])
        for c in range(2):
            m_ref[c], l_ref[c], acc_ref[c] = state[c]

    last_before, _ = far_side(k_first + (n_chunks - 1) * tc)
    _, first_after = far_side(k_first)
    all_far = last_before | first_after

    @pl.when(all_far)
    def _():
        run(False)

    @pl.when(jnp.logical_not(all_far))
    def _():
        run(True)

    @pl.when(j == pl.num_programs(3) - 1)
    def _():
        lam = (jnp.exp(jnp.sum(lq1_ref[...] * lk1_ref[...], keepdims=True))
               - jnp.exp(jnp.sum(lq2_ref[...] * lk2_ref[...], keepdims=True)) + lambda_init)
        a = acc_ref[0] / l_ref[0] - lam * (acc_ref[1] / l_ref[1])
        a = a * lax.rsqrt(jnp.mean(a * a, axis=0, keepdims=True) + EPS) * gs_ref[...]
        o_ref[...] = (a * (1.0 - lambda_init)).T.astype(BF16)


def _proj_kernel(o_ref, w_ref, h_ref, out_ref):
    out_ref[...] = h_ref[...] + _dot(o_ref[...], w_ref[...])


def _attn_tiles(s):
    tq = min(1024, s)
    tc = min(512, s)
    assert tq % LANES == 0 and tc % LANES == 0
    return tq, tc


def _attention_layer(h, g, w_qkv, w_o, g_q, g_k, lq1, lk1, lq2, lk2, g_sub, rel_bias, lambda_init):
    b, s, d = h.shape
    nh = d // HEAD_BLOCK
    tn = min(512, s)
    hd = np.arange(d) // HEAD_DIM
    bd = jnp.asarray((hd[:, None] == hd[None, :]) / HEAD_DIM, BF16)
    tile_gain = lambda x: jnp.tile(x, d // HEAD_DIM).reshape(1, d)
    const2 = lambda shape: pl.BlockSpec(shape, lambda bi, i: (0,) * len(shape))
    qt, k, vt = pl.pallas_call(
        functools.partial(_qkv_kernel, d=d, nh=nh),
        grid=(b, s // tn),
        in_specs=[pl.BlockSpec((None, tn, d), lambda bi, i: (bi, i, 0)),
                  const2((1, d)), const2((d, 3 * d)), const2((d, d)), const2((1, d)), const2((1, d))],
        out_specs=[pl.BlockSpec((None, nh, HEAD_BLOCK, tn), lambda bi, i: (bi, 0, 0, i)),
                   pl.BlockSpec((None, tn, d), lambda bi, i: (bi, i, 0)),
                   pl.BlockSpec((None, nh, HEAD_BLOCK, tn), lambda bi, i: (bi, 0, 0, i))],
        out_shape=[jax.ShapeDtypeStruct((b, nh, HEAD_BLOCK, s), BF16),
                   jax.ShapeDtypeStruct((b, s, d), BF16),
                   jax.ShapeDtypeStruct((b, nh, HEAD_BLOCK, s), BF16)],
        compiler_params=_cparams(("parallel", "parallel")),
    )(h, g.reshape(1, d), w_qkv.astype(BF16), bd, tile_gain(g_q), tile_gain(g_k))

    tq, tc = _attn_tiles(s)
    tko = min(2048, s)
    n_chunks = tko // tc
    tiles, far = _bias_tiles(rel_bias)
    const4 = lambda shape: pl.BlockSpec(shape, lambda bi, hi, i, j: (0,) * len(shape))
    lam_spec = const4((1, HEAD_DIM))
    o = pl.pallas_call(
        functools.partial(_attn_kernel, tq=tq, tc=tc, n_chunks=n_chunks, lambda_init=lambda_init),
        grid=(b, nh, s // tq, s // tko),
        in_specs=[pl.BlockSpec(memory_space=pltpu.SMEM),
                  pl.BlockSpec((None, None, HEAD_BLOCK, tq), lambda bi, hi, i, j: (bi, hi, 0, i)),
                  pl.BlockSpec((None, tko, HEAD_BLOCK), lambda bi, hi, i, j: (bi, j, hi)),
                  pl.BlockSpec((None, None, HEAD_BLOCK, tko), lambda bi, hi, i, j: (bi, hi, 0, j)),
                  pl.BlockSpec((None, N_BIAS_TILES, LANES, LANES), lambda bi, hi, i, j: (hi, 0, 0, 0)),
                  lam_spec, lam_spec, lam_spec, lam_spec,
                  const4((HEAD_BLOCK, 1))],
        out_specs=pl.BlockSpec((None, tq, HEAD_BLOCK), lambda bi, hi, i, j: (bi, i, hi)),
        out_shape=jax.ShapeDtypeStruct((b, s, d), BF16),
        scratch_shapes=[pltpu.VMEM((2, 1, tq), F32), pltpu.VMEM((2, 1, tq), F32),
                        pltpu.VMEM((2, HEAD_BLOCK, tq), F32)],
        compiler_params=_cparams(("parallel", "parallel", "parallel", "arbitrary")),
    )(far, qt, k, vt, tiles, lq1.reshape(1, -1), lk1.reshape(1, -1), lq2.reshape(1, -1), lk2.reshape(1, -1),
      g_sub.reshape(HEAD_BLOCK, 1))

    n = b * s
    tp = min(1024, n)
    out = pl.pallas_call(
        _proj_kernel,
        grid=(n // tp,),
        in_specs=[pl.BlockSpec((tp, d), lambda i: (i, 0)),
                  pl.BlockSpec((d, d), lambda i: (0, 0)),
                  pl.BlockSpec((tp, d), lambda i: (i, 0))],
        out_specs=pl.BlockSpec((tp, d), lambda i: (i, 0)),
        out_shape=jax.ShapeDtypeStruct((n, d), F32),
        compiler_params=_cparams(("parallel",)),
    )(o.reshape(n, d), w_o.astype(BF16), h.reshape(n, d))
    return out.reshape(b, s, d)


def _trunk(x, rel_bias, wa, wb, g_mix, g_ffn, w_qkv, w_attn_out, g_q, g_k, lq1, lk1, lq2, lk2, g_sub,
           w_router, w_gate, w_up, w_down):
    b, s, d = x.shape
    moe = lambda h, i: _moe_layer(h.reshape(b * s, d), g_ffn[i], w_router[i], w_gate, w_up, w_down,
                                  i).reshape(b, s, d)
    h = _fourier_layer(x, g_mix[0], wa, wb)
    h = moe(h, 0)
    lambda_init = 0.8 - 0.6 * math.exp(-0.3 * 1)
    h = _attention_layer(h, g_mix[1], w_qkv[0], w_attn_out[0], g_q[0], g_k[0], lq1[0], lk1[0], lq2[0],
                         lk2[0], g_sub[0], rel_bias, lambda_init)
    return moe(h, 1)


def kernel(x_prompt, x_sample, rel_bias, g_mix, g_ffn, w_fourier_out, w_qkv, w_attn_out, g_q, g_k,
           lambda_q1, lambda_k1, lambda_q2, lambda_k2, g_sub, w_router, w_gate, w_up, w_down):
    assert g_mix.shape[0] == 2, "one Fourier layer followed by one attention layer"
    wa, wb = _fold_w(w_fourier_out[0])
    run = lambda x: _trunk(x, rel_bias, wa, wb, g_mix, g_ffn, w_qkv, w_attn_out, g_q, g_k, lambda_q1, lambda_k1,
                           lambda_q2, lambda_k2, g_sub, w_router, w_gate, w_up, w_down)
    return (run(x_prompt), run(x_sample))
```

```python
import functools
import math

import numpy as np
import jax
import jax.numpy as jnp
from jax import lax
from jax.experimental import pallas as pl
from jax.experimental.pallas import tpu as pltpu

EPS = 1e-6
F32 = jnp.float32
BF16 = jnp.bfloat16
I32 = jnp.int32
HIGHEST = lax.Precision.HIGHEST
LOG2E = math.log2(math.e)

LANES = 128
SUBLANES = 8
FOURIER_GROUP_DIM = 128
HEAD_BLOCK = 128
HEAD_DIM = 64
ONES_ROWS = 16
N_EXPERTS = 16
CAPACITY_FACTOR = 2
FFN_ROW_CHUNK = 512
GATHER_GROUP = 1
COMBINE_GROUP = 8
STEP_NEXT, STEP_FIRST, STEP_IDLE = 0, 1, 2
N_BUCKETS = 32
MAX_DISTANCE = 128
VMEM_LIMIT = 56 * 1024 * 1024


def _cparams(sem):
    return pltpu.CompilerParams(dimension_semantics=sem, vmem_limit_bytes=VMEM_LIMIT)


def _rms(x, g):
    return x * lax.rsqrt(jnp.mean(x * x, axis=-1, keepdims=True) + EPS) * g


def _dot(a, b):
    return jnp.dot(a, b, preferred_element_type=F32)


def _cos_sin(n):
    k = np.arange(n)
    ang = 2.0 * np.pi * ((k[:, None] * k[None, :]) % n) / n
    return np.cos(ang), np.sin(ang)


def _fold_w_kernel(cc_ref, sc_ref, w_ref, wa_ref, wb_ref):
    w = w_ref[...]
    wa_ref[...] = jnp.dot(cc_ref[...], w, precision=HIGHEST, preferred_element_type=F32).astype(BF16)
    wb_ref[...] = jnp.dot(sc_ref[...], w, precision=HIGHEST, preferred_element_type=F32).astype(BF16)


def _fold_w(w):
    d = w.shape[0]
    gd = FOURIER_GROUP_DIM
    c, s = _cos_sin(gd)
    cc = jnp.asarray(c / math.sqrt(gd), F32)
    sc = jnp.asarray(s / math.sqrt(gd), F32)
    const = pl.BlockSpec((gd, gd), lambda g: (0, 0))
    blk = pl.BlockSpec((gd, d), lambda g: (g, 0))
    return pl.pallas_call(
        _fold_w_kernel,
        grid=(d // gd,),
        in_specs=[const, const, blk],
        out_specs=[blk, blk],
        out_shape=[jax.ShapeDtypeStruct((d, d), BF16)] * 2,
        compiler_params=_cparams(("arbitrary",)),
    )(cc, sc, w)


def _fourier_a_kernel(x_ref, g_ref, f1_ref, twr_ref, twi_ref, yr_ref, yi_ref, *, nb, n1, d):
    j = pl.program_id(1)
    lane = lax.broadcasted_iota(I32, twr_ref.shape, 1)
    for c in range(nb):
        xn = _rms(x_ref[:, c, :], g_ref[...]).astype(BF16)
        y = _dot(f1_ref[...], xn)
        yr, yi = y[:n1], y[n1:]
        sel = lane == (j * nb + c)
        tr = jnp.sum(jnp.where(sel, twr_ref[...], 0.0), axis=1, keepdims=True)
        ti = jnp.sum(jnp.where(sel, twi_ref[...], 0.0), axis=1, keepdims=True)
        cols = slice(c * d, (c + 1) * d)
        yr_ref[:, cols] = (yr * tr - yi * ti).astype(BF16)
        yi_ref[:, cols] = (yr * ti + yi * tr).astype(BF16)


def _fourier_b_kernel(yr_ref, yi_ref, f2_ref, wa_ref, wb_ref, x_ref, o_ref, *, kb, n2):
    ars, ais = [], []
    for c in range(kb):
        rows = slice(c * n2, (c + 1) * n2)
        ys = jnp.concatenate([yr_ref[rows, :], yi_ref[rows, :]], axis=0)
        a = _dot(f2_ref[...], ys)
        ars.append(a[:n2].astype(BF16))
        ais.append(a[n2:].astype(BF16))
    ar = jnp.concatenate(ars, axis=0)
    ai = jnp.concatenate(ais, axis=0)
    out = _dot(ar, wa_ref[...]) + _dot(ai, wb_ref[...])
    for c in range(kb):
        o_ref[:, c, :] = x_ref[:, c, :] + out[c * n2:(c + 1) * n2]


def _split_len(s):
    n1 = 1 << (int(math.log2(s)) // 2)
    assert s % n1 == 0
    return n1, s // n1


def _fourier_layer(x, g, wa, wb):
    b, s, d = x.shape
    n1, n2 = _split_len(s)
    c1, s1 = _cos_sin(n1)
    c2, s2 = _cos_sin(n2)
    f1 = jnp.asarray(np.concatenate([c1, -s1], axis=0) / math.sqrt(n1), BF16)
    f2 = jnp.asarray(np.block([[c2, s2], [-s2, c2]]) / math.sqrt(n2), BF16)
    ang = 2.0 * np.pi * ((np.arange(n1)[:, None] * np.arange(n2)[None, :]) % s) / s
    twr = jnp.asarray(np.cos(ang), F32)
    twi = jnp.asarray(-np.sin(ang), F32)

    nb = SUBLANES
    blk_a = pl.BlockSpec((None, n1, nb, d), lambda bi, j: (bi, 0, j, 0))
    blk_ya = pl.BlockSpec((None, n1, nb * d), lambda bi, j: (bi, 0, j))
    full = lambda shape: pl.BlockSpec(shape, lambda bi, j: (0,) * len(shape))
    yr, yi = pl.pallas_call(
        functools.partial(_fourier_a_kernel, nb=nb, n1=n1, d=d),
        grid=(b, n2 // nb),
        in_specs=[blk_a, full((1, d)), full((2 * n1, n1)), full((n1, n2)), full((n1, n2))],
        out_specs=[blk_ya, blk_ya],
        out_shape=[jax.ShapeDtypeStruct((b, n1, n2 * d), BF16)] * 2,
        compiler_params=_cparams(("parallel", "parallel")),
    )(x.reshape(b, n1, n2, d), g.reshape(1, d), f1, twr, twi)

    kb = SUBLANES
    blk_y = pl.BlockSpec((None, kb * n2, d), lambda bi, i: (bi, i, 0))
    blk_x = pl.BlockSpec((None, n2, kb, d), lambda bi, i: (bi, 0, i, 0))
    out = pl.pallas_call(
        functools.partial(_fourier_b_kernel, kb=kb, n2=n2),
        grid=(b, n1 // kb),
        in_specs=[blk_y, blk_y, full((2 * n2, 2 * n2)), full((d, d)), full((d, d)), blk_x],
        out_specs=blk_x,
        out_shape=jax.ShapeDtypeStruct((b, n2, n1, d), F32),
        compiler_params=_cparams(("parallel", "parallel")),
    )(yr.reshape(b, s, d), yi.reshape(b, s, d), f2, wa, wb, x.reshape(b, n2, n1, d))
    return out.reshape(b, s, d)


def _router_kernel(h_ref, g_ref, wrt_ref, hn_ref, aff_ref):
    hn = _rms(h_ref[...], g_ref[...])
    hn_ref[...] = hn.astype(BF16)
    logits = lax.dot_general(wrt_ref[...], hn, (((1,), (1,)), ((), ())),
                             precision=HIGHEST, preferred_element_type=F32)
    ex = jnp.exp(logits - jnp.max(logits, axis=0, keepdims=True))
    aff_ref[...] = ex / jnp.sum(ex, axis=0, keepdims=True)


def _select_kernel(a_ref, pos_ref, cum_ref, *, e, nb, k):
    keys = lax.bitcast_convert_type(a_ref[...], I32).reshape(e, nb, LANES)

    def count(mask):
        part = jnp.sum(mask.astype(F32), axis=2, keepdims=True)
        return jnp.sum(part, axis=1, keepdims=True)

    def value_step(i, t):
        cand = t | (jnp.int32(1) << (30 - i))
        return jnp.where(count(keys >= cand) >= k, cand, t)

    thr = lax.fori_loop(0, 31, value_step, jnp.zeros((e, 1, 1), I32))
    gt = keys > thr
    eq = keys == thr
    need = k - count(gt)
    tok = (lax.broadcasted_iota(I32, (e, nb, LANES), 1) * LANES
           + lax.broadcasted_iota(I32, (e, nb, LANES), 2))
    nbits = (nb * LANES - 1).bit_length()

    def index_step(i, v):
        cand = v | (jnp.int32(1) << (nbits - 1 - i))
        return jnp.where(count(eq & (tok < cand)) < need, cand, v)

    last = lax.fori_loop(0, nbits, index_step, jnp.zeros((e, 1, 1), I32))
    sel = (gt | (eq & (tok <= last))).reshape(e * nb, LANES)
    self32 = sel.astype(F32)

    r = lax.broadcasted_iota(I32, (LANES, LANES), 0)
    c = lax.broadcasted_iota(I32, (LANES, LANES), 1)
    inc = _dot(self32.astype(BF16), (r <= c).astype(BF16))
    tot = jnp.broadcast_to(inc[:, LANES - 1:LANES], (e * nb, LANES)).astype(BF16)
    rb = lax.broadcasted_iota(I32, (nb, nb), 0)
    cb = lax.broadcasted_iota(I32, (nb, nb), 1)
    lower = (cb < rb).astype(BF16)
    before = jnp.concatenate([_dot(lower, tot[x * nb:(x + 1) * nb]) for x in range(e)], axis=0)
    pos = inc - self32 + before
    pos_ref[...] = jnp.where(sel, pos, -1.0).astype(I32)
    cum_ref[...] = before.astype(I32)


def _one_hot_mask(pos_ref, row_tile, rt):
    tt = pos_ref.shape[-1]
    rows = lax.broadcasted_iota(I32, (rt, tt), 0) + row_tile * rt
    return pos_ref[...] == rows


def _gather_kernel(gt, gv, gk, gf, hn_ref, *refs, rt, n_r, group):
    pos_refs, aff_refs, xe_ref, gc_ref = refs[:group], refs[group:2 * group], refs[2 * group], refs[2 * group + 1]
    s = pl.program_id(0)
    tt = pos_refs[0].shape[-1]

    @pl.when(gf[s] != STEP_IDLE)
    def _():
        rows, gate = 0.0, 0.0
        for k in range(group):
            mask = _one_hot_mask(pos_refs[k], gk[s] % n_r, rt) & (gv[s * group + k] == 1)
            tokens = hn_ref[pl.ds(pl.multiple_of(gt[s * group + k] * tt, tt), tt), :]
            rows = rows + _dot(jnp.where(mask, 1.0, 0.0).astype(BF16), tokens)
            gate = gate + jnp.sum(jnp.where(mask, aff_refs[k][...], 0.0), axis=1, keepdims=True)

        @pl.when(gf[s] == STEP_FIRST)
        def _():
            xe_ref[...] = rows.astype(BF16)
            gc_ref[...] = gate

        @pl.when(gf[s] == STEP_NEXT)
        def _():
            xe_ref[...] = (xe_ref[...].astype(F32) + rows).astype(BF16)
            gc_ref[...] += gate


def _ffn_kernel(xe_ref, wg_ref, wu_ref, wd_ref, gc_ref, ye_ref, *acc_refs):
    j = pl.program_id(2)
    n_chunks = len(acc_refs)
    rc = acc_refs[0].shape[0]

    @pl.when(j == 0)
    def _():
        for acc_ref in acc_refs:
            acc_ref[...] = jnp.zeros(acc_ref.shape, F32)

    wg = wg_ref[...].astype(BF16)
    wu = wu_ref[...].astype(BF16)
    wd = wd_ref[...].astype(BF16)

    def gate_up(c):
        x = xe_ref[c * rc:(c + 1) * rc, :]
        return _dot(x, wg), _dot(x, wu)

    ahead = gate_up(0)
    for c in range(n_chunks):
        g, u = ahead
        if c + 1 < n_chunks:
            ahead = gate_up(c + 1)
        hid = (g / (1.0 + jnp.exp(-g))) * u
        acc_refs[c][...] += _dot(hid.astype(BF16), wd)

    @pl.when(j == pl.num_programs(2) - 1)
    def _():
        for c, acc_ref in enumerate(acc_refs):
            rows = slice(c * rc, (c + 1) * rc)
            ye_ref[rows, :] = (acc_ref[...] * gc_ref[rows, :]).astype(BF16)


def _combine_kernel(gi, gv, gk, gf, h_ref, *refs, rt, n_r, group):
    pos_refs, ye_refs, o_ref = refs[:group], refs[group:2 * group], refs[2 * group]
    s = pl.program_id(0)

    @pl.when(gf[s] != STEP_IDLE)
    def _():
        masks = [_one_hot_mask(pos_refs[k], gi[s * group + k] % n_r, rt) & (gv[s * group + k] == 1)
                 for k in range(group)]
        onehot = jnp.where(jnp.concatenate(masks, axis=0), 1.0, 0.0).astype(BF16)
        ye = jnp.concatenate([ye_refs[k][...] for k in range(group)], axis=0)
        back = lax.dot_general(onehot, ye, (((0,), (0,)), ((), ())), preferred_element_type=F32)

        @pl.when(gf[s] == STEP_FIRST)
        def _():
            o_ref[...] = h_ref[...] + back

        @pl.when(gf[s] == STEP_NEXT)
        def _():
            o_ref[...] += back


def _schedule_kernel(lo_ref, *out_refs, n_e, n_r, n_t, rt, cap, groups):
    def emitter(item_ref, valid_ref, key_ref, flag_ref, group):
        for i_ref, fill in ((item_ref, 0), (valid_ref, 0), (key_ref, 0), (flag_ref, STEP_IDLE)):
            def clear(i, c, i_ref=i_ref, fill=fill):
                i_ref[i] = fill
                return c
            lax.fori_loop(0, i_ref.shape[0], clear, 0)

        def emit(state, key, item):
            step, cnt, cur = state
            fresh = key != cur
            step = step + (fresh & (cnt > 0)).astype(I32)
            cnt = jnp.where(fresh, 0, cnt)
            item_ref[step * group + cnt] = item
            valid_ref[step * group + cnt] = 1

            @pl.when(cnt == 0)
            def _():
                key_ref[step] = key
                flag_ref[step] = jnp.where(fresh, STEP_FIRST, STEP_NEXT)

            full = cnt + 1 == group
            return step + full.astype(I32), jnp.where(full, 0, cnt + 1), key

        def finish(state):
            step, cnt, cur = state
            last = step + (cnt > 0).astype(I32)

            def pad(i, c):
                key_ref[i] = cur
                return c
            lax.fori_loop(last, key_ref.shape[0], pad, 0)

        return emit, finish

    def rows_of(e, t):
        lo = lo_ref[e, t]
        hi = jnp.where(t + 1 < n_t, lo_ref[e, jnp.minimum(t + 1, n_t - 1)], cap)
        return lo, hi

    start = (jnp.int32(0), jnp.int32(0), jnp.int32(-1))
    emit_g, finish_g = emitter(*out_refs[:4], groups[0])

    def gather_tile(i, state):
        e, t = i // n_t, i % n_t
        lo, hi = rows_of(e, t)
        return lax.fori_loop(lo // rt, jnp.where(hi > lo, (hi - 1) // rt + 1, lo // rt),
                             lambda r, st: emit_g(st, e * n_r + r, t), state)

    finish_g(lax.fori_loop(0, n_e * n_t, gather_tile, start))

    emit_c, finish_c = emitter(*out_refs[4:], groups[1])

    def combine_tile(i, state):
        t, e = i // n_e, i % n_e
        lo, hi = rows_of(e, t)
        first = jnp.minimum(lo // rt, n_r - 1)
        stop = jnp.where(hi > lo, (hi - 1) // rt + 1, jnp.where(e == 0, first + 1, first))
        return lax.fori_loop(first, stop, lambda r, st: emit_c(st, t, e * n_r + r), state)

    finish_c(lax.fori_loop(0, n_t * n_e, combine_tile, start))


def _schedules(lo, cap, rt, groups):
    n_e, n_t = lo.shape
    n_r = cap // rt
    n_true = n_e * (n_r + n_t - 1)
    steps = ((n_true + (groups[0] - 1) * n_e * n_r) // groups[0],
             (n_true + n_t + (groups[1] - 1) * n_t) // groups[1])
    sizes = [steps[0] * groups[0]] * 2 + [steps[0]] * 2 + [steps[1] * groups[1]] * 2 + [steps[1]] * 2
    smem = pl.BlockSpec(memory_space=pltpu.SMEM)
    out = pl.pallas_call(
        functools.partial(_schedule_kernel, n_e=n_e, n_r=n_r, n_t=n_t, rt=rt, cap=cap, groups=groups),
        in_specs=[smem],
        out_specs=[smem] * 8,
        out_shape=[jax.ShapeDtypeStruct((k,), I32) for k in sizes],
    )(lo)
    return out[:4], out[4:]


def _moe_layer(h, g, w_router, w_gate, w_up, w_down, layer):
    n, d = h.shape
    e = N_EXPERTS
    cap = CAPACITY_FACTOR * n // e
    f = w_gate.shape[-1]
    nb = n // LANES
    tn = min(1024, n)

    hn, aff = pl.pallas_call(
        _router_kernel,
        grid=(n // tn,),
        in_specs=[pl.BlockSpec((tn, d), lambda i: (i, 0)),
                  pl.BlockSpec((1, d), lambda i: (0, 0)),
                  pl.BlockSpec((e, d), lambda i: (0, 0))],
        out_specs=[pl.BlockSpec((tn, d), lambda i: (i, 0)),
                   pl.BlockSpec((e, tn), lambda i: (0, i))],
        out_shape=[jax.ShapeDtypeStruct((n, d), BF16), jax.ShapeDtypeStruct((e, n), F32)],
        compiler_params=_cparams(("parallel",)),
    )(h, g.reshape(1, d), w_router.T)

    whole = pl.BlockSpec((e * nb, LANES), lambda i: (0, 0))
    pos, cum = pl.pallas_call(
        functools.partial(_select_kernel, e=e, nb=nb, k=cap),
        grid=(1,),
        in_specs=[whole],
        out_specs=[whole, whole],
        out_shape=[jax.ShapeDtypeStruct((e * nb, LANES), I32)] * 2,
        compiler_params=_cparams(("arbitrary",)),
    )(aff.reshape(e * nb, LANES))

    rt = min(256, cap)
    tt = min(1024, n)
    n_r, n_t = cap // rt, n // tt
    lo = cum[:, 0].reshape(e, nb)[:, ::tt // LANES]
    (at, av, ak, af), (ci, cv, ck, cf) = _schedules(lo, cap, rt, (GATHER_GROUP, COMBINE_GROUP))

    pos3 = pos.reshape(e * n_t, 1, tt)
    aff3 = aff.reshape(e * n_t, 1, tt)
    grp = GATHER_GROUP
    row_spec = lambda k: pl.BlockSpec(
        (None, 1, tt), lambda s, at, av, ak, af: (ak[s] // n_r * n_t + at[s * grp + k], 0, 0))
    xe, gc = pl.pallas_call(
        functools.partial(_gather_kernel, rt=rt, n_r=n_r, group=grp),
        grid_spec=pltpu.PrefetchScalarGridSpec(
            num_scalar_prefetch=4,
            grid=(ak.shape[0],),
            in_specs=[pl.BlockSpec((n, d), lambda s, at, av, ak, af: (0, 0), pipeline_mode=pl.Buffered(1))]
                     + [row_spec(k) for k in range(grp)] * 2,
            out_specs=[pl.BlockSpec((None, rt, d), lambda s, at, av, ak, af: (ak[s] // n_r, ak[s] % n_r, 0)),
                       pl.BlockSpec((None, rt, 1), lambda s, at, av, ak, af: (ak[s] // n_r, ak[s] % n_r, 0))],
        ),
        out_shape=[jax.ShapeDtypeStruct((e, cap, d), BF16), jax.ShapeDtypeStruct((e, cap, 1), F32)],
        compiler_params=_cparams(("arbitrary",)),
    )(at, av, ak, af, hn, *([pos3] * grp), *([aff3] * grp))

    tm = min(2048, cap)
    tf = 256
    ye = pl.pallas_call(
        _ffn_kernel,
        grid=(e, cap // tm, f // tf),
        in_specs=[pl.BlockSpec((None, tm, d), lambda x, r, j: (x, r, 0)),
                  pl.BlockSpec((None, None, d, tf), lambda x, r, j: (layer, x, 0, j)),
                  pl.BlockSpec((None, None, d, tf), lambda x, r, j: (layer, x, 0, j)),
                  pl.BlockSpec((None, None, tf, d), lambda x, r, j: (layer, x, j, 0)),
                  pl.BlockSpec((None, tm, 1), lambda x, r, j: (x, r, 0))],
        out_specs=pl.BlockSpec((None, tm, d), lambda x, r, j: (x, r, 0)),
        out_shape=jax.ShapeDtypeStruct((e, cap, d), BF16),
        scratch_shapes=[pltpu.VMEM((min(FFN_ROW_CHUNK, tm), d), F32)] * (tm // min(FFN_ROW_CHUNK, tm)),
        compiler_params=_cparams(("parallel", "parallel", "arbitrary")),
    )(xe, w_gate, w_up, w_down, gc)

    grp = COMBINE_GROUP
    pos_spec = lambda k: pl.BlockSpec(
        (None, 1, tt), lambda s, ci, cv, ck, cf: (ci[s * grp + k] // n_r * n_t + ck[s], 0, 0))
    ye_spec = lambda k: pl.BlockSpec(
        (None, rt, d), lambda s, ci, cv, ck, cf: (ci[s * grp + k] // n_r, ci[s * grp + k] % n_r, 0))
    tile_spec = pl.BlockSpec((tt, d), lambda s, ci, cv, ck, cf: (ck[s], 0))
    return pl.pallas_call(
        functools.partial(_combine_kernel, rt=rt, n_r=n_r, group=grp),
        grid_spec=pltpu.PrefetchScalarGridSpec(
            num_scalar_prefetch=4,
            grid=(ck.shape[0],),
            in_specs=[tile_spec] + [pos_spec(k) for k in range(grp)] + [ye_spec(k) for k in range(grp)],
            out_specs=tile_spec,
        ),
        out_shape=jax.ShapeDtypeStruct((n, d), F32),
        compiler_params=_cparams(("arbitrary",)),
    )(ci, cv, ck, cf, h, *([pos3] * grp), *([ye] * grp))


def _t5_bucket_np(rel):
    nb = N_BUCKETS // 2
    max_exact = nb // 2
    ret = np.where(rel > 0, nb, 0)
    n = np.abs(rel)
    nf = np.maximum(n, 1).astype(np.float64)
    large = max_exact + (np.log(nf / max_exact) / math.log(MAX_DISTANCE / max_exact)
                         * (nb - max_exact)).astype(np.int64)
    large = np.minimum(large, nb - 1)
    return ret + np.where(n < max_exact, n, large)


def _far_distance():
    n = np.arange(1, 4 * MAX_DISTANCE)
    last = _t5_bucket_np(-n)
    return int(n[np.nonzero(last != last[-1])[0][-1]] + 1)


FAR = _far_distance()


LOOKAHEAD = 2
N_BIAS_TILES = 5
assert FAR <= LANES + 1


def _bias_tiles_kernel(near_ref, far_ref, t_ref):
    h = pl.program_id(0)
    d = lax.broadcasted_iota(I32, (LANES, LANES), 0) - lax.broadcasted_iota(I32, (LANES, LANES), 1)
    for idx in range(N_BIAS_TILES):
        rel = d + LANES * (idx - 2)
        tile = jnp.where(rel < 0, far_ref[h, 0], far_ref[h, 1])
        if abs(idx - 2) <= 1:
            tile = lax.fori_loop(
                0, 2 * FAR - 1, lambda r, t: jnp.where(rel == r - (FAR - 1), near_ref[h, r], t), tile)
        t_ref[idx] = tile


def _bias_tiles(rel_bias):
    nh = rel_bias.shape[1]
    near = (rel_bias[_t5_bucket_np(np.arange(-(FAR - 1), FAR))] * LOG2E).T
    far = jnp.stack([rel_bias[N_BUCKETS // 2 - 1], rel_bias[N_BUCKETS - 1]], axis=1) * LOG2E
    smem = pl.BlockSpec(memory_space=pltpu.SMEM)
    tiles = pl.pallas_call(
        _bias_tiles_kernel,
        grid=(nh,),
        in_specs=[smem, smem],
        out_specs=pl.BlockSpec((None, N_BIAS_TILES, LANES, LANES), lambda h: (h, 0, 0, 0)),
        out_shape=jax.ShapeDtypeStruct((nh, N_BIAS_TILES, LANES, LANES), F32),
        compiler_params=_cparams(("arbitrary",)),
    )(near, far)
    return tiles, far


def _qkv_kernel(h_ref, g_ref, w_ref, bd_ref, gq_ref, gk_ref, qt_ref, k_ref, vt_ref, *, d, nh):
    hn = _rms(h_ref[...], g_ref[...]).astype(BF16)
    qkv = _dot(hn, w_ref[...])
    q, k, v = qkv[:, :d], qkv[:, d:2 * d], qkv[:, 2 * d:]

    def head_norm(x, gain):
        ms = _dot((x * x).astype(BF16), bd_ref[...])
        return x * lax.rsqrt(ms + EPS) * gain

    qn = head_norm(q, gq_ref[...]) * (HEAD_DIM ** -0.5 * LOG2E)
    k_ref[...] = head_norm(k, gk_ref[...]).astype(BF16)
    for h in range(nh):
        cols = slice(h * HEAD_BLOCK, (h + 1) * HEAD_BLOCK)
        qt_ref[h] = qn[:, cols].T.astype(BF16)
        vt_ref[h] = v[:, cols].T.astype(BF16)


def _attn_kernel(far_ref, qt_ref, k_ref, vt_ref, tiles_ref, lq1_ref, lk1_ref, lq2_ref, lk2_ref, gs_ref, o_ref,
                 m_ref, l_ref, acc_ref, *, tq, tc, n_chunks, lambda_init):
    h = pl.program_id(1)
    i = pl.program_id(2)
    j = pl.program_id(3)

    @pl.when(j == 0)
    def _():
        m_ref[...] = jnp.full(m_ref.shape, -1e30, F32)
        l_ref[...] = jnp.zeros(l_ref.shape, F32)
        acc_ref[...] = jnp.zeros(acc_ref.shape, F32)

    qt = qt_ref[...]
    row = lax.broadcasted_iota(I32, qt.shape, 0)
    comps = (jnp.where(row < HEAD_DIM, qt, jnp.zeros_like(qt)),
             jnp.where(row >= HEAD_DIM, qt, jnp.zeros_like(qt)))
    q0 = i * tq
    k_first = j * n_chunks * tc
    ones = jnp.ones((ONES_ROWS, tc), BF16)

    def far_side(k0):
        before = k0 + tc - 1 - q0 <= -FAR
        after = k0 - (q0 + tq - 1) >= FAR
        return before, after

    def scores(m, with_band):
        jj, c = m // 2, m % 2
        s = _dot(k_ref[jj * tc:(jj + 1) * tc, :], comps[c])
        if with_band:
            blk = (k_first + jj * tc - q0) // LANES
            mid = N_BIAS_TILES // 2
            s = s + jnp.concatenate(
                [jnp.concatenate([tiles_ref[jnp.clip(blk + a - b, -mid, mid) + mid]
                                  for b in range(tq // LANES)], axis=1)
                 for a in range(tc // LANES)], axis=0)
        return s

    def run(with_band):
        state = [(m_ref[c], l_ref[c], acc_ref[c]) for c in range(2)]
        n_maps = 2 * n_chunks
        ahead = [scores(m, with_band) for m in range(min(LOOKAHEAD, n_maps))]
        for m in range(n_maps):
            jj, c = m // 2, m % 2
            s = ahead.pop(0)
            if m + LOOKAHEAD < n_maps:
                ahead.append(scores(m + LOOKAHEAD, with_band))
            rows = slice(jj * tc, (jj + 1) * tc)
            if with_band:
                const = 0.0
            else:
                before, _ = far_side(k_first + jj * tc)
                const = jnp.where(before, far_ref[h, 0], far_ref[h, 1])
            vt1 = jnp.concatenate([vt_ref[:, rows], ones], axis=0)
            m_old, l_old, acc_old = state[c]
            m_new = jnp.maximum(m_old, jnp.max(s, axis=0, keepdims=True) + const)
            p = jnp.exp2(s - (m_new - const)).astype(BF16)
            alpha = jnp.exp2(m_old - m_new)
            pv = _dot(vt1, p)
            state[c] = (m_new, alpha * l_old + pv[HEAD_BLOCK:HEAD_BLOCK + 1],
                        alpha * acc_old + pv[:HEAD_BLOCK])
        for c in range(2):
            m_ref[c], l_ref[c], acc_ref[c] = state[c]

    last_before, _ = far_side(k_first + (n_chunks - 1) * tc)
    _, first_after = far_side(k_first)
    all_far = last_before | first_after

    @pl.when(all_far)
    def _():
        run(False)

    @pl.when(jnp.logical_not(all_far))
    def _():
        run(True)

    @pl.when(j == pl.num_programs(3) - 1)
    def _():
        lam = (jnp.exp(jnp.sum(lq1_ref[...] * lk1_ref[...], keepdims=True))
               - jnp.exp(jnp.sum(lq2_ref[...] * lk2_ref[...], keepdims=True)) + lambda_init)
        a = acc_ref[0] / l_ref[0] - lam * (acc_ref[1] / l_ref[1])
        a = a * lax.rsqrt(jnp.mean(a * a, axis=0, keepdims=True) + EPS) * gs_ref[...]
        o_ref[...] = (a * (1.0 - lambda_init)).T.astype(BF16)


def _proj_kernel(o_ref, w_ref, h_ref, out_ref):
    out_ref[...] = h_ref[...] + _dot(o_ref[...], w_ref[...])


def _attn_tiles(s):
    tq = min(1024, s)
    tc = min(512, s)
    assert tq % LANES == 0 and tc % LANES == 0
    return tq, tc


def _attention_layer(h, g, w_qkv, w_o, g_q, g_k, lq1, lk1, lq2, lk2, g_sub, rel_bias, lambda_init):
    b, s, d = h.shape
    nh = d // HEAD_BLOCK
    tn = min(512, s)
    hd = np.arange(d) // HEAD_DIM
    bd = jnp.asarray((hd[:, None] == hd[None, :]) / HEAD_DIM, BF16)
    tile_gain = lambda x: jnp.tile(x, d // HEAD_DIM).reshape(1, d)
    const2 = lambda shape: pl.BlockSpec(shape, lambda bi, i: (0,) * len(shape))
    qt, k, vt = pl.pallas_call(
        functools.partial(_qkv_kernel, d=d, nh=nh),
        grid=(b, s // tn),
        in_specs=[pl.BlockSpec((None, tn, d), lambda bi, i: (bi, i, 0)),
                  const2((1, d)), const2((d, 3 * d)), const2((d, d)), const2((1, d)), const2((1, d))],
        out_specs=[pl.BlockSpec((None, nh, HEAD_BLOCK, tn), lambda bi, i: (bi, 0, 0, i)),
                   pl.BlockSpec((None, tn, d), lambda bi, i: (bi, i, 0)),
                   pl.BlockSpec((None, nh, HEAD_BLOCK, tn), lambda bi, i: (bi, 0, 0, i))],
        out_shape=[jax.ShapeDtypeStruct((b, nh, HEAD_BLOCK, s), BF16),
                   jax.ShapeDtypeStruct((b, s, d), BF16),
                   jax.ShapeDtypeStruct((b, nh, HEAD_BLOCK, s), BF16)],
        compiler_params=_cparams(("parallel", "parallel")),
    )(h, g.reshape(1, d), w_qkv.astype(BF16), bd, tile_gain(g_q), tile_gain(g_k))

    tq, tc = _attn_tiles(s)
    tko = min(2048, s)
    n_chunks = tko // tc
    tiles, far = _bias_tiles(rel_bias)
    const4 = lambda shape: pl.BlockSpec(shape, lambda bi, hi, i, j: (0,) * len(shape))
    lam_spec = const4((1, HEAD_DIM))
    o = pl.pallas_call(
        functools.partial(_attn_kernel, tq=tq, tc=tc, n_chunks=n_chunks, lambda_init=lambda_init),
        grid=(b, nh, s // tq, s // tko),
        in_specs=[pl.BlockSpec(memory_space=pltpu.SMEM),
                  pl.BlockSpec((None, None, HEAD_BLOCK, tq), lambda bi, hi, i, j: (bi, hi, 0, i)),
                  pl.BlockSpec((None, tko, HEAD_BLOCK), lambda bi, hi, i, j: (bi, j, hi)),
                  pl.BlockSpec((None, None, HEAD_BLOCK, tko), lambda bi, hi, i, j: (bi, hi, 0, j)),
                  pl.BlockSpec((None, N_BIAS_TILES, LANES, LANES), lambda bi, hi, i, j: (hi, 0, 0, 0)),
                  lam_spec, lam_spec, lam_spec, lam_spec,
                  const4((HEAD_BLOCK, 1))],
        out_specs=pl.BlockSpec((None, tq, HEAD_BLOCK), lambda bi, hi, i, j: (bi, i, hi)),
        out_shape=jax.ShapeDtypeStruct((b, s, d), BF16),
        scratch_shapes=[pltpu.VMEM((2, 1, tq), F32), pltpu.VMEM((2, 1, tq), F32),
                        pltpu.VMEM((2, HEAD_BLOCK, tq), F32)],
        compiler_params=_cparams(("parallel", "parallel", "parallel", "arbitrary")),
    )(far, qt, k, vt, tiles, lq1.reshape(1, -1), lk1.reshape(1, -1), lq2.reshape(1, -1), lk2.reshape(1, -1),
      g_sub.reshape(HEAD_BLOCK, 1))

    n = b * s
    tp = min(1024, n)
    out = pl.pallas_call(
        _proj_kernel,
        grid=(n // tp,),
        in_specs=[pl.BlockSpec((tp, d), lambda i: (i, 0)),
                  pl.BlockSpec((d, d), lambda i: (0, 0)),
                  pl.BlockSpec((tp, d), lambda i: (i, 0))],
        out_specs=pl.BlockSpec((tp, d), lambda i: (i, 0)),
        out_shape=jax.ShapeDtypeStruct((n, d), F32),
        compiler_params=_cparams(("parallel",)),
    )(o.reshape(n, d), w_o.astype(BF16), h.reshape(n, d))
    return out.reshape(b, s, d)


def _trunk(x, rel_bias, wa, wb, g_mix, g_ffn, w_qkv, w_attn_out, g_q, g_k, lq1, lk1, lq2, lk2, g_sub,
           w_router, w_gate, w_up, w_down):
    b, s, d = x.shape
    moe = lambda h, i: _moe_layer(h.reshape(b * s, d), g_ffn[i], w_router[i], w_gate, w_up, w_down,
                                  i).reshape(b, s, d)
    h = _fourier_layer(x, g_mix[0], wa, wb)
    h = moe(h, 0)
    lambda_init = 0.8 - 0.6 * math.exp(-0.3 * 1)
    h = _attention_layer(h, g_mix[1], w_qkv[0], w_attn_out[0], g_q[0], g_k[0], lq1[0], lk1[0], lq2[0],
                         lk2[0], g_sub[0], rel_bias, lambda_init)
    return moe(h, 1)


def kernel(x_prompt, x_sample, rel_bias, g_mix, g_ffn, w_fourier_out, w_qkv, w_attn_out, g_q, g_k,
           lambda_q1, lambda_k1, lambda_q2, lambda_k2, g_sub, w_router, w_gate, w_up, w_down):
    assert g_mix.shape[0] == 2, "one Fourier layer followed by one attention layer"
    wa, wb = _fold_w(w_fourier_out[0])
    run = lambda x: _trunk(x, rel_bias, wa, wb, g_mix, g_ffn, w_qkv, w_attn_out, g_q, g_k, lambda_q1, lambda_k1,
                           lambda_q2, lambda_k2, g_sub, w_router, w_gate, w_up, w_down)
    return (run(x_prompt), run(x_sample))
```

```python
import functools
import math

import numpy as np
import jax
import jax.numpy as jnp
from jax import lax
from jax.experimental import pallas as pl
from jax.experimental.pallas import tpu as pltpu

EPS = 1e-6
F32 = jnp.float32
BF16 = jnp.bfloat16
I32 = jnp.int32
HIGHEST = lax.Precision.HIGHEST
LOG2E = math.log2(math.e)

LANES = 128
SUBLANES = 8
FOURIER_GROUP_DIM = 128
HEAD_BLOCK = 128
HEAD_DIM = 64
ONES_ROWS = 16
N_EXPERTS = 16
CAPACITY_FACTOR = 2
FFN_ROW_CHUNK = 512
GATHER_GROUP = 1
COMBINE_GROUP = 8
STEP_NEXT, STEP_FIRST, STEP_IDLE = 0, 1, 2
N_BUCKETS = 32
MAX_DISTANCE = 128
VMEM_LIMIT = 56 * 1024 * 1024


def _cparams(sem):
    return pltpu.CompilerParams(dimension_semantics=sem, vmem_limit_bytes=VMEM_LIMIT)


def _rms(x, g):
    return x * lax.rsqrt(jnp.mean(x * x, axis=-1, keepdims=True) + EPS) * g


def _dot(a, b):
    return jnp.dot(a, b, preferred_element_type=F32)


def _cos_sin(n):
    k = np.arange(n)
    ang = 2.0 * np.pi * ((k[:, None] * k[None, :]) % n) / n
    return np.cos(ang), np.sin(ang)


def _fold_w_kernel(cc_ref, sc_ref, w_ref, wa_ref, wb_ref):
    w = w_ref[...]
    wa_ref[...] = jnp.dot(cc_ref[...], w, precision=HIGHEST, preferred_element_type=F32).astype(BF16)
    wb_ref[...] = jnp.dot(sc_ref[...], w, precision=HIGHEST, preferred_element_type=F32).astype(BF16)


def _fold_w(w):
    d = w.shape[0]
    gd = FOURIER_GROUP_DIM
    c, s = _cos_sin(gd)
    cc = jnp.asarray(c / math.sqrt(gd), F32)
    sc = jnp.asarray(s / math.sqrt(gd), F32)
    const = pl.BlockSpec((gd, gd), lambda g: (0, 0))
    blk = pl.BlockSpec((gd, d), lambda g: (g, 0))
    return pl.pallas_call(
        _fold_w_kernel,
        grid=(d // gd,),
        in_specs=[const, const, blk],
        out_specs=[blk, blk],
        out_shape=[jax.ShapeDtypeStruct((d, d), BF16)] * 2,
        compiler_params=_cparams(("arbitrary",)),
    )(cc, sc, w)


def _fourier_a_kernel(x_ref, g_ref, f1_ref, twr_ref, twi_ref, yr_ref, yi_ref, *, nb, n1, d):
    j = pl.program_id(1)
    lane = lax.broadcasted_iota(I32, twr_ref.shape, 1)
    for c in range(nb):
        xn = _rms(x_ref[:, c, :], g_ref[...]).astype(BF16)
        y = _dot(f1_ref[...], xn)
        yr, yi = y[:n1], y[n1:]
        sel = lane == (j * nb + c)
        tr = jnp.sum(jnp.where(sel, twr_ref[...], 0.0), axis=1, keepdims=True)
        ti = jnp.sum(jnp.where(sel, twi_ref[...], 0.0), axis=1, keepdims=True)
        cols = slice(c * d, (c + 1) * d)
        yr_ref[:, cols] = (yr * tr - yi * ti).astype(BF16)
        yi_ref[:, cols] = (yr * ti + yi * tr).astype(BF16)


def _fourier_b_kernel(yr_ref, yi_ref, f2_ref, wa_ref, wb_ref, x_ref, o_ref, *, kb, n2):
    ars, ais = [], []
    for c in range(kb):
        rows = slice(c * n2, (c + 1) * n2)
        ys = jnp.concatenate([yr_ref[rows, :], yi_ref[rows, :]], axis=0)
        a = _dot(f2_ref[...], ys)
        ars.append(a[:n2].astype(BF16))
        ais.append(a[n2:].astype(BF16))
    ar = jnp.concatenate(ars, axis=0)
    ai = jnp.concatenate(ais, axis=0)
    out = _dot(ar, wa_ref[...]) + _dot(ai, wb_ref[...])
    for c in range(kb):
        o_ref[:, c, :] = x_ref[:, c, :] + out[c * n2:(c + 1) * n2]


def _split_len(s):
    n1 = 1 << (int(math.log2(s)) // 2)
    assert s % n1 == 0
    return n1, s // n1


def _fourier_layer(x, g, wa, wb):
    b, s, d = x.shape
    n1, n2 = _split_len(s)
    c1, s1 = _cos_sin(n1)
    c2, s2 = _cos_sin(n2)
    f1 = jnp.asarray(np.concatenate([c1, -s1], axis=0) / math.sqrt(n1), BF16)
    f2 = jnp.asarray(np.block([[c2, s2], [-s2, c2]]) / math.sqrt(n2), BF16)
    ang = 2.0 * np.pi * ((np.arange(n1)[:, None] * np.arange(n2)[None, :]) % s) / s
    twr = jnp.asarray(np.cos(ang), F32)
    twi = jnp.asarray(-np.sin(ang), F32)

    nb = SUBLANES
    blk_a = pl.BlockSpec((None, n1, nb, d), lambda bi, j: (bi, 0, j, 0))
    blk_ya = pl.BlockSpec((None, n1, nb * d), lambda bi, j: (bi, 0, j))
    full = lambda shape: pl.BlockSpec(shape, lambda bi, j: (0,) * len(shape))
    yr, yi = pl.pallas_call(
        functools.partial(_fourier_a_kernel, nb=nb, n1=n1, d=d),
        grid=(b, n2 // nb),
        in_specs=[blk_a, full((1, d)), full((2 * n1, n1)), full((n1, n2)), full((n1, n2))],
        out_specs=[blk_ya, blk_ya],
        out_shape=[jax.ShapeDtypeStruct((b, n1, n2 * d), BF16)] * 2,
        compiler_params=_cparams(("parallel", "parallel")),
    )(x.reshape(b, n1, n2, d), g.reshape(1, d), f1, twr, twi)

    kb = SUBLANES
    blk_y = pl.BlockSpec((None, kb * n2, d), lambda bi, i: (bi, i, 0))
    blk_x = pl.BlockSpec((None, n2, kb, d), lambda bi, i: (bi, 0, i, 0))
    out = pl.pallas_call(
        functools.partial(_fourier_b_kernel, kb=kb, n2=n2),
        grid=(b, n1 // kb),
        in_specs=[blk_y, blk_y, full((2 * n2, 2 * n2)), full((d, d)), full((d, d)), blk_x],
        out_specs=blk_x,
        out_shape=jax.ShapeDtypeStruct((b, n2, n1, d), F32),
        compiler_params=_cparams(("parallel", "parallel")),
    )(yr.reshape(b, s, d), yi.reshape(b, s, d), f2, wa, wb, x.reshape(b, n2, n1, d))
    return out.reshape(b, s, d)


def _router_kernel(h_ref, g_ref, wrt_ref, hn_ref, aff_ref):
    hn = _rms(h_ref[...], g_ref[...])
    hn_ref[...] = hn.astype(BF16)
    logits = lax.dot_general(wrt_ref[...], hn, (((1,), (1,)), ((), ())),
                             precision=HIGHEST, preferred_element_type=F32)
    ex = jnp.exp(logits - jnp.max(logits, axis=0, keepdims=True))
    aff_ref[...] = ex / jnp.sum(ex, axis=0, keepdims=True)


def _select_kernel(a_ref, pos_ref, cum_ref, *, e, nb, k):
    keys = lax.bitcast_convert_type(a_ref[...], I32).reshape(e, nb, LANES)

    def count(mask):
        part = jnp.sum(mask.astype(F32), axis=2, keepdims=True)
        return jnp.sum(part, axis=1, keepdims=True)

    def value_step(i, t):
        cand = t | (jnp.int32(1) << (30 - i))
        return jnp.where(count(keys >= cand) >= k, cand, t)

    thr = lax.fori_loop(0, 31, value_step, jnp.zeros((e, 1, 1), I32))
    gt = keys > thr
    eq = keys == thr
    need = k - count(gt)
    tok = (lax.broadcasted_iota(I32, (e, nb, LANES), 1) * LANES
           + lax.broadcasted_iota(I32, (e, nb, LANES), 2))
    nbits = (nb * LANES - 1).bit_length()

    def index_step(i, v):
        cand = v | (jnp.int32(1) << (nbits - 1 - i))
        return jnp.where(count(eq & (tok < cand)) < need, cand, v)

    last = lax.fori_loop(0, nbits, index_step, jnp.zeros((e, 1, 1), I32))
    sel = (gt | (eq & (tok <= last))).reshape(e * nb, LANES)
    self32 = sel.astype(F32)

    r = lax.broadcasted_iota(I32, (LANES, LANES), 0)
    c = lax.broadcasted_iota(I32, (LANES, LANES), 1)
    inc = _dot(self32.astype(BF16), (r <= c).astype(BF16))
    tot = jnp.broadcast_to(inc[:, LANES - 1:LANES], (e * nb, LANES)).astype(BF16)
    rb = lax.broadcasted_iota(I32, (nb, nb), 0)
    cb = lax.broadcasted_iota(I32, (nb, nb), 1)
    lower = (cb < rb).astype(BF16)
    before = jnp.concatenate([_dot(lower, tot[x * nb:(x + 1) * nb]) for x in range(e)], axis=0)
    pos = inc - self32 + before
    pos_ref[...] = jnp.where(sel, pos, -1.0).astype(I32)
    cum_ref[...] = before.astype(I32)


def _one_hot_mask(pos_ref, row_tile, rt):
    tt = pos_ref.shape[-1]
    rows = lax.broadcasted_iota(I32, (rt, tt), 0) + row_tile * rt
    return pos_ref[...] == rows


def _gather_kernel(gt, gv, gk, gf, hn_ref, *refs, rt, n_r, group):
    pos_refs, aff_refs, xe_ref, gc_ref = refs[:group], refs[group:2 * group], refs[2 * group], refs[2 * group + 1]
    s = pl.program_id(0)
    tt = pos_refs[0].shape[-1]

    @pl.when(gf[s] != STEP_IDLE)
    def _():
        rows, gate = 0.0, 0.0
        for k in range(group):
            mask = _one_hot_mask(pos_refs[k], gk[s] % n_r, rt) & (gv[s * group + k] == 1)
            tokens = hn_ref[pl.ds(pl.multiple_of(gt[s * group + k] * tt, tt), tt), :]
            rows = rows + _dot(jnp.where(mask, 1.0, 0.0).astype(BF16), tokens)
            gate = gate + jnp.sum(jnp.where(mask, aff_refs[k][...], 0.0), axis=1, keepdims=True)

        @pl.when(gf[s] == STEP_FIRST)
        def _():
            xe_ref[...] = rows.astype(BF16)
            gc_ref[...] = gate

        @pl.when(gf[s] == STEP_NEXT)
        def _():
            xe_ref[...] = (xe_ref[...].astype(F32) + rows).astype(BF16)
            gc_ref[...] += gate


def _ffn_kernel(xe_ref, wg_ref, wu_ref, wd_ref, gc_ref, ye_ref, *acc_refs):
    j = pl.program_id(2)
    n_chunks = len(acc_refs)
    rc = acc_refs[0].shape[0]

    @pl.when(j == 0)
    def _():
        for acc_ref in acc_refs:
            acc_ref[...] = jnp.zeros(acc_ref.shape, F32)

    wg = wg_ref[...].astype(BF16)
    wu = wu_ref[...].astype(BF16)
    wd = wd_ref[...].astype(BF16)

    def gate_up(c):
        x = xe_ref[c * rc:(c + 1) * rc, :]
        return _dot(x, wg), _dot(x, wu)

    ahead = gate_up(0)
    for c in range(n_chunks):
        g, u = ahead
        if c + 1 < n_chunks:
            ahead = gate_up(c + 1)
        hid = (g / (1.0 + jnp.exp(-g))) * u
        acc_refs[c][...] += _dot(hid.astype(BF16), wd)

    @pl.when(j == pl.num_programs(2) - 1)
    def _():
        for c, acc_ref in enumerate(acc_refs):
            rows = slice(c * rc, (c + 1) * rc)
            ye_ref[rows, :] = (acc_ref[...] * gc_ref[rows, :]).astype(BF16)


def _combine_kernel(gi, gv, gk, gf, h_ref, *refs, rt, n_r, group):
    pos_refs, ye_refs, o_ref = refs[:group], refs[group:2 * group], refs[2 * group]
    s = pl.program_id(0)

    @pl.when(gf[s] != STEP_IDLE)
    def _():
        masks = [_one_hot_mask(pos_refs[k], gi[s * group + k] % n_r, rt) & (gv[s * group + k] == 1)
                 for k in range(group)]
        onehot = jnp.where(jnp.concatenate(masks, axis=0), 1.0, 0.0).astype(BF16)
        ye = jnp.concatenate([ye_refs[k][...] for k in range(group)], axis=0)
        back = lax.dot_general(onehot, ye, (((0,), (0,)), ((), ())), preferred_element_type=F32)

        @pl.when(gf[s] == STEP_FIRST)
        def _():
            o_ref[...] = h_ref[...] + back

        @pl.when(gf[s] == STEP_NEXT)
        def _():
            o_ref[...] += back


def _schedule_kernel(lo_ref, *out_refs, n_e, n_r, n_t, rt, cap, groups):
    def emitter(item_ref, valid_ref, key_ref, flag_ref, group):
        for i_ref, fill in ((item_ref, 0), (valid_ref, 0), (key_ref, 0), (flag_ref, STEP_IDLE)):
            def clear(i, c, i_ref=i_ref, fill=fill):
                i_ref[i] = fill
                return c
            lax.fori_loop(0, i_ref.shape[0], clear, 0)

        def emit(state, key, item):
            step, cnt, cur = state
            fresh = key != cur
            step = step + (fresh & (cnt > 0)).astype(I32)
            cnt = jnp.where(fresh, 0, cnt)
            item_ref[step * group + cnt] = item
            valid_ref[step * group + cnt] = 1

            @pl.when(cnt == 0)
            def _():
                key_ref[step] = key
                flag_ref[step] = jnp.where(fresh, STEP_FIRST, STEP_NEXT)

            full = cnt + 1 == group
            return step + full.astype(I32), jnp.where(full, 0, cnt + 1), key

        def finish(state):
            step, cnt, cur = state
            last = step + (cnt > 0).astype(I32)

            def pad(i, c):
                key_ref[i] = cur
                return c
            lax.fori_loop(last, key_ref.shape[0], pad, 0)

        return emit, finish

    def rows_of(e, t):
        lo = lo_ref[e, t]
        hi = jnp.where(t + 1 < n_t, lo_ref[e, jnp.minimum(t + 1, n_t - 1)], cap)
        return lo, hi

    start = (jnp.int32(0), jnp.int32(0), jnp.int32(-1))
    emit_g, finish_g = emitter(*out_refs[:4], groups[0])

    def gather_tile(i, state):
        e, t = i // n_t, i % n_t
        lo, hi = rows_of(e, t)
        return lax.fori_loop(lo // rt, jnp.where(hi > lo, (hi - 1) // rt + 1, lo // rt),
                             lambda r, st: emit_g(st, e * n_r + r, t), state)

    finish_g(lax.fori_loop(0, n_e * n_t, gather_tile, start))

    emit_c, finish_c = emitter(*out_refs[4:], groups[1])

    def combine_tile(i, state):
        t, e = i // n_e, i % n_e
        lo, hi = rows_of(e, t)
        first = jnp.minimum(lo // rt, n_r - 1)
        stop = jnp.where(hi > lo, (hi - 1) // rt + 1, jnp.where(e == 0, first + 1, first))
        return lax.fori_loop(first, stop, lambda r, st: emit_c(st, t, e * n_r + r), state)

    finish_c(lax.fori_loop(0, n_t * n_e, combine_tile, start))


def _schedules(lo, cap, rt, groups):
    n_e, n_t = lo.shape
    n_r = cap // rt
    n_true = n_e * (n_r + n_t - 1)
    steps = ((n_true + (groups[0] - 1) * n_e * n_r) // groups[0],
             (n_true + n_t + (groups[1] - 1) * n_t) // groups[1])
    sizes = [steps[0] * groups[0]] * 2 + [steps[0]] * 2 + [steps[1] * groups[1]] * 2 + [steps[1]] * 2
    smem = pl.BlockSpec(memory_space=pltpu.SMEM)
    out = pl.pallas_call(
        functools.partial(_schedule_kernel, n_e=n_e, n_r=n_r, n_t=n_t, rt=rt, cap=cap, groups=groups),
        in_specs=[smem],
        out_specs=[smem] * 8,
        out_shape=[jax.ShapeDtypeStruct((k,), I32) for k in sizes],
    )(lo)
    return out[:4], out[4:]


def _moe_layer(h, g, w_router, w_gate, w_up, w_down, layer):
    n, d = h.shape
    e = N_EXPERTS
    cap = CAPACITY_FACTOR * n // e
    f = w_gate.shape[-1]
    nb = n // LANES
    tn = min(1024, n)

    hn, aff = pl.pallas_call(
        _router_kernel,
        grid=(n // tn,),
        in_specs=[pl.BlockSpec((tn, d), lambda i: (i, 0)),
                  pl.BlockSpec((1, d), lambda i: (0, 0)),
                  pl.BlockSpec((e, d), lambda i: (0, 0))],
        out_specs=[pl.BlockSpec((tn, d), lambda i: (i, 0)),
                   pl.BlockSpec((e, tn), lambda i: (0, i))],
        out_shape=[jax.ShapeDtypeStruct((n, d), BF16), jax.ShapeDtypeStruct((e, n), F32)],
        compiler_params=_cparams(("parallel",)),
    )(h, g.reshape(1, d), w_router.T)

    whole = pl.BlockSpec((e * nb, LANES), lambda i: (0, 0))
    pos, cum = pl.pallas_call(
        functools.partial(_select_kernel, e=e, nb=nb, k=cap),
        grid=(1,),
        in_specs=[whole],
        out_specs=[whole, whole],
        out_shape=[jax.ShapeDtypeStruct((e * nb, LANES), I32)] * 2,
        compiler_params=_cparams(("arbitrary",)),
    )(aff.reshape(e * nb, LANES))

    rt = min(256, cap)
    tt = min(1024, n)
    n_r, n_t = cap // rt, n // tt
    lo = cum[:, 0].reshape(e, nb)[:, ::tt // LANES]
    (at, av, ak, af), (ci, cv, ck, cf) = _schedules(lo, cap, rt, (GATHER_GROUP, COMBINE_GROUP))

    pos3 = pos.reshape(e * n_t, 1, tt)
    aff3 = aff.reshape(e * n_t, 1, tt)
    grp = GATHER_GROUP
    row_spec = lambda k: pl.BlockSpec(
        (None, 1, tt), lambda s, at, av, ak, af: (ak[s] // n_r * n_t + at[s * grp + k], 0, 0))
    xe, gc = pl.pallas_call(
        functools.partial(_gather_kernel, rt=rt, n_r=n_r, group=grp),
        grid_spec=pltpu.PrefetchScalarGridSpec(
            num_scalar_prefetch=4,
            grid=(ak.shape[0],),
            in_specs=[pl.BlockSpec((n, d), lambda s, at, av, ak, af: (0, 0), pipeline_mode=pl.Buffered(1))]
                     + [row_spec(k) for k in range(grp)] * 2,
            out_specs=[pl.BlockSpec((None, rt, d), lambda s, at, av, ak, af: (ak[s] // n_r, ak[s] % n_r, 0)),
                       pl.BlockSpec((None, rt, 1), lambda s, at, av, ak, af: (ak[s] // n_r, ak[s] % n_r, 0))],
        ),
        out_shape=[jax.ShapeDtypeStruct((e, cap, d), BF16), jax.ShapeDtypeStruct((e, cap, 1), F32)],
        compiler_params=_cparams(("arbitrary",)),
    )(at, av, ak, af, hn, *([pos3] * grp), *([aff3] * grp))

    tm = min(2048, cap)
    tf = 256
    ye = pl.pallas_call(
        _ffn_kernel,
        grid=(e, cap // tm, f // tf),
        in_specs=[pl.BlockSpec((None, tm, d), lambda x, r, j: (x, r, 0)),
                  pl.BlockSpec((None, None, d, tf), lambda x, r, j: (layer, x, 0, j)),
                  pl.BlockSpec((None, None, d, tf), lambda x, r, j: (layer, x, 0, j)),
                  pl.BlockSpec((None, None, tf, d), lambda x, r, j: (layer, x, j, 0)),
                  pl.BlockSpec((None, tm, 1), lambda x, r, j: (x, r, 0))],
        out_specs=pl.BlockSpec((None, tm, d), lambda x, r, j: (x, r, 0)),
        out_shape=jax.ShapeDtypeStruct((e, cap, d), BF16),
        scratch_shapes=[pltpu.VMEM((min(FFN_ROW_CHUNK, tm), d), F32)] * (tm // min(FFN_ROW_CHUNK, tm)),
        compiler_params=_cparams(("parallel", "parallel", "arbitrary")),
    )(xe, w_gate, w_up, w_down, gc)

    grp = COMBINE_GROUP
    pos_spec = lambda k: pl.BlockSpec(
        (None, 1, tt), lambda s, ci, cv, ck, cf: (ci[s * grp + k] // n_r * n_t + ck[s], 0, 0))
    ye_spec = lambda k: pl.BlockSpec(
        (None, rt, d), lambda s, ci, cv, ck, cf: (ci[s * grp + k] // n_r, ci[s * grp + k] % n_r, 0))
    tile_spec = pl.BlockSpec((tt, d), lambda s, ci, cv, ck, cf: (ck[s], 0))
    return pl.pallas_call(
        functools.partial(_combine_kernel, rt=rt, n_r=n_r, group=grp),
        grid_spec=pltpu.PrefetchScalarGridSpec(
            num_scalar_prefetch=4,
            grid=(ck.shape[0],),
            in_specs=[tile_spec] + [pos_spec(k) for k in range(grp)] + [ye_spec(k) for k in range(grp)],
            out_specs=tile_spec,
        ),
        out_shape=jax.ShapeDtypeStruct((n, d), F32),
        compiler_params=_cparams(("arbitrary",)),
    )(ci, cv, ck, cf, h, *([pos3] * grp), *([ye] * grp))


def _t5_bucket_np(rel):
    nb = N_BUCKETS // 2
    max_exact = nb // 2
    ret = np.where(rel > 0, nb, 0)
    n = np.abs(rel)
    nf = np.maximum(n, 1).astype(np.float64)
    large = max_exact + (np.log(nf / max_exact) / math.log(MAX_DISTANCE / max_exact)
                         * (nb - max_exact)).astype(np.int64)
    large = np.minimum(large, nb - 1)
    return ret + np.where(n < max_exact, n, large)


def _far_distance():
    n = np.arange(1, 4 * MAX_DISTANCE)
    last = _t5_bucket_np(-n)
    return int(n[np.nonzero(last != last[-1])[0][-1]] + 1)


FAR = _far_distance()


LOOKAHEAD = 2
N_BIAS_TILES = 5
assert FAR <= LANES + 1


def _bias_tiles_kernel(near_ref, far_ref, t_ref):
    h = pl.program_id(0)
    d = lax.broadcasted_iota(I32, (LANES, LANES), 0) - lax.broadcasted_iota(I32, (LANES, LANES), 1)
    for idx in range(N_BIAS_TILES):
        rel = d + LANES * (idx - 2)
        tile = jnp.where(rel < 0, far_ref[h, 0], far_ref[h, 1])
        if abs(idx - 2) <= 1:
            tile = lax.fori_loop(
                0, 2 * FAR - 1, lambda r, t: jnp.where(rel == r - (FAR - 1), near_ref[h, r], t), tile)
        t_ref[idx] = tile


def _bias_tiles(rel_bias):
    nh = rel_bias.shape[1]
    near = (rel_bias[_t5_bucket_np(np.arange(-(FAR - 1), FAR))] * LOG2E).T
    far = jnp.stack([rel_bias[N_BUCKETS // 2 - 1], rel_bias[N_BUCKETS - 1]], axis=1) * LOG2E
    smem = pl.BlockSpec(memory_space=pltpu.SMEM)
    tiles = pl.pallas_call(
        _bias_tiles_kernel,
        grid=(nh,),
        in_specs=[smem, smem],
        out_specs=pl.BlockSpec((None, N_BIAS_TILES, LANES, LANES), lambda h: (h, 0, 0, 0)),
        out_shape=jax.ShapeDtypeStruct((nh, N_BIAS_TILES, LANES, LANES), F32),
        compiler_params=_cparams(("arbitrary",)),
    )(near, far)
    return tiles, far


def _qkv_kernel(h_ref, g_ref, w_ref, bd_ref, gq_ref, gk_ref, qt_ref, k_ref, vt_ref, *, d, nh):
    hn = _rms(h_ref[...], g_ref[...]).astype(BF16)
    qkv = _dot(hn, w_ref[...])
    q, k, v = qkv[:, :d], qkv[:, d:2 * d], qkv[:, 2 * d:]

    def head_norm(x, gain):
        ms = _dot((x * x).astype(BF16), bd_ref[...])
        return x * lax.rsqrt(ms + EPS) * gain

    qn = head_norm(q, gq_ref[...]) * (HEAD_DIM ** -0.5 * LOG2E)
    k_ref[...] = head_norm(k, gk_ref[...]).astype(BF16)
    for h in range(nh):
        cols = slice(h * HEAD_BLOCK, (h + 1) * HEAD_BLOCK)
        qt_ref[h] = qn[:, cols].T.astype(BF16)
        vt_ref[h] = v[:, cols].T.astype(BF16)


def _attn_kernel(far_ref, qt_ref, k_ref, vt_ref, tiles_ref, lq1_ref, lk1_ref, lq2_ref, lk2_ref, gs_ref, o_ref,
                 m_ref, l_ref, acc_ref, *, tq, tc, n_chunks, lambda_init):
    h = pl.program_id(1)
    i = pl.program_id(2)
    j = pl.program_id(3)

    @pl.when(j == 0)
    def _():
        m_ref[...] = jnp.full(m_ref.shape, -1e30, F32)
        l_ref[...] = jnp.zeros(l_ref.shape, F32)
        acc_ref[...] = jnp.zeros(acc_ref.shape, F32)

    qt = qt_ref[...]
    row = lax.broadcasted_iota(I32, qt.shape, 0)
    comps = (jnp.where(row < HEAD_DIM, qt, jnp.zeros_like(qt)),
             jnp.where(row >= HEAD_DIM, qt, jnp.zeros_like(qt)))
    q0 = i * tq
    k_first = j * n_chunks * tc
    ones = jnp.ones((ONES_ROWS, tc), BF16)

    def far_side(k0):
        before = k0 + tc - 1 - q0 <= -FAR
        after = k0 - (q0 + tq - 1) >= FAR
        return before, after

    def scores(m, with_band):
        jj, c = m // 2, m % 2
        s = _dot(k_ref[jj * tc:(jj + 1) * tc, :], comps[c])
        if with_band:
            blk = (k_first + jj * tc - q0) // LANES
            mid = N_BIAS_TILES // 2
            s = s + jnp.concatenate(
                [jnp.concatenate([tiles_ref[jnp.clip(blk + a - b, -mid, mid) + mid]
                                  for b in range(tq // LANES)], axis=1)
                 for a in range(tc // LANES)], axis=0)
        return s

    def run(with_band):
        state = [(m_ref[c], l_ref[c], acc_ref[c]) for c in range(2)]
        n_maps = 2 * n_chunks
        ahead = [scores(m, with_band) for m in range(min(LOOKAHEAD, n_maps))]
        for m in range(n_maps):
            jj, c = m // 2, m % 2
            s = ahead.pop(0)
            if m + LOOKAHEAD < n_maps:
                ahead.append(scores(m + LOOKAHEAD, with_band))
            rows = slice(jj * tc, (jj + 1) * tc)
            if with_band:
                const = 0.0
            else:
                before, _ = far_side(k_first + jj * tc)
                const = jnp.where(before, far_ref[h, 0], far_ref[h, 1])
            vt1 = jnp.concatenate([vt_ref[:, rows], ones], axis=0)
            m_old, l_old, acc_old = state[c]
            m_new = jnp.maximum(m_old, jnp.max(s, axis=0, keepdims=True) + const)
            p = jnp.exp2(s - (m_new - const)).astype(BF16)
            alpha = jnp.exp2(m_old - m_new)
            pv = _dot(vt1, p)
            state[c] = (m_new, alpha * l_old + pv[HEAD_BLOCK:HEAD_BLOCK + 1],
                        alpha * acc_old + pv[:HEAD_BLOCK])
        for c in range(2):
            m_ref[c], l_ref[c], acc_ref[c] = state[c]

    last_before, _ = far_side(k_first + (n_chunks - 1) * tc)
    _, first_after = far_side(k_first)
    all_far = last_before | first_after

    @pl.when(all_far)
    def _():
        run(False)

    @pl.when(jnp.logical_not(all_far))
    def _():
        run(True)

    @pl.when(j == pl.num_programs(3) - 1)
    def _():
        lam = (jnp.exp(jnp.sum(lq1_ref[...] * lk1_ref[...], keepdims=True))
               - jnp.exp(jnp.sum(lq2_ref[...] * lk2_ref[...], keepdims=True)) + lambda_init)
        a = acc_ref[0] / l_ref[0] - lam * (acc_ref[1] / l_ref[1])
        a = a * lax.rsqrt(jnp.mean(a * a, axis=0, keepdims=True) + EPS) * gs_ref[...]
        o_ref[...] = (a * (1.0 - lambda_init)).T.astype(BF16)


def _proj_kernel(o_ref, w_ref, h_ref, out_ref):
    out_ref[...] = h_ref[...] + _dot(o_ref[...], w_ref[...])


def _attn_tiles(s):
    tq = min(1024, s)
    tc = min(512, s)
    assert tq % LANES == 0 and tc % LANES == 0
    return tq, tc


def _attention_layer(h, g, w_qkv, w_o, g_q, g_k, lq1, lk1, lq2, lk2, g_sub, rel_bias, lambda_init):
    b, s, d = h.shape
    nh = d // HEAD_BLOCK
    tn = min(512, s)
    hd = np.arange(d) // HEAD_DIM
    bd = jnp.asarray((hd[:, None] == hd[None, :]) / HEAD_DIM, BF16)
    tile_gain = lambda x: jnp.tile(x, d // HEAD_DIM).reshape(1, d)
    const2 = lambda shape: pl.BlockSpec(shape, lambda bi, i: (0,) * len(shape))
    qt, k, vt = pl.pallas_call(
        functools.partial(_qkv_kernel, d=d, nh=nh),
        grid=(b, s // tn),
        in_specs=[pl.BlockSpec((None, tn, d), lambda bi, i: (bi, i, 0)),
                  const2((1, d)), const2((d, 3 * d)), const2((d, d)), const2((1, d)), const2((1, d))],
        out_specs=[pl.BlockSpec((None, nh, HEAD_BLOCK, tn), lambda bi, i: (bi, 0, 0, i)),
                   pl.BlockSpec((None, tn, d), lambda bi, i: (bi, i, 0)),
                   pl.BlockSpec((None, nh, HEAD_BLOCK, tn), lambda bi, i: (bi, 0, 0, i))],
        out_shape=[jax.ShapeDtypeStruct((b, nh, HEAD_BLOCK, s), BF16),
                   jax.ShapeDtypeStruct((b, s, d), BF16),
                   jax.ShapeDtypeStruct((b, nh, HEAD_BLOCK, s), BF16)],
        compiler_params=_cparams(("parallel", "parallel")),
    )(h, g.reshape(1, d), w_qkv.astype(BF16), bd, tile_gain(g_q), tile_gain(g_k))

    tq, tc = _attn_tiles(s)
    tko = min(4096, s)
    n_chunks = tko // tc
    tiles, far = _bias_tiles(rel_bias)
    const4 = lambda shape: pl.BlockSpec(shape, lambda bi, hi, i, j: (0,) * len(shape))
    lam_spec = const4((1, HEAD_DIM))
    o = pl.pallas_call(
        functools.partial(_attn_kernel, tq=tq, tc=tc, n_chunks=n_chunks, lambda_init=lambda_init),
        grid=(b, nh, s // tq, s // tko),
        in_specs=[pl.BlockSpec(memory_space=pltpu.SMEM),
                  pl.BlockSpec((None, None, HEAD_BLOCK, tq), lambda bi, hi, i, j: (bi, hi, 0, i)),
                  pl.BlockSpec((None, tko, HEAD_BLOCK), lambda bi, hi, i, j: (bi, j, hi)),
                  pl.BlockSpec((None, None, HEAD_BLOCK, tko), lambda bi, hi, i, j: (bi, hi, 0, j)),
                  pl.BlockSpec((None, N_BIAS_TILES, LANES, LANES), lambda bi, hi, i, j: (hi, 0, 0, 0)),
                  lam_spec, lam_spec, lam_spec, lam_spec,
                  const4((HEAD_BLOCK, 1))],
        out_specs=pl.BlockSpec((None, tq, HEAD_BLOCK), lambda bi, hi, i, j: (bi, i, hi)),
        out_shape=jax.ShapeDtypeStruct((b, s, d), BF16),
        scratch_shapes=[pltpu.VMEM((2, 1, tq), F32), pltpu.VMEM((2, 1, tq), F32),
                        pltpu.VMEM((2, HEAD_BLOCK, tq), F32)],
        compiler_params=_cparams(("parallel", "parallel", "parallel", "arbitrary")),
    )(far, qt, k, vt, tiles, lq1.reshape(1, -1), lk1.reshape(1, -1), lq2.reshape(1, -1), lk2.reshape(1, -1),
      g_sub.reshape(HEAD_BLOCK, 1))

    n = b * s
    tp = min(1024, n)
    out = pl.pallas_call(
        _proj_kernel,
        grid=(n // tp,),
        in_specs=[pl.BlockSpec((tp, d), lambda i: (i, 0)),
                  pl.BlockSpec((d, d), lambda i: (0, 0)),
                  pl.BlockSpec((tp, d), lambda i: (i, 0))],
        out_specs=pl.BlockSpec((tp, d), lambda i: (i, 0)),
        out_shape=jax.ShapeDtypeStruct((n, d), F32),
        compiler_params=_cparams(("parallel",)),
    )(o.reshape(n, d), w_o.astype(BF16), h.reshape(n, d))
    return out.reshape(b, s, d)


def _trunk(x, rel_bias, wa, wb, g_mix, g_ffn, w_qkv, w_attn_out, g_q, g_k, lq1, lk1, lq2, lk2, g_sub,
           w_router, w_gate, w_up, w_down):
    b, s, d = x.shape
    moe = lambda h, i: _moe_layer(h.reshape(b * s, d), g_ffn[i], w_router[i], w_gate, w_up, w_down,
                                  i).reshape(b, s, d)
    h = _fourier_layer(x, g_mix[0], wa, wb)
    h = moe(h, 0)
    lambda_init = 0.8 - 0.6 * math.exp(-0.3 * 1)
    h = _attention_layer(h, g_mix[1], w_qkv[0], w_attn_out[0], g_q[0], g_k[0], lq1[0], lk1[0], lq2[0],
                         lk2[0], g_sub[0], rel_bias, lambda_init)
    return moe(h, 1)


def kernel(x_prompt, x_sample, rel_bias, g_mix, g_ffn, w_fourier_out, w_qkv, w_attn_out, g_q, g_k,
           lambda_q1, lambda_k1, lambda_q2, lambda_k2, g_sub, w_router, w_gate, w_up, w_down):
    assert g_mix.shape[0] == 2, "one Fourier layer followed by one attention layer"
    wa, wb = _fold_w(w_fourier_out[0])
    run = lambda x: _trunk(x, rel_bias, wa, wb, g_mix, g_ffn, w_qkv, w_attn_out, g_q, g_k, lambda_q1, lambda_k1,
                           lambda_q2, lambda_k2, g_sub, w_router, w_gate, w_up, w_down)
    return (run(x_prompt), run(x_sample))
```

```python
import functools
import math

import numpy as np
import jax
import jax.numpy as jnp
from jax import lax
from jax.experimental import pallas as pl
from jax.experimental.pallas import tpu as pltpu

EPS = 1e-6
F32 = jnp.float32
BF16 = jnp.bfloat16
I32 = jnp.int32
HIGHEST = lax.Precision.HIGHEST
LOG2E = math.log2(math.e)

LANES = 128
SUBLANES = 8
FOURIER_GROUP_DIM = 128
HEAD_BLOCK = 128
HEAD_DIM = 64
ONES_ROWS = 16
N_EXPERTS = 16
CAPACITY_FACTOR = 2
FFN_ROW_CHUNK = 512
GATHER_GROUP = 1
COMBINE_GROUP = 8
STEP_NEXT, STEP_FIRST, STEP_IDLE = 0, 1, 2
N_BUCKETS = 32
MAX_DISTANCE = 128
VMEM_LIMIT = 56 * 1024 * 1024


def _cparams(sem):
    return pltpu.CompilerParams(dimension_semantics=sem, vmem_limit_bytes=VMEM_LIMIT)


def _rms(x, g):
    return x * lax.rsqrt(jnp.mean(x * x, axis=-1, keepdims=True) + EPS) * g


def _dot(a, b):
    return jnp.dot(a, b, preferred_element_type=F32)


def _cos_sin(n):
    k = np.arange(n)
    ang = 2.0 * np.pi * ((k[:, None] * k[None, :]) % n) / n
    return np.cos(ang), np.sin(ang)


def _fold_w_kernel(cc_ref, sc_ref, w_ref, wa_ref, wb_ref):
    w = w_ref[...]
    wa_ref[...] = jnp.dot(cc_ref[...], w, precision=HIGHEST, preferred_element_type=F32).astype(BF16)
    wb_ref[...] = jnp.dot(sc_ref[...], w, precision=HIGHEST, preferred_element_type=F32).astype(BF16)


def _fold_w(w):
    d = w.shape[0]
    gd = FOURIER_GROUP_DIM
    c, s = _cos_sin(gd)
    cc = jnp.asarray(c / math.sqrt(gd), F32)
    sc = jnp.asarray(s / math.sqrt(gd), F32)
    const = pl.BlockSpec((gd, gd), lambda g: (0, 0))
    blk = pl.BlockSpec((gd, d), lambda g: (g, 0))
    return pl.pallas_call(
        _fold_w_kernel,
        grid=(d // gd,),
        in_specs=[const, const, blk],
        out_specs=[blk, blk],
        out_shape=[jax.ShapeDtypeStruct((d, d), BF16)] * 2,
        compiler_params=_cparams(("arbitrary",)),
    )(cc, sc, w)


def _fourier_a_kernel(x_ref, g_ref, f1_ref, twr_ref, twi_ref, yr_ref, yi_ref, *, nb, n1, d):
    j = pl.program_id(1)
    lane = lax.broadcasted_iota(I32, twr_ref.shape, 1)
    for c in range(nb):
        xn = _rms(x_ref[:, c, :], g_ref[...]).astype(BF16)
        y = _dot(f1_ref[...], xn)
        yr, yi = y[:n1], y[n1:]
        sel = lane == (j * nb + c)
        tr = jnp.sum(jnp.where(sel, twr_ref[...], 0.0), axis=1, keepdims=True)
        ti = jnp.sum(jnp.where(sel, twi_ref[...], 0.0), axis=1, keepdims=True)
        cols = slice(c * d, (c + 1) * d)
        yr_ref[:, cols] = (yr * tr - yi * ti).astype(BF16)
        yi_ref[:, cols] = (yr * ti + yi * tr).astype(BF16)


def _fourier_b_kernel(yr_ref, yi_ref, f2_ref, wa_ref, wb_ref, x_ref, o_ref, *, kb, n2):
    ars, ais = [], []
    for c in range(kb):
        rows = slice(c * n2, (c + 1) * n2)
        ys = jnp.concatenate([yr_ref[rows, :], yi_ref[rows, :]], axis=0)
        a = _dot(f2_ref[...], ys)
        ars.append(a[:n2].astype(BF16))
        ais.append(a[n2:].astype(BF16))
    ar = jnp.concatenate(ars, axis=0)
    ai = jnp.concatenate(ais, axis=0)
    out = _dot(ar, wa_ref[...]) + _dot(ai, wb_ref[...])
    for c in range(kb):
        o_ref[:, c, :] = x_ref[:, c, :] + out[c * n2:(c + 1) * n2]


def _split_len(s):
    n1 = 1 << (int(math.log2(s)) // 2)
    assert s % n1 == 0
    return n1, s // n1


def _fourier_layer(x, g, wa, wb):
    b, s, d = x.shape
    n1, n2 = _split_len(s)
    c1, s1 = _cos_sin(n1)
    c2, s2 = _cos_sin(n2)
    f1 = jnp.asarray(np.concatenate([c1, -s1], axis=0) / math.sqrt(n1), BF16)
    f2 = jnp.asarray(np.block([[c2, s2], [-s2, c2]]) / math.sqrt(n2), BF16)
    ang = 2.0 * np.pi * ((np.arange(n1)[:, None] * np.arange(n2)[None, :]) % s) / s
    twr = jnp.asarray(np.cos(ang), F32)
    twi = jnp.asarray(-np.sin(ang), F32)

    nb = SUBLANES
    blk_a = pl.BlockSpec((None, n1, nb, d), lambda bi, j: (bi, 0, j, 0))
    blk_ya = pl.BlockSpec((None, n1, nb * d), lambda bi, j: (bi, 0, j))
    full = lambda shape: pl.BlockSpec(shape, lambda bi, j: (0,) * len(shape))
    yr, yi = pl.pallas_call(
        functools.partial(_fourier_a_kernel, nb=nb, n1=n1, d=d),
        grid=(b, n2 // nb),
        in_specs=[blk_a, full((1, d)), full((2 * n1, n1)), full((n1, n2)), full((n1, n2))],
        out_specs=[blk_ya, blk_ya],
        out_shape=[jax.ShapeDtypeStruct((b, n1, n2 * d), BF16)] * 2,
        compiler_params=_cparams(("parallel", "parallel")),
    )(x.reshape(b, n1, n2, d), g.reshape(1, d), f1, twr, twi)

    kb = SUBLANES
    blk_y = pl.BlockSpec((None, kb * n2, d), lambda bi, i: (bi, i, 0))
    blk_x = pl.BlockSpec((None, n2, kb, d), lambda bi, i: (bi, 0, i, 0))
    out = pl.pallas_call(
        functools.partial(_fourier_b_kernel, kb=kb, n2=n2),
        grid=(b, n1 // kb),
        in_specs=[blk_y, blk_y, full((2 * n2, 2 * n2)), full((d, d)), full((d, d)), blk_x],
        out_specs=blk_x,
        out_shape=jax.ShapeDtypeStruct((b, n2, n1, d), F32),
        compiler_params=_cparams(("parallel", "parallel")),
    )(yr.reshape(b, s, d), yi.reshape(b, s, d), f2, wa, wb, x.reshape(b, n2, n1, d))
    return out.reshape(b, s, d)


def _router_kernel(h_ref, g_ref, wrt_ref, hn_ref, aff_ref):
    hn = _rms(h_ref[...], g_ref[...])
    hn_ref[...] = hn.astype(BF16)
    logits = lax.dot_general(wrt_ref[...], hn, (((1,), (1,)), ((), ())),
                             precision=HIGHEST, preferred_element_type=F32)
    ex = jnp.exp(logits - jnp.max(logits, axis=0, keepdims=True))
    aff_ref[...] = ex / jnp.sum(ex, axis=0, keepdims=True)


def _select_kernel(a_ref, pos_ref, cum_ref, *, e, nb, k):
    keys = lax.bitcast_convert_type(a_ref[...], I32).reshape(e, nb, LANES)

    def count(mask):
        part = jnp.sum(mask.astype(F32), axis=2, keepdims=True)
        return jnp.sum(part, axis=1, keepdims=True)

    def value_step(i, t):
        cand = t | (jnp.int32(1) << (30 - i))
        return jnp.where(count(keys >= cand) >= k, cand, t)

    thr = lax.fori_loop(0, 31, value_step, jnp.zeros((e, 1, 1), I32))
    gt = keys > thr
    eq = keys == thr
    need = k - count(gt)
    tok = (lax.broadcasted_iota(I32, (e, nb, LANES), 1) * LANES
           + lax.broadcasted_iota(I32, (e, nb, LANES), 2))
    nbits = (nb * LANES - 1).bit_length()

    def index_step(i, v):
        cand = v | (jnp.int32(1) << (nbits - 1 - i))
        return jnp.where(count(eq & (tok < cand)) < need, cand, v)

    last = lax.fori_loop(0, nbits, index_step, jnp.zeros((e, 1, 1), I32))
    sel = (gt | (eq & (tok <= last))).reshape(e * nb, LANES)
    self32 = sel.astype(F32)

    r = lax.broadcasted_iota(I32, (LANES, LANES), 0)
    c = lax.broadcasted_iota(I32, (LANES, LANES), 1)
    inc = _dot(self32.astype(BF16), (r <= c).astype(BF16))
    tot = jnp.broadcast_to(inc[:, LANES - 1:LANES], (e * nb, LANES)).astype(BF16)
    rb = lax.broadcasted_iota(I32, (nb, nb), 0)
    cb = lax.broadcasted_iota(I32, (nb, nb), 1)
    lower = (cb < rb).astype(BF16)
    before = jnp.concatenate([_dot(lower, tot[x * nb:(x + 1) * nb]) for x in range(e)], axis=0)
    pos = inc - self32 + before
    pos_ref[...] = jnp.where(sel, pos, -1.0).astype(I32)
    cum_ref[...] = before.astype(I32)


def _one_hot_mask(pos_ref, row_tile, rt):
    tt = pos_ref.shape[-1]
    rows = lax.broadcasted_iota(I32, (rt, tt), 0) + row_tile * rt
    return pos_ref[...] == rows


def _gather_kernel(gt, gv, gk, gf, hn_ref, *refs, rt, n_r, group):
    pos_refs, aff_refs, xe_ref, gc_ref = refs[:group], refs[group:2 * group], refs[2 * group], refs[2 * group + 1]
    s = pl.program_id(0)
    tt = pos_refs[0].shape[-1]

    @pl.when(gf[s] != STEP_IDLE)
    def _():
        rows, gate = 0.0, 0.0
        for k in range(group):
            mask = _one_hot_mask(pos_refs[k], gk[s] % n_r, rt) & (gv[s * group + k] == 1)
            tokens = hn_ref[pl.ds(pl.multiple_of(gt[s * group + k] * tt, tt), tt), :]
            rows = rows + _dot(jnp.where(mask, 1.0, 0.0).astype(BF16), tokens)
            gate = gate + jnp.sum(jnp.where(mask, aff_refs[k][...], 0.0), axis=1, keepdims=True)

        @pl.when(gf[s] == STEP_FIRST)
        def _():
            xe_ref[...] = rows.astype(BF16)
            gc_ref[...] = gate

        @pl.when(gf[s] == STEP_NEXT)
        def _():
            xe_ref[...] = (xe_ref[...].astype(F32) + rows).astype(BF16)
            gc_ref[...] += gate


def _ffn_kernel(xe_ref, wg_ref, wu_ref, wd_ref, gc_ref, ye_ref, *acc_refs):
    j = pl.program_id(2)
    n_chunks = len(acc_refs)
    rc = acc_refs[0].shape[0]

    @pl.when(j == 0)
    def _():
        for acc_ref in acc_refs:
            acc_ref[...] = jnp.zeros(acc_ref.shape, F32)

    wg = wg_ref[...].astype(BF16)
    wu = wu_ref[...].astype(BF16)
    wd = wd_ref[...].astype(BF16)

    def gate_up(c):
        x = xe_ref[c * rc:(c + 1) * rc, :]
        return _dot(x, wg), _dot(x, wu)

    ahead = gate_up(0)
    for c in range(n_chunks):
        g, u = ahead
        if c + 1 < n_chunks:
            ahead = gate_up(c + 1)
        hid = (g / (1.0 + jnp.exp(-g))) * u
        acc_refs[c][...] += _dot(hid.astype(BF16), wd)

    @pl.when(j == pl.num_programs(2) - 1)
    def _():
        for c, acc_ref in enumerate(acc_refs):
            rows = slice(c * rc, (c + 1) * rc)
            ye_ref[rows, :] = (acc_ref[...] * gc_ref[rows, :]).astype(BF16)


def _combine_kernel(gi, gv, gk, gf, h_ref, *refs, rt, n_r, group):
    pos_refs, ye_refs, o_ref = refs[:group], refs[group:2 * group], refs[2 * group]
    s = pl.program_id(0)

    @pl.when(gf[s] != STEP_IDLE)
    def _():
        masks = [_one_hot_mask(pos_refs[k], gi[s * group + k] % n_r, rt) & (gv[s * group + k] == 1)
                 for k in range(group)]
        onehot = jnp.where(jnp.concatenate(masks, axis=0), 1.0, 0.0).astype(BF16)
        ye = jnp.concatenate([ye_refs[k][...] for k in range(group)], axis=0)
        back = lax.dot_general(onehot, ye, (((0,), (0,)), ((), ())), preferred_element_type=F32)

        @pl.when(gf[s] == STEP_FIRST)
        def _():
            o_ref[...] = h_ref[...] + back

        @pl.when(gf[s] == STEP_NEXT)
        def _():
            o_ref[...] += back


def _schedule_kernel(lo_ref, *out_refs, n_e, n_r, n_t, rt, cap, groups):
    def emitter(item_ref, valid_ref, key_ref, flag_ref, group):
        for i_ref, fill in ((item_ref, 0), (valid_ref, 0), (key_ref, 0), (flag_ref, STEP_IDLE)):
            def clear(i, c, i_ref=i_ref, fill=fill):
                i_ref[i] = fill
                return c
            lax.fori_loop(0, i_ref.shape[0], clear, 0)

        def emit(state, key, item):
            step, cnt, cur = state
            fresh = key != cur
            step = step + (fresh & (cnt > 0)).astype(I32)
            cnt = jnp.where(fresh, 0, cnt)
            item_ref[step * group + cnt] = item
            valid_ref[step * group + cnt] = 1

            @pl.when(cnt == 0)
            def _():
                key_ref[step] = key
                flag_ref[step] = jnp.where(fresh, STEP_FIRST, STEP_NEXT)

            full = cnt + 1 == group
            return step + full.astype(I32), jnp.where(full, 0, cnt + 1), key

        def finish(state):
            step, cnt, cur = state
            last = step + (cnt > 0).astype(I32)

            def pad(i, c):
                key_ref[i] = cur
                return c
            lax.fori_loop(last, key_ref.shape[0], pad, 0)

        return emit, finish

    def rows_of(e, t):
        lo = lo_ref[e, t]
        hi = jnp.where(t + 1 < n_t, lo_ref[e, jnp.minimum(t + 1, n_t - 1)], cap)
        return lo, hi

    start = (jnp.int32(0), jnp.int32(0), jnp.int32(-1))
    emit_g, finish_g = emitter(*out_refs[:4], groups[0])

    def gather_tile(i, state):
        e, t = i // n_t, i % n_t
        lo, hi = rows_of(e, t)
        return lax.fori_loop(lo // rt, jnp.where(hi > lo, (hi - 1) // rt + 1, lo // rt),
                             lambda r, st: emit_g(st, e * n_r + r, t), state)

    finish_g(lax.fori_loop(0, n_e * n_t, gather_tile, start))

    emit_c, finish_c = emitter(*out_refs[4:], groups[1])

    def combine_tile(i, state):
        t, e = i // n_e, i % n_e
        lo, hi = rows_of(e, t)
        first = jnp.minimum(lo // rt, n_r - 1)
        stop = jnp.where(hi > lo, (hi - 1) // rt + 1, jnp.where(e == 0, first + 1, first))
        return lax.fori_loop(first, stop, lambda r, st: emit_c(st, t, e * n_r + r), state)

    finish_c(lax.fori_loop(0, n_t * n_e, combine_tile, start))


def _schedules(lo, cap, rt, groups):
    n_e, n_t = lo.shape
    n_r = cap // rt
    n_true = n_e * (n_r + n_t - 1)
    steps = ((n_true + (groups[0] - 1) * n_e * n_r) // groups[0],
             (n_true + n_t + (groups[1] - 1) * n_t) // groups[1])
    sizes = [steps[0] * groups[0]] * 2 + [steps[0]] * 2 + [steps[1] * groups[1]] * 2 + [steps[1]] * 2
    smem = pl.BlockSpec(memory_space=pltpu.SMEM)
    out = pl.pallas_call(
        functools.partial(_schedule_kernel, n_e=n_e, n_r=n_r, n_t=n_t, rt=rt, cap=cap, groups=groups),
        in_specs=[smem],
        out_specs=[smem] * 8,
        out_shape=[jax.ShapeDtypeStruct((k,), I32) for k in sizes],
    )(lo)
    return out[:4], out[4:]


def _moe_layer(h, g, w_router, w_gate, w_up, w_down, layer):
    n, d = h.shape
    e = N_EXPERTS
    cap = CAPACITY_FACTOR * n // e
    f = w_gate.shape[-1]
    nb = n // LANES
    tn = min(1024, n)

    hn, aff = pl.pallas_call(
        _router_kernel,
        grid=(n // tn,),
        in_specs=[pl.BlockSpec((tn, d), lambda i: (i, 0)),
                  pl.BlockSpec((1, d), lambda i: (0, 0)),
                  pl.BlockSpec((e, d), lambda i: (0, 0))],
        out_specs=[pl.BlockSpec((tn, d), lambda i: (i, 0)),
                   pl.BlockSpec((e, tn), lambda i: (0, i))],
        out_shape=[jax.ShapeDtypeStruct((n, d), BF16), jax.ShapeDtypeStruct((e, n), F32)],
        compiler_params=_cparams(("parallel",)),
    )(h, g.reshape(1, d), w_router.T)

    whole = pl.BlockSpec((e * nb, LANES), lambda i: (0, 0))
    pos, cum = pl.pallas_call(
        functools.partial(_select_kernel, e=e, nb=nb, k=cap),
        grid=(1,),
        in_specs=[whole],
        out_specs=[whole, whole],
        out_shape=[jax.ShapeDtypeStruct((e * nb, LANES), I32)] * 2,
        compiler_params=_cparams(("arbitrary",)),
    )(aff.reshape(e * nb, LANES))

    rt = min(256, cap)
    tt = min(1024, n)
    n_r, n_t = cap // rt, n // tt
    lo = cum[:, 0].reshape(e, nb)[:, ::tt // LANES]
    (at, av, ak, af), (ci, cv, ck, cf) = _schedules(lo, cap, rt, (GATHER_GROUP, COMBINE_GROUP))

    pos3 = pos.reshape(e * n_t, 1, tt)
    aff3 = aff.reshape(e * n_t, 1, tt)
    grp = GATHER_GROUP
    row_spec = lambda k: pl.BlockSpec(
        (None, 1, tt), lambda s, at, av, ak, af: (ak[s] // n_r * n_t + at[s * grp + k], 0, 0))
    xe, gc = pl.pallas_call(
        functools.partial(_gather_kernel, rt=rt, n_r=n_r, group=grp),
        grid_spec=pltpu.PrefetchScalarGridSpec(
            num_scalar_prefetch=4,
            grid=(ak.shape[0],),
            in_specs=[pl.BlockSpec((n, d), lambda s, at, av, ak, af: (0, 0), pipeline_mode=pl.Buffered(1))]
                     + [row_spec(k) for k in range(grp)] * 2,
            out_specs=[pl.BlockSpec((None, rt, d), lambda s, at, av, ak, af: (ak[s] // n_r, ak[s] % n_r, 0)),
                       pl.BlockSpec((None, rt, 1), lambda s, at, av, ak, af: (ak[s] // n_r, ak[s] % n_r, 0))],
        ),
        out_shape=[jax.ShapeDtypeStruct((e, cap, d), BF16), jax.ShapeDtypeStruct((e, cap, 1), F32)],
        compiler_params=_cparams(("arbitrary",)),
    )(at, av, ak, af, hn, *([pos3] * grp), *([aff3] * grp))

    tm = min(2048, cap)
    tf = 256
    ye = pl.pallas_call(
        _ffn_kernel,
        grid=(e, cap // tm, f // tf),
        in_specs=[pl.BlockSpec((None, tm, d), lambda x, r, j: (x, r, 0)),
                  pl.BlockSpec((None, None, d, tf), lambda x, r, j: (layer, x, 0, j)),
                  pl.BlockSpec((None, None, d, tf), lambda x, r, j: (layer, x, 0, j)),
                  pl.BlockSpec((None, None, tf, d), lambda x, r, j: (layer, x, j, 0)),
                  pl.BlockSpec((None, tm, 1), lambda x, r, j: (x, r, 0))],
        out_specs=pl.BlockSpec((None, tm, d), lambda x, r, j: (x, r, 0)),
        out_shape=jax.ShapeDtypeStruct((e, cap, d), BF16),
        scratch_shapes=[pltpu.VMEM((min(FFN_ROW_CHUNK, tm), d), F32)] * (tm // min(FFN_ROW_CHUNK, tm)),
        compiler_params=_cparams(("parallel", "parallel", "arbitrary")),
    )(xe, w_gate, w_up, w_down, gc)

    grp = COMBINE_GROUP
    pos_spec = lambda k: pl.BlockSpec(
        (None, 1, tt), lambda s, ci, cv, ck, cf: (ci[s * grp + k] // n_r * n_t + ck[s], 0, 0))
    ye_spec = lambda k: pl.BlockSpec(
        (None, rt, d), lambda s, ci, cv, ck, cf: (ci[s * grp + k] // n_r, ci[s * grp + k] % n_r, 0))
    tile_spec = pl.BlockSpec((tt, d), lambda s, ci, cv, ck, cf: (ck[s], 0))
    return pl.pallas_call(
        functools.partial(_combine_kernel, rt=rt, n_r=n_r, group=grp),
        grid_spec=pltpu.PrefetchScalarGridSpec(
            num_scalar_prefetch=4,
            grid=(ck.shape[0],),
            in_specs=[tile_spec] + [pos_spec(k) for k in range(grp)] + [ye_spec(k) for k in range(grp)],
            out_specs=tile_spec,
        ),
        out_shape=jax.ShapeDtypeStruct((n, d), F32),
        compiler_params=_cparams(("arbitrary",)),
    )(ci, cv, ck, cf, h, *([pos3] * grp), *([ye] * grp))


def _t5_bucket_np(rel):
    nb = N_BUCKETS // 2
    max_exact = nb // 2
    ret = np.where(rel > 0, nb, 0)
    n = np.abs(rel)
    nf = np.maximum(n, 1).astype(np.float64)
    large = max_exact + (np.log(nf / max_exact) / math.log(MAX_DISTANCE / max_exact)
                         * (nb - max_exact)).astype(np.int64)
    large = np.minimum(large, nb - 1)
    return ret + np.where(n < max_exact, n, large)


def _far_distance():
    n = np.arange(1, 4 * MAX_DISTANCE)
    last = _t5_bucket_np(-n)
    return int(n[np.nonzero(last != last[-1])[0][-1]] + 1)


FAR = _far_distance()


MAX_RISE = 64.0
LOOKAHEAD = 2
N_BIAS_TILES = 5
assert FAR <= LANES + 1


def _bias_tiles_kernel(near_ref, far_ref, t_ref):
    h = pl.program_id(0)
    d = lax.broadcasted_iota(I32, (LANES, LANES), 0) - lax.broadcasted_iota(I32, (LANES, LANES), 1)
    for idx in range(N_BIAS_TILES):
        rel = d + LANES * (idx - 2)
        tile = jnp.where(rel < 0, far_ref[h, 0], far_ref[h, 1])
        if abs(idx - 2) <= 1:
            tile = lax.fori_loop(
                0, 2 * FAR - 1, lambda r, t: jnp.where(rel == r - (FAR - 1), near_ref[h, r], t), tile)
        t_ref[idx] = tile


def _bias_tiles(rel_bias):
    nh = rel_bias.shape[1]
    near = (rel_bias[_t5_bucket_np(np.arange(-(FAR - 1), FAR))] * LOG2E).T
    far = jnp.stack([rel_bias[N_BUCKETS // 2 - 1], rel_bias[N_BUCKETS - 1]], axis=1) * LOG2E
    smem = pl.BlockSpec(memory_space=pltpu.SMEM)
    tiles = pl.pallas_call(
        _bias_tiles_kernel,
        grid=(nh,),
        in_specs=[smem, smem],
        out_specs=pl.BlockSpec((None, N_BIAS_TILES, LANES, LANES), lambda h: (h, 0, 0, 0)),
        out_shape=jax.ShapeDtypeStruct((nh, N_BIAS_TILES, LANES, LANES), F32),
        compiler_params=_cparams(("arbitrary",)),
    )(near, far)
    return tiles, far


def _qkv_kernel(h_ref, g_ref, w_ref, bd_ref, gq_ref, gk_ref, qt_ref, k_ref, vt_ref, *, d, nh):
    hn = _rms(h_ref[...], g_ref[...]).astype(BF16)
    qkv = _dot(hn, w_ref[...])
    q, k, v = qkv[:, :d], qkv[:, d:2 * d], qkv[:, 2 * d:]

    def head_norm(x, gain):
        ms = _dot((x * x).astype(BF16), bd_ref[...])
        return x * lax.rsqrt(ms + EPS) * gain

    qn = head_norm(q, gq_ref[...]) * (HEAD_DIM ** -0.5 * LOG2E)
    k_ref[...] = head_norm(k, gk_ref[...]).astype(BF16)
    for h in range(nh):
        cols = slice(h * HEAD_BLOCK, (h + 1) * HEAD_BLOCK)
        qt_ref[h] = qn[:, cols].T.astype(BF16)
        vt_ref[h] = v[:, cols].T.astype(BF16)


def _attn_kernel(far_ref, qt_ref, k_ref, vt_ref, tiles_ref, lq1_ref, lk1_ref, lq2_ref, lk2_ref, gs_ref, o_ref,
                 m_ref, l_ref, acc_ref, redo_ref, *, tq, tc, n_chunks, lambda_init):
    h = pl.program_id(1)
    i = pl.program_id(2)
    j = pl.program_id(3)

    qt = qt_ref[...]
    row = lax.broadcasted_iota(I32, qt.shape, 0)
    comps = (jnp.where(row < HEAD_DIM, qt, jnp.zeros_like(qt)),
             jnp.where(row >= HEAD_DIM, qt, jnp.zeros_like(qt)))
    q0 = i * tq
    k_first = j * n_chunks * tc
    ones = jnp.ones((ONES_ROWS, tc), BF16)

    def far_side(k0):
        before = k0 + tc - 1 - q0 <= -FAR
        after = k0 - (q0 + tq - 1) >= FAR
        return before, after

    def bias_rows(k0, n_rows):
        blk = (k0 - q0) // LANES
        mid = N_BIAS_TILES // 2
        take = min(n_rows, LANES)
        return jnp.concatenate(
            [jnp.concatenate([tiles_ref[jnp.clip(blk + a - b, -mid, mid) + mid][:take]
                              for b in range(tq // LANES)], axis=1)
             for a in range(max(n_rows // LANES, 1))], axis=0)

    @pl.when(j == 0)
    def _():
        head = bias_rows(0, ONES_ROWS)
        for c in range(2):
            m_ref[c] = jnp.max(_dot(k_ref[:ONES_ROWS, :], comps[c]) + head, axis=0, keepdims=True)
        l_ref[...] = jnp.zeros(l_ref.shape, F32)
        acc_ref[...] = jnp.zeros(acc_ref.shape, F32)

    def scores(m, with_band):
        jj, c = m // 2, m % 2
        s = _dot(k_ref[jj * tc:(jj + 1) * tc, :], comps[c])
        return s + bias_rows(k_first + jj * tc, tc) if with_band else s

    def run(with_band, exact):
        state = [(m_ref[c], l_ref[c], acc_ref[c]) for c in range(2)]
        rise = jnp.full((1, tq), -1e30, F32)
        n_maps = 2 * n_chunks
        ahead = [scores(m, with_band) for m in range(min(LOOKAHEAD, n_maps))]
        for m in range(n_maps):
            jj, c = m // 2, m % 2
            s = ahead.pop(0)
            if m + LOOKAHEAD < n_maps:
                ahead.append(scores(m + LOOKAHEAD, with_band))
            rows = slice(jj * tc, (jj + 1) * tc)
            if with_band:
                const = 0.0
            else:
                before, _ = far_side(k_first + jj * tc)
                const = jnp.where(before, far_ref[h, 0], far_ref[h, 1])
            vt1 = jnp.concatenate([vt_ref[:, rows], ones], axis=0)
            m_old, l_old, acc_old = state[c]
            top = jnp.max(s, axis=0, keepdims=True) + const
            m_new = jnp.maximum(m_old, top)
            alpha = jnp.exp2(m_old - m_new)
            if exact:
                pv = _dot(vt1, jnp.exp2(s - (m_new - const)).astype(BF16))
                state[c] = (m_new, alpha * l_old + pv[HEAD_BLOCK:HEAD_BLOCK + 1],
                            alpha * acc_old + pv[:HEAD_BLOCK])
            else:
                pv = _dot(vt1, jnp.exp2(s - (m_old - const)).astype(BF16))
                rise = jnp.maximum(rise, top - m_old)
                state[c] = (m_new, alpha * (l_old + pv[HEAD_BLOCK:HEAD_BLOCK + 1]),
                            alpha * (acc_old + pv[:HEAD_BLOCK]))
        return state, rise

    def commit(state):
        for c in range(2):
            m_ref[c], l_ref[c], acc_ref[c] = state[c]

    def streamed(with_band):
        state, rise = run(with_band, exact=False)
        safe = jnp.max(rise) <= MAX_RISE
        redo_ref[0] = jnp.where(safe, 0, 1)

        @pl.when(safe)
        def _():
            commit(state)

    last_before, _ = far_side(k_first + (n_chunks - 1) * tc)
    _, first_after = far_side(k_first)
    all_far = last_before | first_after

    @pl.when(all_far)
    def _():
        streamed(False)

    @pl.when(jnp.logical_not(all_far))
    def _():
        streamed(True)

    @pl.when(redo_ref[0] == 1)
    def _():
        commit(run(True, exact=True)[0])

    @pl.when(j == pl.num_programs(3) - 1)
    def _():
        lam = (jnp.exp(jnp.sum(lq1_ref[...] * lk1_ref[...], keepdims=True))
               - jnp.exp(jnp.sum(lq2_ref[...] * lk2_ref[...], keepdims=True)) + lambda_init)
        a = acc_ref[0] / l_ref[0] - lam * (acc_ref[1] / l_ref[1])
        a = a * lax.rsqrt(jnp.mean(a * a, axis=0, keepdims=True) + EPS) * gs_ref[...]
        o_ref[...] = (a * (1.0 - lambda_init)).T.astype(BF16)


def _proj_kernel(o_ref, w_ref, h_ref, out_ref):
    out_ref[...] = h_ref[...] + _dot(o_ref[...], w_ref[...])


def _attn_tiles(s):
    tq = min(1024, s)
    tc = min(512, s)
    assert tq % LANES == 0 and tc % LANES == 0
    return tq, tc


def _attention_layer(h, g, w_qkv, w_o, g_q, g_k, lq1, lk1, lq2, lk2, g_sub, rel_bias, lambda_init):
    b, s, d = h.shape
    nh = d // HEAD_BLOCK
    tn = min(512, s)
    hd = np.arange(d) // HEAD_DIM
    bd = jnp.asarray((hd[:, None] == hd[None, :]) / HEAD_DIM, BF16)
    tile_gain = lambda x: jnp.tile(x, d // HEAD_DIM).reshape(1, d)
    const2 = lambda shape: pl.BlockSpec(shape, lambda bi, i: (0,) * len(shape))
    qt, k, vt = pl.pallas_call(
        functools.partial(_qkv_kernel, d=d, nh=nh),
        grid=(b, s // tn),
        in_specs=[pl.BlockSpec((None, tn, d), lambda bi, i: (bi, i, 0)),
                  const2((1, d)), const2((d, 3 * d)), const2((d, d)), const2((1, d)), const2((1, d))],
        out_specs=[pl.BlockSpec((None, nh, HEAD_BLOCK, tn), lambda bi, i: (bi, 0, 0, i)),
                   pl.BlockSpec((None, tn, d), lambda bi, i: (bi, i, 0)),
                   pl.BlockSpec((None, nh, HEAD_BLOCK, tn), lambda bi, i: (bi, 0, 0, i))],
        out_shape=[jax.ShapeDtypeStruct((b, nh, HEAD_BLOCK, s), BF16),
                   jax.ShapeDtypeStruct((b, s, d), BF16),
                   jax.ShapeDtypeStruct((b, nh, HEAD_BLOCK, s), BF16)],
        compiler_params=_cparams(("parallel", "parallel")),
    )(h, g.reshape(1, d), w_qkv.astype(BF16), bd, tile_gain(g_q), tile_gain(g_k))

    tq, tc = _attn_tiles(s)
    tko = min(4096, s)
    n_chunks = tko // tc
    tiles, far = _bias_tiles(rel_bias)
    const4 = lambda shape: pl.BlockSpec(shape, lambda bi, hi, i, j: (0,) * len(shape))
    lam_spec = const4((1, HEAD_DIM))
    o = pl.pallas_call(
        functools.partial(_attn_kernel, tq=tq, tc=tc, n_chunks=n_chunks, lambda_init=lambda_init),
        grid=(b, nh, s // tq, s // tko),
        in_specs=[pl.BlockSpec(memory_space=pltpu.SMEM),
                  pl.BlockSpec((None, None, HEAD_BLOCK, tq), lambda bi, hi, i, j: (bi, hi, 0, i)),
                  pl.BlockSpec((None, tko, HEAD_BLOCK), lambda bi, hi, i, j: (bi, j, hi)),
                  pl.BlockSpec((None, None, HEAD_BLOCK, tko), lambda bi, hi, i, j: (bi, hi, 0, j)),
                  pl.BlockSpec((None, N_BIAS_TILES, LANES, LANES), lambda bi, hi, i, j: (hi, 0, 0, 0)),
                  lam_spec, lam_spec, lam_spec, lam_spec,
                  const4((HEAD_BLOCK, 1))],
        out_specs=pl.BlockSpec((None, tq, HEAD_BLOCK), lambda bi, hi, i, j: (bi, i, hi)),
        out_shape=jax.ShapeDtypeStruct((b, s, d), BF16),
        scratch_shapes=[pltpu.VMEM((2, 1, tq), F32), pltpu.VMEM((2, 1, tq), F32),
                        pltpu.VMEM((2, HEAD_BLOCK, tq), F32), pltpu.SMEM((1,), I32)],
        compiler_params=_cparams(("parallel", "parallel", "parallel", "arbitrary")),
    )(far, qt, k, vt, tiles, lq1.reshape(1, -1), lk1.reshape(1, -1), lq2.reshape(1, -1), lk2.reshape(1, -1),
      g_sub.reshape(HEAD_BLOCK, 1))

    n = b * s
    tp = min(1024, n)
    out = pl.pallas_call(
        _proj_kernel,
        grid=(n // tp,),
        in_specs=[pl.BlockSpec((tp, d), lambda i: (i, 0)),
                  pl.BlockSpec((d, d), lambda i: (0, 0)),
                  pl.BlockSpec((tp, d), lambda i: (i, 0))],
        out_specs=pl.BlockSpec((tp, d), lambda i: (i, 0)),
        out_shape=jax.ShapeDtypeStruct((n, d), F32),
        compiler_params=_cparams(("parallel",)),
    )(o.reshape(n, d), w_o.astype(BF16), h.reshape(n, d))
    return out.reshape(b, s, d)


def _trunk(x, rel_bias, wa, wb, g_mix, g_ffn, w_qkv, w_attn_out, g_q, g_k, lq1, lk1, lq2, lk2, g_sub,
           w_router, w_gate, w_up, w_down):
    b, s, d = x.shape
    moe = lambda h, i: _moe_layer(h.reshape(b * s, d), g_ffn[i], w_router[i], w_gate, w_up, w_down,
                                  i).reshape(b, s, d)
    h = _fourier_layer(x, g_mix[0], wa, wb)
    h = moe(h, 0)
    lambda_init = 0.8 - 0.6 * math.exp(-0.3 * 1)
    h = _attention_layer(h, g_mix[1], w_qkv[0], w_attn_out[0], g_q[0], g_k[0], lq1[0], lk1[0], lq2[0],
                         lk2[0], g_sub[0], rel_bias, lambda_init)
    return moe(h, 1)


def kernel(x_prompt, x_sample, rel_bias, g_mix, g_ffn, w_fourier_out, w_qkv, w_attn_out, g_q, g_k,
           lambda_q1, lambda_k1, lambda_q2, lambda_k2, g_sub, w_router, w_gate, w_up, w_down):
    assert g_mix.shape[0] == 2, "one Fourier layer followed by one attention layer"
    wa, wb = _fold_w(w_fourier_out[0])
    run = lambda x: _trunk(x, rel_bias, wa, wb, g_mix, g_ffn, w_qkv, w_attn_out, g_q, g_k, lambda_q1, lambda_k1,
                           lambda_q2, lambda_k2, g_sub, w_router, w_gate, w_up, w_down)
    return (run(x_prompt), run(x_sample))
```

```python
import functools
import math

import numpy as np
import jax
import jax.numpy as jnp
from jax import lax
from jax.experimental import pallas as pl
from jax.experimental.pallas import tpu as pltpu

EPS = 1e-6
F32 = jnp.float32
BF16 = jnp.bfloat16
I32 = jnp.int32
HIGHEST = lax.Precision.HIGHEST
LOG2E = math.log2(math.e)

LANES = 128
SUBLANES = 8
FOURIER_GROUP_DIM = 128
HEAD_BLOCK = 128
HEAD_DIM = 64
ONES_ROWS = 16
N_EXPERTS = 16
CAPACITY_FACTOR = 2
FFN_ROW_CHUNK = 512
COMBINE_GROUP = 8
STEP_NEXT, STEP_FIRST, STEP_IDLE = 0, 1, 2
N_BUCKETS = 32
MAX_DISTANCE = 128
VMEM_LIMIT = 56 * 1024 * 1024


def _cparams(sem):
    return pltpu.CompilerParams(dimension_semantics=sem, vmem_limit_bytes=VMEM_LIMIT)


def _rms(x, g):
    return x * lax.rsqrt(jnp.mean(x * x, axis=-1, keepdims=True) + EPS) * g


def _dot(a, b):
    return jnp.dot(a, b, preferred_element_type=F32)


def _cos_sin(n):
    k = np.arange(n)
    ang = 2.0 * np.pi * ((k[:, None] * k[None, :]) % n) / n
    return np.cos(ang), np.sin(ang)


def _fold_w_kernel(cc_ref, sc_ref, w_ref, wa_ref, wb_ref):
    w = w_ref[...]
    wa_ref[...] = jnp.dot(cc_ref[...], w, precision=HIGHEST, preferred_element_type=F32).astype(BF16)
    wb_ref[...] = jnp.dot(sc_ref[...], w, precision=HIGHEST, preferred_element_type=F32).astype(BF16)


def _fold_w(w):
    d = w.shape[0]
    gd = FOURIER_GROUP_DIM
    c, s = _cos_sin(gd)
    cc = jnp.asarray(c / math.sqrt(gd), F32)
    sc = jnp.asarray(s / math.sqrt(gd), F32)
    const = pl.BlockSpec((gd, gd), lambda g: (0, 0))
    blk = pl.BlockSpec((gd, d), lambda g: (g, 0))
    return pl.pallas_call(
        _fold_w_kernel,
        grid=(d // gd,),
        in_specs=[const, const, blk],
        out_specs=[blk, blk],
        out_shape=[jax.ShapeDtypeStruct((d, d), BF16)] * 2,
        compiler_params=_cparams(("arbitrary",)),
    )(cc, sc, w)


def _fourier_a_kernel(x_ref, g_ref, f1_ref, twr_ref, twi_ref, yr_ref, yi_ref, *, nb, n1, d):
    j = pl.program_id(1)
    lane = lax.broadcasted_iota(I32, twr_ref.shape, 1)
    for c in range(nb):
        xn = _rms(x_ref[:, c, :], g_ref[...]).astype(BF16)
        y = _dot(f1_ref[...], xn)
        yr, yi = y[:n1], y[n1:]
        sel = lane == (j * nb + c)
        tr = jnp.sum(jnp.where(sel, twr_ref[...], 0.0), axis=1, keepdims=True)
        ti = jnp.sum(jnp.where(sel, twi_ref[...], 0.0), axis=1, keepdims=True)
        cols = slice(c * d, (c + 1) * d)
        yr_ref[:, cols] = (yr * tr - yi * ti).astype(BF16)
        yi_ref[:, cols] = (yr * ti + yi * tr).astype(BF16)


def _fourier_b_kernel(yr_ref, yi_ref, f2_ref, wa_ref, wb_ref, x_ref, o_ref, *, kb, n2):
    ars, ais = [], []
    for c in range(kb):
        rows = slice(c * n2, (c + 1) * n2)
        ys = jnp.concatenate([yr_ref[rows, :], yi_ref[rows, :]], axis=0)
        a = _dot(f2_ref[...], ys)
        ars.append(a[:n2].astype(BF16))
        ais.append(a[n2:].astype(BF16))
    ar = jnp.concatenate(ars, axis=0)
    ai = jnp.concatenate(ais, axis=0)
    out = _dot(ar, wa_ref[...]) + _dot(ai, wb_ref[...])
    for c in range(kb):
        o_ref[:, c, :] = x_ref[:, c, :] + out[c * n2:(c + 1) * n2]


def _split_len(s):
    n1 = 1 << (int(math.log2(s)) // 2)
    assert s % n1 == 0
    return n1, s // n1


def _fourier_layer(x, g, wa, wb):
    b, s, d = x.shape
    n1, n2 = _split_len(s)
    c1, s1 = _cos_sin(n1)
    c2, s2 = _cos_sin(n2)
    f1 = jnp.asarray(np.concatenate([c1, -s1], axis=0) / math.sqrt(n1), BF16)
    f2 = jnp.asarray(np.block([[c2, s2], [-s2, c2]]) / math.sqrt(n2), BF16)
    ang = 2.0 * np.pi * ((np.arange(n1)[:, None] * np.arange(n2)[None, :]) % s) / s
    twr = jnp.asarray(np.cos(ang), F32)
    twi = jnp.asarray(-np.sin(ang), F32)

    nb = SUBLANES
    blk_a = pl.BlockSpec((None, n1, nb, d), lambda bi, j: (bi, 0, j, 0))
    blk_ya = pl.BlockSpec((None, n1, nb * d), lambda bi, j: (bi, 0, j))
    full = lambda shape: pl.BlockSpec(shape, lambda bi, j: (0,) * len(shape))
    yr, yi = pl.pallas_call(
        functools.partial(_fourier_a_kernel, nb=nb, n1=n1, d=d),
        grid=(b, n2 // nb),
        in_specs=[blk_a, full((1, d)), full((2 * n1, n1)), full((n1, n2)), full((n1, n2))],
        out_specs=[blk_ya, blk_ya],
        out_shape=[jax.ShapeDtypeStruct((b, n1, n2 * d), BF16)] * 2,
        compiler_params=_cparams(("parallel", "parallel")),
    )(x.reshape(b, n1, n2, d), g.reshape(1, d), f1, twr, twi)

    kb = SUBLANES
    blk_y = pl.BlockSpec((None, kb * n2, d), lambda bi, i: (bi, i, 0))
    blk_x = pl.BlockSpec((None, n2, kb, d), lambda bi, i: (bi, 0, i, 0))
    out = pl.pallas_call(
        functools.partial(_fourier_b_kernel, kb=kb, n2=n2),
        grid=(b, n1 // kb),
        in_specs=[blk_y, blk_y, full((2 * n2, 2 * n2)), full((d, d)), full((d, d)), blk_x],
        out_specs=blk_x,
        out_shape=jax.ShapeDtypeStruct((b, n2, n1, d), F32),
        compiler_params=_cparams(("parallel", "parallel")),
    )(yr.reshape(b, s, d), yi.reshape(b, s, d), f2, wa, wb, x.reshape(b, n2, n1, d))
    return out.reshape(b, s, d)


def _router_kernel(h_ref, g_ref, wrt_ref, hn_ref, aff_ref):
    hn = _rms(h_ref[...], g_ref[...])
    hn_ref[...] = hn.astype(BF16)
    logits = lax.dot_general(wrt_ref[...], hn, (((1,), (1,)), ((), ())),
                             precision=HIGHEST, preferred_element_type=F32)
    ex = jnp.exp(logits - jnp.max(logits, axis=0, keepdims=True))
    aff_ref[...] = ex / jnp.sum(ex, axis=0, keepdims=True)


def _select_kernel(a_ref, pos_ref, cum_ref, *, e, nb, k):
    keys = lax.bitcast_convert_type(a_ref[...], I32).reshape(e, nb, LANES)

    def count(mask):
        part = jnp.sum(mask.astype(F32), axis=2, keepdims=True)
        return jnp.sum(part, axis=1, keepdims=True)

    def value_step(i, t):
        cand = t | (jnp.int32(1) << (30 - i))
        return jnp.where(count(keys >= cand) >= k, cand, t)

    thr = lax.fori_loop(0, 31, value_step, jnp.zeros((e, 1, 1), I32))
    gt = keys > thr
    eq = keys == thr
    need = k - count(gt)
    tok = (lax.broadcasted_iota(I32, (e, nb, LANES), 1) * LANES
           + lax.broadcasted_iota(I32, (e, nb, LANES), 2))
    nbits = (nb * LANES - 1).bit_length()

    def index_step(i, v):
        cand = v | (jnp.int32(1) << (nbits - 1 - i))
        return jnp.where(count(eq & (tok < cand)) < need, cand, v)

    last = lax.fori_loop(0, nbits, index_step, jnp.zeros((e, 1, 1), I32))
    sel = (gt | (eq & (tok <= last))).reshape(e * nb, LANES)
    self32 = sel.astype(F32)

    r = lax.broadcasted_iota(I32, (LANES, LANES), 0)
    c = lax.broadcasted_iota(I32, (LANES, LANES), 1)
    inc = _dot(self32.astype(BF16), (r <= c).astype(BF16))
    tot = jnp.broadcast_to(inc[:, LANES - 1:LANES], (e * nb, LANES)).astype(BF16)
    rb = lax.broadcasted_iota(I32, (nb, nb), 0)
    cb = lax.broadcasted_iota(I32, (nb, nb), 1)
    lower = (cb < rb).astype(BF16)
    before = jnp.concatenate([_dot(lower, tot[x * nb:(x + 1) * nb]) for x in range(e)], axis=0)
    pos = inc - self32 + before
    pos_ref[...] = jnp.where(sel, pos, -1.0).astype(I32)
    cum_ref[...] = before.astype(I32)


def _one_hot_mask(pos_ref, row_tile, rt):
    tt = pos_ref.shape[-1]
    rows = lax.broadcasted_iota(I32, (rt, tt), 0) + row_tile * rt
    return pos_ref[...] == rows


def _gather_kernel(gt, gv, gk, gf, hn_ref, pos_ref, aff_ref, xe_ref, gc_ref, *, rt, n_r):
    s = pl.program_id(0)
    tt = pos_ref.shape[-1]

    @pl.when(gf[s] != STEP_IDLE)
    def _():
        mask = _one_hot_mask(pos_ref, gk[s] % n_r, rt)
        tokens = hn_ref[pl.ds(pl.multiple_of(gt[s] * tt, tt), tt), :]
        rows = _dot(jnp.where(mask, 1.0, 0.0).astype(BF16), tokens)
        gate = jnp.sum(jnp.where(mask, aff_ref[...], 0.0), axis=1, keepdims=True)

        @pl.when(gf[s] & STEP_FIRST != 0)
        def _():
            xe_ref[...] = rows.astype(BF16)
            gc_ref[...] = gate

        @pl.when(gf[s] & STEP_FIRST == 0)
        def _():
            xe_ref[...] = (xe_ref[...].astype(F32) + rows).astype(BF16)
            gc_ref[...] += gate


def _ffn_kernel(xe_ref, wg_ref, wu_ref, wd_ref, gc_ref, ye_ref, *acc_refs):
    j = pl.program_id(2)
    n_chunks = len(acc_refs)
    rc = acc_refs[0].shape[0]

    @pl.when(j == 0)
    def _():
        for acc_ref in acc_refs:
            acc_ref[...] = jnp.zeros(acc_ref.shape, F32)

    wg = wg_ref[...].astype(BF16)
    wu = wu_ref[...].astype(BF16)
    wd = wd_ref[...].astype(BF16)

    def gate_up(c):
        x = xe_ref[c * rc:(c + 1) * rc, :]
        return _dot(x, wg), _dot(x, wu)

    ahead = gate_up(0)
    for c in range(n_chunks):
        g, u = ahead
        if c + 1 < n_chunks:
            ahead = gate_up(c + 1)
        hid = (g / (1.0 + jnp.exp(-g))) * u
        acc_refs[c][...] += _dot(hid.astype(BF16), wd)

    @pl.when(j == pl.num_programs(2) - 1)
    def _():
        for c, acc_ref in enumerate(acc_refs):
            rows = slice(c * rc, (c + 1) * rc)
            ye_ref[rows, :] = (acc_ref[...] * gc_ref[rows, :]).astype(BF16)


def _combine_kernel(gi, gv, gk, gf, h_ref, *refs, rt, n_r, group):
    pos_refs, ye_refs, o_ref = refs[:group], refs[group:2 * group], refs[2 * group]
    s = pl.program_id(0)

    @pl.when(gf[s] != STEP_IDLE)
    def _():
        masks = [_one_hot_mask(pos_refs[k], gi[s * group + k] % n_r, rt) & (gv[s * group + k] == 1)
                 for k in range(group)]
        onehot = jnp.where(jnp.concatenate(masks, axis=0), 1.0, 0.0).astype(BF16)
        ye = jnp.concatenate([ye_refs[k][...] for k in range(group)], axis=0)
        back = lax.dot_general(onehot, ye, (((0,), (0,)), ((), ())), preferred_element_type=F32)

        @pl.when(gf[s] & STEP_FIRST != 0)
        def _():
            o_ref[...] = h_ref[...] + back

        @pl.when(gf[s] & STEP_FIRST == 0)
        def _():
            o_ref[...] += back


def _schedule_kernel(lo_ref, *out_refs, n_e, n_r, n_t, rt, cap, groups):
    def emitter(item_ref, valid_ref, key_ref, flag_ref, group):
        for i_ref, fill in ((item_ref, 0), (valid_ref, 0), (key_ref, 0), (flag_ref, STEP_IDLE)):
            def clear(i, c, i_ref=i_ref, fill=fill):
                i_ref[i] = fill
                return c
            lax.fori_loop(0, i_ref.shape[0], clear, 0)

        def emit(state, key, item):
            step, cnt, cur = state
            fresh = key != cur
            step = step + (fresh & (cnt > 0)).astype(I32)
            cnt = jnp.where(fresh, 0, cnt)
            item_ref[step * group + cnt] = item
            valid_ref[step * group + cnt] = 1

            @pl.when(cnt == 0)
            def _():
                key_ref[step] = key
                flag_ref[step] = jnp.where(fresh, STEP_FIRST, STEP_NEXT)

            full = cnt + 1 == group
            return step + full.astype(I32), jnp.where(full, 0, cnt + 1), key

        def finish(state):
            step, cnt, cur = state

            def pad(i, c):
                key_ref[i] = cur
                return c
            lax.fori_loop(step + (cnt > 0).astype(I32), key_ref.shape[0], pad, 0)

        return emit, finish

    def rows_of(e, t):
        lo = lo_ref[e, t]
        hi = jnp.where(t + 1 < n_t, lo_ref[e, jnp.minimum(t + 1, n_t - 1)], cap)
        return lo, hi

    start = (jnp.int32(0), jnp.int32(0), jnp.int32(-1))
    emit_g, finish_g = emitter(*out_refs[:4], groups[0])

    def gather_tile(i, state):
        e, t = i // n_t, i % n_t
        lo, hi = rows_of(e, t)
        return lax.fori_loop(lo // rt, jnp.where(hi > lo, (hi - 1) // rt + 1, lo // rt),
                             lambda r, st: emit_g(st, e * n_r + r, t), state)

    finish_g(lax.fori_loop(0, n_e * n_t, gather_tile, start))

    emit_c, finish_c = emitter(*out_refs[4:], groups[1])

    def combine_tile(i, state):
        t, e = i // n_e, i % n_e
        lo, hi = rows_of(e, t)
        first = jnp.minimum(lo // rt, n_r - 1)
        stop = jnp.where(hi > lo, (hi - 1) // rt + 1, jnp.where(e == 0, first + 1, first))
        return lax.fori_loop(first, stop, lambda r, st: emit_c(st, t, e * n_r + r), state)

    finish_c(lax.fori_loop(0, n_t * n_e, combine_tile, start))


def _schedules(lo, cap, rt, groups):
    n_e, n_t = lo.shape
    n_r = cap // rt
    n_true = n_e * (n_r + n_t - 1)
    steps = ((n_true + (groups[0] - 1) * n_e * n_r) // groups[0],
             (n_true + n_t + (groups[1] - 1) * n_t) // groups[1])
    sizes = [steps[0] * groups[0]] * 2 + [steps[0]] * 2 + [steps[1] * groups[1]] * 2 + [steps[1]] * 2
    smem = pl.BlockSpec(memory_space=pltpu.SMEM)
    out = pl.pallas_call(
        functools.partial(_schedule_kernel, n_e=n_e, n_r=n_r, n_t=n_t, rt=rt, cap=cap, groups=groups),
        in_specs=[smem],
        out_specs=[smem] * 8,
        out_shape=[jax.ShapeDtypeStruct((k,), I32) for k in sizes],
    )(lo)
    return out[:4], out[4:]


def _moe_layer(h, g, w_router, w_gate, w_up, w_down, layer):
    n, d = h.shape
    e = N_EXPERTS
    cap = CAPACITY_FACTOR * n // e
    f = w_gate.shape[-1]
    nb = n // LANES
    tn = min(1024, n)

    hn, aff = pl.pallas_call(
        _router_kernel,
        grid=(n // tn,),
        in_specs=[pl.BlockSpec((tn, d), lambda i: (i, 0)),
                  pl.BlockSpec((1, d), lambda i: (0, 0)),
                  pl.BlockSpec((e, d), lambda i: (0, 0))],
        out_specs=[pl.BlockSpec((tn, d), lambda i: (i, 0)),
                   pl.BlockSpec((e, tn), lambda i: (0, i))],
        out_shape=[jax.ShapeDtypeStruct((n, d), BF16), jax.ShapeDtypeStruct((e, n), F32)],
        compiler_params=_cparams(("parallel",)),
    )(h, g.reshape(1, d), w_router.T)

    whole = pl.BlockSpec((e * nb, LANES), lambda i: (0, 0))
    pos, cum = pl.pallas_call(
        functools.partial(_select_kernel, e=e, nb=nb, k=cap),
        grid=(1,),
        in_specs=[whole],
        out_specs=[whole, whole],
        out_shape=[jax.ShapeDtypeStruct((e * nb, LANES), I32)] * 2,
        compiler_params=_cparams(("arbitrary",)),
    )(aff.reshape(e * nb, LANES))

    rt = min(256, cap)
    tt = tn
    n_r, n_t = cap // rt, n // tt
    lo = cum[:, 0].reshape(e, nb)[:, ::tt // LANES]
    (at, av, ak, af), (ci, cv, ck, cf) = _schedules(lo, cap, rt, (1, COMBINE_GROUP))

    pos3 = pos.reshape(e * n_t, 1, tt)
    aff3 = aff.reshape(e * n_t, 1, tt)
    row_spec = pl.BlockSpec((None, 1, tt), lambda s, at, av, ak, af: (ak[s] // n_r * n_t + at[s], 0, 0))
    xe, gc = pl.pallas_call(
        functools.partial(_gather_kernel, rt=rt, n_r=n_r),
        grid_spec=pltpu.PrefetchScalarGridSpec(
            num_scalar_prefetch=4,
            grid=(ak.shape[0],),
            in_specs=[pl.BlockSpec((n, d), lambda s, at, av, ak, af: (0, 0), pipeline_mode=pl.Buffered(1)),
                      row_spec, row_spec],
            out_specs=[pl.BlockSpec((None, rt, d), lambda s, at, av, ak, af: (ak[s] // n_r, ak[s] % n_r, 0)),
                       pl.BlockSpec((None, rt, 1), lambda s, at, av, ak, af: (ak[s] // n_r, ak[s] % n_r, 0))],
        ),
        out_shape=[jax.ShapeDtypeStruct((e, cap, d), BF16), jax.ShapeDtypeStruct((e, cap, 1), F32)],
        compiler_params=_cparams(("arbitrary",)),
    )(at, av, ak, af, hn, pos3, aff3)

    tm = min(2048, cap)
    tf = 256
    ye = pl.pallas_call(
        _ffn_kernel,
        grid=(e, cap // tm, f // tf),
        in_specs=[pl.BlockSpec((None, tm, d), lambda x, r, j: (x, r, 0)),
                  pl.BlockSpec((None, None, d, tf), lambda x, r, j: (layer, x, 0, j)),
                  pl.BlockSpec((None, None, d, tf), lambda x, r, j: (layer, x, 0, j)),
                  pl.BlockSpec((None, None, tf, d), lambda x, r, j: (layer, x, j, 0)),
                  pl.BlockSpec((None, tm, 1), lambda x, r, j: (x, r, 0))],
        out_specs=pl.BlockSpec((None, tm, d), lambda x, r, j: (x, r, 0)),
        out_shape=jax.ShapeDtypeStruct((e, cap, d), BF16),
        scratch_shapes=[pltpu.VMEM((min(FFN_ROW_CHUNK, tm), d), F32)] * (tm // min(FFN_ROW_CHUNK, tm)),
        compiler_params=_cparams(("parallel", "parallel", "arbitrary")),
    )(xe, w_gate, w_up, w_down, gc)

    grp = COMBINE_GROUP
    pos_spec = lambda k: pl.BlockSpec(
        (None, 1, tt), lambda s, ci, cv, ck, cf: (ci[s * grp + k] // n_r * n_t + ck[s], 0, 0))
    ye_spec = lambda k: pl.BlockSpec(
        (None, rt, d), lambda s, ci, cv, ck, cf: (ci[s * grp + k] // n_r, ci[s * grp + k] % n_r, 0))
    tile_spec = pl.BlockSpec((tt, d), lambda s, ci, cv, ck, cf: (ck[s], 0))
    return pl.pallas_call(
        functools.partial(_combine_kernel, rt=rt, n_r=n_r, group=grp),
        grid_spec=pltpu.PrefetchScalarGridSpec(
            num_scalar_prefetch=4,
            grid=(ck.shape[0],),
            in_specs=[tile_spec] + [pos_spec(k) for k in range(grp)] + [ye_spec(k) for k in range(grp)],
            out_specs=tile_spec,
        ),
        out_shape=jax.ShapeDtypeStruct((n, d), F32),
        compiler_params=_cparams(("arbitrary",)),
    )(ci, cv, ck, cf, h, *([pos3] * grp), *([ye] * grp))


def _t5_bucket_np(rel):
    nb = N_BUCKETS // 2
    max_exact = nb // 2
    ret = np.where(rel > 0, nb, 0)
    n = np.abs(rel)
    nf = np.maximum(n, 1).astype(np.float64)
    large = max_exact + (np.log(nf / max_exact) / math.log(MAX_DISTANCE / max_exact)
                         * (nb - max_exact)).astype(np.int64)
    large = np.minimum(large, nb - 1)
    return ret + np.where(n < max_exact, n, large)


def _far_distance():
    n = np.arange(1, 4 * MAX_DISTANCE)
    last = _t5_bucket_np(-n)
    return int(n[np.nonzero(last != last[-1])[0][-1]] + 1)


FAR = _far_distance()


SUM_LIMIT = 1e30
LOOKAHEAD = 2
N_BIAS_TILES = 5
assert FAR <= LANES + 1


def _bias_tiles_kernel(near_ref, far_ref, t_ref):
    h = pl.program_id(0)
    d = lax.broadcasted_iota(I32, (LANES, LANES), 0) - lax.broadcasted_iota(I32, (LANES, LANES), 1)
    for idx in range(N_BIAS_TILES):
        rel = d + LANES * (idx - 2)
        tile = jnp.where(rel < 0, far_ref[h, 0], far_ref[h, 1])
        if abs(idx - 2) <= 1:
            tile = lax.fori_loop(
                0, 2 * FAR - 1, lambda r, t: jnp.where(rel == r - (FAR - 1), near_ref[h, r], t), tile)
        t_ref[idx] = tile


def _bias_tiles(rel_bias):
    nh = rel_bias.shape[1]
    near = (rel_bias[_t5_bucket_np(np.arange(-(FAR - 1), FAR))] * LOG2E).T
    far = jnp.stack([rel_bias[N_BUCKETS // 2 - 1], rel_bias[N_BUCKETS - 1]], axis=1) * LOG2E
    smem = pl.BlockSpec(memory_space=pltpu.SMEM)
    tiles = pl.pallas_call(
        _bias_tiles_kernel,
        grid=(nh,),
        in_specs=[smem, smem],
        out_specs=pl.BlockSpec((None, N_BIAS_TILES, LANES, LANES), lambda h: (h, 0, 0, 0)),
        out_shape=jax.ShapeDtypeStruct((nh, N_BIAS_TILES, LANES, LANES), F32),
        compiler_params=_cparams(("arbitrary",)),
    )(near, far)
    return tiles, far


def _qkv_kernel(h_ref, g_ref, w_ref, bd_ref, gq_ref, gk_ref, qt_ref, k_ref, vt_ref, *, d, nh):
    hn = _rms(h_ref[...], g_ref[...]).astype(BF16)
    qkv = _dot(hn, w_ref[...])
    q, k, v = qkv[:, :d], qkv[:, d:2 * d], qkv[:, 2 * d:]

    def head_norm(x, gain):
        ms = _dot((x * x).astype(BF16), bd_ref[...])
        return x * lax.rsqrt(ms + EPS) * gain

    qn = head_norm(q, gq_ref[...]) * (HEAD_DIM ** -0.5 * LOG2E)
    k_ref[...] = head_norm(k, gk_ref[...]).astype(BF16)
    for h in range(nh):
        cols = slice(h * HEAD_BLOCK, (h + 1) * HEAD_BLOCK)
        qt_ref[h] = qn[:, cols].T.astype(BF16)
        vt_ref[h] = v[:, cols].T.astype(BF16)


def _attn_kernel(far_ref, qt_ref, k_ref, vt_ref, tiles_ref, lq1_ref, lk1_ref, lq2_ref, lk2_ref, gs_ref, o_ref,
                 m_ref, l_ref, acc_ref, redo_ref, *, tq, tc, n_chunks, lambda_init):
    h = pl.program_id(1)
    i = pl.program_id(2)
    j = pl.program_id(3)

    qt = qt_ref[...]
    row = lax.broadcasted_iota(I32, qt.shape, 0)
    comps = (jnp.where(row < HEAD_DIM, qt, jnp.zeros_like(qt)),
             jnp.where(row >= HEAD_DIM, qt, jnp.zeros_like(qt)))
    q0 = i * tq
    k_first = j * n_chunks * tc
    ones = jnp.ones((ONES_ROWS, tc), BF16)

    def far_side(k0):
        before = k0 + tc - 1 - q0 <= -FAR
        after = k0 - (q0 + tq - 1) >= FAR
        return before, after

    def bias_rows(k0, n_rows):
        blk = (k0 - q0) // LANES
        mid = N_BIAS_TILES // 2
        take = min(n_rows, LANES)
        return jnp.concatenate(
            [jnp.concatenate([tiles_ref[jnp.clip(blk + a - b, -mid, mid) + mid][:take]
                              for b in range(tq // LANES)], axis=1)
             for a in range(max(n_rows // LANES, 1))], axis=0)

    @pl.when(j == 0)
    def _():
        head = bias_rows(0, ONES_ROWS)
        for c in range(2):
            m_ref[c] = jnp.max(_dot(k_ref[:ONES_ROWS, :], comps[c]) + head, axis=0, keepdims=True)
        l_ref[...] = jnp.zeros(l_ref.shape, F32)
        acc_ref[...] = jnp.zeros(acc_ref.shape, F32)

    def scores(m, with_band):
        jj, c = m // 2, m % 2
        s = _dot(k_ref[jj * tc:(jj + 1) * tc, :], comps[c])
        return s + bias_rows(k_first + jj * tc, tc) if with_band else s

    def run(with_band, exact):
        state = [(m_ref[c], l_ref[c], acc_ref[c]) for c in range(2)]
        n_maps = 2 * n_chunks
        ahead = [scores(m, with_band) for m in range(min(LOOKAHEAD, n_maps))]
        for m in range(n_maps):
            jj, c = m // 2, m % 2
            s = ahead.pop(0)
            if m + LOOKAHEAD < n_maps:
                ahead.append(scores(m + LOOKAHEAD, with_band))
            rows = slice(jj * tc, (jj + 1) * tc)
            if with_band:
                const = 0.0
            else:
                before, _ = far_side(k_first + jj * tc)
                const = jnp.where(before, far_ref[h, 0], far_ref[h, 1])
            vt1 = jnp.concatenate([vt_ref[:, rows], ones], axis=0)
            m_old, l_old, acc_old = state[c]
            if exact:
                m_new = jnp.maximum(m_old, jnp.max(s, axis=0, keepdims=True) + const)
                alpha = jnp.exp2(m_old - m_new)
                pv = _dot(vt1, jnp.exp2(s - (m_new - const)).astype(BF16))
                state[c] = (m_new, alpha * l_old + pv[HEAD_BLOCK:HEAD_BLOCK + 1],
                            alpha * acc_old + pv[:HEAD_BLOCK])
            else:
                pv = _dot(vt1, jnp.exp2(s - (m_old - const)).astype(BF16))
                state[c] = (m_old, l_old + pv[HEAD_BLOCK:HEAD_BLOCK + 1], acc_old + pv[:HEAD_BLOCK])
        return state

    def commit(state):
        for c in range(2):
            m_ref[c], l_ref[c], acc_ref[c] = state[c]

    def streamed(with_band):
        state = run(with_band, exact=False)
        worst = [jnp.maximum(jnp.max(jnp.where(l < SUM_LIMIT, 0.0, 1.0)),
                             jnp.max(jnp.where(jnp.abs(acc) < SUM_LIMIT, 0.0, 1.0))) for _, l, acc in state]
        safe = jnp.maximum(worst[0], worst[1]) == 0.0
        redo_ref[0] = jnp.where(safe, 0, 1)

        @pl.when(safe)
        def _():
            commit(state)

    last_before, _ = far_side(k_first + (n_chunks - 1) * tc)
    _, first_after = far_side(k_first)
    all_far = last_before | first_after

    @pl.when(all_far)
    def _():
        streamed(False)

    @pl.when(jnp.logical_not(all_far))
    def _():
        streamed(True)

    @pl.when(redo_ref[0] == 1)
    def _():
        commit(run(True, exact=True))

    @pl.when(j == pl.num_programs(3) - 1)
    def _():
        lam = (jnp.exp(jnp.sum(lq1_ref[...] * lk1_ref[...], keepdims=True))
               - jnp.exp(jnp.sum(lq2_ref[...] * lk2_ref[...], keepdims=True)) + lambda_init)
        a = acc_ref[0] / l_ref[0] - lam * (acc_ref[1] / l_ref[1])
        a = a * lax.rsqrt(jnp.mean(a * a, axis=0, keepdims=True) + EPS) * gs_ref[...]
        o_ref[...] = (a * (1.0 - lambda_init)).T.astype(BF16)


def _proj_kernel(o_ref, w_ref, h_ref, out_ref):
    out_ref[...] = h_ref[...] + _dot(o_ref[...], w_ref[...])


def _attn_tiles(s):
    tq = min(1024, s)
    tc = min(512, s)
    assert tq % LANES == 0 and tc % LANES == 0
    return tq, tc


def _attention_layer(h, g, w_qkv, w_o, g_q, g_k, lq1, lk1, lq2, lk2, g_sub, rel_bias, lambda_init):
    b, s, d = h.shape
    nh = d // HEAD_BLOCK
    tn = min(512, s)
    hd = np.arange(d) // HEAD_DIM
    bd = jnp.asarray((hd[:, None] == hd[None, :]) / HEAD_DIM, BF16)
    tile_gain = lambda x: jnp.tile(x, d // HEAD_DIM).reshape(1, d)
    const2 = lambda shape: pl.BlockSpec(shape, lambda bi, i: (0,) * len(shape))
    qt, k, vt = pl.pallas_call(
        functools.partial(_qkv_kernel, d=d, nh=nh),
        grid=(b, s // tn),
        in_specs=[pl.BlockSpec((None, tn, d), lambda bi, i: (bi, i, 0)),
                  const2((1, d)), const2((d, 3 * d)), const2((d, d)), const2((1, d)), const2((1, d))],
        out_specs=[pl.BlockSpec((None, nh, HEAD_BLOCK, tn), lambda bi, i: (bi, 0, 0, i)),
                   pl.BlockSpec((None, tn, d), lambda bi, i: (bi, i, 0)),
                   pl.BlockSpec((None, nh, HEAD_BLOCK, tn), lambda bi, i: (bi, 0, 0, i))],
        out_shape=[jax.ShapeDtypeStruct((b, nh, HEAD_BLOCK, s), BF16),
                   jax.ShapeDtypeStruct((b, s, d), BF16),
                   jax.ShapeDtypeStruct((b, nh, HEAD_BLOCK, s), BF16)],
        compiler_params=_cparams(("parallel", "parallel")),
    )(h, g.reshape(1, d), w_qkv.astype(BF16), bd, tile_gain(g_q), tile_gain(g_k))

    tq, tc = _attn_tiles(s)
    tko = min(4096, s)
    n_chunks = tko // tc
    tiles, far = _bias_tiles(rel_bias)
    const4 = lambda shape: pl.BlockSpec(shape, lambda bi, hi, i, j: (0,) * len(shape))
    lam_spec = const4((1, HEAD_DIM))
    o = pl.pallas_call(
        functools.partial(_attn_kernel, tq=tq, tc=tc, n_chunks=n_chunks, lambda_init=lambda_init),
        grid=(b, nh, s // tq, s // tko),
        in_specs=[pl.BlockSpec(memory_space=pltpu.SMEM),
                  pl.BlockSpec((None, None, HEAD_BLOCK, tq), lambda bi, hi, i, j: (bi, hi, 0, i)),
                  pl.BlockSpec((None, tko, HEAD_BLOCK), lambda bi, hi, i, j: (bi, j, hi)),
                  pl.BlockSpec((None, None, HEAD_BLOCK, tko), lambda bi, hi, i, j: (bi, hi, 0, j)),
                  pl.BlockSpec((None, N_BIAS_TILES, LANES, LANES), lambda bi, hi, i, j: (hi, 0, 0, 0)),
                  lam_spec, lam_spec, lam_spec, lam_spec,
                  const4((HEAD_BLOCK, 1))],
        out_specs=pl.BlockSpec((None, tq, HEAD_BLOCK), lambda bi, hi, i, j: (bi, i, hi)),
        out_shape=jax.ShapeDtypeStruct((b, s, d), BF16),
        scratch_shapes=[pltpu.VMEM((2, 1, tq), F32), pltpu.VMEM((2, 1, tq), F32),
                        pltpu.VMEM((2, HEAD_BLOCK, tq), F32), pltpu.SMEM((1,), I32)],
        compiler_params=_cparams(("parallel", "parallel", "parallel", "arbitrary")),
    )(far, qt, k, vt, tiles, lq1.reshape(1, -1), lk1.reshape(1, -1), lq2.reshape(1, -1), lk2.reshape(1, -1),
      g_sub.reshape(HEAD_BLOCK, 1))

    n = b * s
    tp = min(1024, n)
    out = pl.pallas_call(
        _proj_kernel,
        grid=(n // tp,),
        in_specs=[pl.BlockSpec((tp, d), lambda i: (i, 0)),
                  pl.BlockSpec((d, d), lambda i: (0, 0)),
                  pl.BlockSpec((tp, d), lambda i: (i, 0))],
        out_specs=pl.BlockSpec((tp, d), lambda i: (i, 0)),
        out_shape=jax.ShapeDtypeStruct((n, d), F32),
        compiler_params=_cparams(("parallel",)),
    )(o.reshape(n, d), w_o.astype(BF16), h.reshape(n, d))
    return out.reshape(b, s, d)


def _trunk(x, rel_bias, wa, wb, g_mix, g_ffn, w_qkv, w_attn_out, g_q, g_k, lq1, lk1, lq2, lk2, g_sub,
           w_router, w_gate, w_up, w_down):
    b, s, d = x.shape
    moe = lambda h, i: _moe_layer(h.reshape(b * s, d), g_ffn[i], w_router[i], w_gate, w_up, w_down,
                                  i).reshape(b, s, d)
    h = _fourier_layer(x, g_mix[0], wa, wb)
    h = moe(h, 0)
    lambda_init = 0.8 - 0.6 * math.exp(-0.3 * 1)
    h = _attention_layer(h, g_mix[1], w_qkv[0], w_attn_out[0], g_q[0], g_k[0], lq1[0], lk1[0], lq2[0],
                         lk2[0], g_sub[0], rel_bias, lambda_init)
    return moe(h, 1)


def kernel(x_prompt, x_sample, rel_bias, g_mix, g_ffn, w_fourier_out, w_qkv, w_attn_out, g_q, g_k,
           lambda_q1, lambda_k1, lambda_q2, lambda_k2, g_sub, w_router, w_gate, w_up, w_down):
    assert g_mix.shape[0] == 2, "one Fourier layer followed by one attention layer"
    wa, wb = _fold_w(w_fourier_out[0])
    run = lambda x: _trunk(x, rel_bias, wa, wb, g_mix, g_ffn, w_qkv, w_attn_out, g_q, g_k, lambda_q1, lambda_k1,
                           lambda_q2, lambda_k2, g_sub, w_router, w_gate, w_up, w_down)
    return (run(x_prompt), run(x_sample))
```

```python
import functools
import math

import numpy as np
import jax
import jax.numpy as jnp
from jax import lax
from jax.experimental import pallas as pl
from jax.experimental.pallas import tpu as pltpu

EPS = 1e-6
F32 = jnp.float32
BF16 = jnp.bfloat16
I32 = jnp.int32
HIGHEST = lax.Precision.HIGHEST
LOG2E = math.log2(math.e)

LANES = 128
SUBLANES = 8
FOURIER_GROUP_DIM = 128
HEAD_BLOCK = 128
HEAD_DIM = 64
ONES_ROWS = 16
N_EXPERTS = 16
CAPACITY_FACTOR = 2
FFN_ROW_CHUNK = 512
COMBINE_GROUP = 8
STEP_NEXT, STEP_FIRST, STEP_IDLE = 0, 1, 2
N_BUCKETS = 32
MAX_DISTANCE = 128
VMEM_LIMIT = 56 * 1024 * 1024


def _cparams(sem):
    return pltpu.CompilerParams(dimension_semantics=sem, vmem_limit_bytes=VMEM_LIMIT)


def _rms(x, g):
    return x * lax.rsqrt(jnp.mean(x * x, axis=-1, keepdims=True) + EPS) * g


def _dot(a, b):
    return jnp.dot(a, b, preferred_element_type=F32)


def _cos_sin(n):
    k = np.arange(n)
    ang = 2.0 * np.pi * ((k[:, None] * k[None, :]) % n) / n
    return np.cos(ang), np.sin(ang)


def _fold_w_kernel(cc_ref, sc_ref, w_ref, wa_ref, wb_ref):
    w = w_ref[...]
    wa_ref[...] = jnp.dot(cc_ref[...], w, precision=HIGHEST, preferred_element_type=F32).astype(BF16)
    wb_ref[...] = jnp.dot(sc_ref[...], w, precision=HIGHEST, preferred_element_type=F32).astype(BF16)


def _fold_w(w):
    d = w.shape[0]
    gd = FOURIER_GROUP_DIM
    c, s = _cos_sin(gd)
    cc = jnp.asarray(c / math.sqrt(gd), F32)
    sc = jnp.asarray(s / math.sqrt(gd), F32)
    const = pl.BlockSpec((gd, gd), lambda g: (0, 0))
    blk = pl.BlockSpec((gd, d), lambda g: (g, 0))
    return pl.pallas_call(
        _fold_w_kernel,
        grid=(d // gd,),
        in_specs=[const, const, blk],
        out_specs=[blk, blk],
        out_shape=[jax.ShapeDtypeStruct((d, d), BF16)] * 2,
        compiler_params=_cparams(("arbitrary",)),
    )(cc, sc, w)


def _fourier_a_kernel(x_ref, g_ref, f1_ref, twr_ref, twi_ref, yr_ref, yi_ref, *, nb, n1, d):
    j = pl.program_id(1)
    lane = lax.broadcasted_iota(I32, twr_ref.shape, 1)
    for c in range(nb):
        xn = _rms(x_ref[:, c, :], g_ref[...]).astype(BF16)
        y = _dot(f1_ref[...], xn)
        yr, yi = y[:n1], y[n1:]
        sel = lane == (j * nb + c)
        tr = jnp.sum(jnp.where(sel, twr_ref[...], 0.0), axis=1, keepdims=True)
        ti = jnp.sum(jnp.where(sel, twi_ref[...], 0.0), axis=1, keepdims=True)
        cols = slice(c * d, (c + 1) * d)
        yr_ref[:, cols] = (yr * tr - yi * ti).astype(BF16)
        yi_ref[:, cols] = (yr * ti + yi * tr).astype(BF16)


def _fourier_b_kernel(yr_ref, yi_ref, f2_ref, wa_ref, wb_ref, x_ref, o_ref, *, kb, n2):
    ars, ais = [], []
    for c in range(kb):
        rows = slice(c * n2, (c + 1) * n2)
        ys = jnp.concatenate([yr_ref[rows, :], yi_ref[rows, :]], axis=0)
        a = _dot(f2_ref[...], ys)
        ars.append(a[:n2].astype(BF16))
        ais.append(a[n2:].astype(BF16))
    ar = jnp.concatenate(ars, axis=0)
    ai = jnp.concatenate(ais, axis=0)
    out = _dot(ar, wa_ref[...]) + _dot(ai, wb_ref[...])
    for c in range(kb):
        o_ref[:, c, :] = x_ref[:, c, :] + out[c * n2:(c + 1) * n2]


def _split_len(s):
    n1 = 1 << (int(math.log2(s)) // 2)
    assert s % n1 == 0
    return n1, s // n1


def _fourier_layer(x, g, wa, wb):
    b, s, d = x.shape
    n1, n2 = _split_len(s)
    c1, s1 = _cos_sin(n1)
    c2, s2 = _cos_sin(n2)
    f1 = jnp.asarray(np.concatenate([c1, -s1], axis=0) / math.sqrt(n1), BF16)
    f2 = jnp.asarray(np.block([[c2, s2], [-s2, c2]]) / math.sqrt(n2), BF16)
    ang = 2.0 * np.pi * ((np.arange(n1)[:, None] * np.arange(n2)[None, :]) % s) / s
    twr = jnp.asarray(np.cos(ang), F32)
    twi = jnp.asarray(-np.sin(ang), F32)

    nb = SUBLANES
    blk_a = pl.BlockSpec((None, n1, nb, d), lambda bi, j: (bi, 0, j, 0))
    blk_ya = pl.BlockSpec((None, n1, nb * d), lambda bi, j: (bi, 0, j))
    full = lambda shape: pl.BlockSpec(shape, lambda bi, j: (0,) * len(shape))
    yr, yi = pl.pallas_call(
        functools.partial(_fourier_a_kernel, nb=nb, n1=n1, d=d),
        grid=(b, n2 // nb),
        in_specs=[blk_a, full((1, d)), full((2 * n1, n1)), full((n1, n2)), full((n1, n2))],
        out_specs=[blk_ya, blk_ya],
        out_shape=[jax.ShapeDtypeStruct((b, n1, n2 * d), BF16)] * 2,
        compiler_params=_cparams(("parallel", "parallel")),
    )(x.reshape(b, n1, n2, d), g.reshape(1, d), f1, twr, twi)

    kb = SUBLANES
    blk_y = pl.BlockSpec((None, kb * n2, d), lambda bi, i: (bi, i, 0))
    blk_x = pl.BlockSpec((None, n2, kb, d), lambda bi, i: (bi, 0, i, 0))
    out = pl.pallas_call(
        functools.partial(_fourier_b_kernel, kb=kb, n2=n2),
        grid=(b, n1 // kb),
        in_specs=[blk_y, blk_y, full((2 * n2, 2 * n2)), full((d, d)), full((d, d)), blk_x],
        out_specs=blk_x,
        out_shape=jax.ShapeDtypeStruct((b, n2, n1, d), F32),
        compiler_params=_cparams(("parallel", "parallel")),
    )(yr.reshape(b, s, d), yi.reshape(b, s, d), f2, wa, wb, x.reshape(b, n2, n1, d))
    return out.reshape(b, s, d)


def _router_kernel(h_ref, g_ref, wrt_ref, hn_ref, aff_ref):
    hn = _rms(h_ref[...], g_ref[...])
    hn_ref[...] = hn.astype(BF16)
    logits = lax.dot_general(wrt_ref[...], hn, (((1,), (1,)), ((), ())),
                             precision=HIGHEST, preferred_element_type=F32)
    ex = jnp.exp(logits - jnp.max(logits, axis=0, keepdims=True))
    aff_ref[...] = ex / jnp.sum(ex, axis=0, keepdims=True)


def _select_kernel(a_ref, pos_ref, cum_ref, *, e, nb, k):
    keys = lax.bitcast_convert_type(a_ref[...], I32).reshape(e, nb, LANES)

    def count(mask):
        part = jnp.sum(mask.astype(F32), axis=2, keepdims=True)
        return jnp.sum(part, axis=1, keepdims=True)

    def value_step(i, t):
        cand = t | (jnp.int32(1) << (30 - i))
        return jnp.where(count(keys >= cand) >= k, cand, t)

    thr = lax.fori_loop(0, 31, value_step, jnp.zeros((e, 1, 1), I32))
    gt = keys > thr
    eq = keys == thr
    need = k - count(gt)
    tok = (lax.broadcasted_iota(I32, (e, nb, LANES), 1) * LANES
           + lax.broadcasted_iota(I32, (e, nb, LANES), 2))
    nbits = (nb * LANES - 1).bit_length()

    def index_step(i, v):
        cand = v | (jnp.int32(1) << (nbits - 1 - i))
        return jnp.where(count(eq & (tok < cand)) < need, cand, v)

    last = lax.fori_loop(0, nbits, index_step, jnp.zeros((e, 1, 1), I32))
    sel = (gt | (eq & (tok <= last))).reshape(e * nb, LANES)
    self32 = sel.astype(F32)

    r = lax.broadcasted_iota(I32, (LANES, LANES), 0)
    c = lax.broadcasted_iota(I32, (LANES, LANES), 1)
    inc = _dot(self32.astype(BF16), (r <= c).astype(BF16))
    tot = jnp.broadcast_to(inc[:, LANES - 1:LANES], (e * nb, LANES)).astype(BF16)
    rb = lax.broadcasted_iota(I32, (nb, nb), 0)
    cb = lax.broadcasted_iota(I32, (nb, nb), 1)
    lower = (cb < rb).astype(BF16)
    before = jnp.concatenate([_dot(lower, tot[x * nb:(x + 1) * nb]) for x in range(e)], axis=0)
    pos = inc - self32 + before
    pos_ref[...] = jnp.where(sel, pos, -1.0).astype(I32)
    cum_ref[...] = before.astype(I32)


def _one_hot_mask(pos_ref, row_tile, rt):
    tt = pos_ref.shape[-1]
    rows = lax.broadcasted_iota(I32, (rt, tt), 0) + row_tile * rt
    return pos_ref[...] == rows


def _gather_kernel(gt, gv, gk, gf, hn_ref, pos_ref, aff_ref, xe_ref, gc_ref, *, rt, n_r):
    s = pl.program_id(0)
    tt = pos_ref.shape[-1]

    @pl.when(gf[s] != STEP_IDLE)
    def _():
        mask = _one_hot_mask(pos_ref, gk[s] % n_r, rt)
        tokens = hn_ref[pl.ds(pl.multiple_of(gt[s] * tt, tt), tt), :]
        rows = _dot(jnp.where(mask, 1.0, 0.0).astype(BF16), tokens)
        gate = jnp.sum(jnp.where(mask, aff_ref[...], 0.0), axis=1, keepdims=True)

        @pl.when(gf[s] & STEP_FIRST != 0)
        def _():
            xe_ref[...] = rows.astype(BF16)
            gc_ref[...] = gate

        @pl.when(gf[s] & STEP_FIRST == 0)
        def _():
            xe_ref[...] = (xe_ref[...].astype(F32) + rows).astype(BF16)
            gc_ref[...] += gate


def _ffn_kernel(xe_ref, wg_ref, wu_ref, wd_ref, gc_ref, ye_ref, *acc_refs):
    j = pl.program_id(2)
    n_chunks = len(acc_refs)
    rc = acc_refs[0].shape[0]

    @pl.when(j == 0)
    def _():
        for acc_ref in acc_refs:
            acc_ref[...] = jnp.zeros(acc_ref.shape, F32)

    wg = wg_ref[...].astype(BF16)
    wu = wu_ref[...].astype(BF16)
    wd = wd_ref[...].astype(BF16)

    def gate_up(c):
        x = xe_ref[c * rc:(c + 1) * rc, :]
        return _dot(x, wg), _dot(x, wu)

    ahead = gate_up(0)
    for c in range(n_chunks):
        g, u = ahead
        if c + 1 < n_chunks:
            ahead = gate_up(c + 1)
        hid = (g / (1.0 + jnp.exp(-g))) * u
        acc_refs[c][...] += _dot(hid.astype(BF16), wd)

    @pl.when(j == pl.num_programs(2) - 1)
    def _():
        for c, acc_ref in enumerate(acc_refs):
            rows = slice(c * rc, (c + 1) * rc)
            ye_ref[rows, :] = (acc_ref[...] * gc_ref[rows, :]).astype(BF16)


def _combine_kernel(gi, gv, gk, gf, h_ref, *refs, rt, n_r, group):
    pos_refs, ye_refs, o_ref = refs[:group], refs[group:2 * group], refs[2 * group]
    s = pl.program_id(0)

    @pl.when(gf[s] != STEP_IDLE)
    def _():
        masks = [_one_hot_mask(pos_refs[k], gi[s * group + k] % n_r, rt) & (gv[s * group + k] == 1)
                 for k in range(group)]
        onehot = jnp.where(jnp.concatenate(masks, axis=0), 1.0, 0.0).astype(BF16)
        ye = jnp.concatenate([ye_refs[k][...] for k in range(group)], axis=0)
        back = lax.dot_general(onehot, ye, (((0,), (0,)), ((), ())), preferred_element_type=F32)

        @pl.when(gf[s] & STEP_FIRST != 0)
        def _():
            o_ref[...] = h_ref[...] + back

        @pl.when(gf[s] & STEP_FIRST == 0)
        def _():
            o_ref[...] += back


def _schedule_kernel(lo_ref, *out_refs, n_e, n_r, n_t, rt, cap, groups):
    def emitter(item_ref, valid_ref, key_ref, flag_ref, group):
        for i_ref, fill in ((item_ref, 0), (valid_ref, 0), (key_ref, 0), (flag_ref, STEP_IDLE)):
            def clear(i, c, i_ref=i_ref, fill=fill):
                i_ref[i] = fill
                return c
            lax.fori_loop(0, i_ref.shape[0], clear, 0)

        def emit(state, key, item):
            step, cnt, cur = state
            fresh = key != cur
            step = step + (fresh & (cnt > 0)).astype(I32)
            cnt = jnp.where(fresh, 0, cnt)
            item_ref[step * group + cnt] = item
            valid_ref[step * group + cnt] = 1

            @pl.when(cnt == 0)
            def _():
                key_ref[step] = key
                flag_ref[step] = jnp.where(fresh, STEP_FIRST, STEP_NEXT)

            full = cnt + 1 == group
            return step + full.astype(I32), jnp.where(full, 0, cnt + 1), key

        def finish(state):
            step, cnt, cur = state

            def pad(i, c):
                key_ref[i] = cur
                return c
            lax.fori_loop(step + (cnt > 0).astype(I32), key_ref.shape[0], pad, 0)

        return emit, finish

    def rows_of(e, t):
        lo = lo_ref[e, t]
        hi = jnp.where(t + 1 < n_t, lo_ref[e, jnp.minimum(t + 1, n_t - 1)], cap)
        return lo, hi

    start = (jnp.int32(0), jnp.int32(0), jnp.int32(-1))
    emit_g, finish_g = emitter(*out_refs[:4], groups[0])

    def gather_tile(i, state):
        e, t = i // n_t, i % n_t
        lo, hi = rows_of(e, t)
        return lax.fori_loop(lo // rt, jnp.where(hi > lo, (hi - 1) // rt + 1, lo // rt),
                             lambda r, st: emit_g(st, e * n_r + r, t), state)

    finish_g(lax.fori_loop(0, n_e * n_t, gather_tile, start))

    emit_c, finish_c = emitter(*out_refs[4:], groups[1])

    def combine_tile(i, state):
        t, e = i // n_e, i % n_e
        lo, hi = rows_of(e, t)
        first = jnp.minimum(lo // rt, n_r - 1)
        stop = jnp.where(hi > lo, (hi - 1) // rt + 1, jnp.where(e == 0, first + 1, first))
        return lax.fori_loop(first, stop, lambda r, st: emit_c(st, t, e * n_r + r), state)

    finish_c(lax.fori_loop(0, n_t * n_e, combine_tile, start))


def _schedules(lo, cap, rt, groups):
    n_e, n_t = lo.shape
    n_r = cap // rt
    n_true = n_e * (n_r + n_t - 1)
    steps = ((n_true + (groups[0] - 1) * n_e * n_r) // groups[0],
             (n_true + n_t + (groups[1] - 1) * n_t) // groups[1])
    sizes = [steps[0] * groups[0]] * 2 + [steps[0]] * 2 + [steps[1] * groups[1]] * 2 + [steps[1]] * 2
    smem = pl.BlockSpec(memory_space=pltpu.SMEM)
    out = pl.pallas_call(
        functools.partial(_schedule_kernel, n_e=n_e, n_r=n_r, n_t=n_t, rt=rt, cap=cap, groups=groups),
        in_specs=[smem],
        out_specs=[smem] * 8,
        out_shape=[jax.ShapeDtypeStruct((k,), I32) for k in sizes],
    )(lo)
    return out[:4], out[4:]


def _moe_layer(h, g, w_router, w_gate, w_up, w_down, layer):
    n, d = h.shape
    e = N_EXPERTS
    cap = CAPACITY_FACTOR * n // e
    f = w_gate.shape[-1]
    nb = n // LANES
    tn = min(1024, n)

    hn, aff = pl.pallas_call(
        _router_kernel,
        grid=(n // tn,),
        in_specs=[pl.BlockSpec((tn, d), lambda i: (i, 0)),
                  pl.BlockSpec((1, d), lambda i: (0, 0)),
                  pl.BlockSpec((e, d), lambda i: (0, 0))],
        out_specs=[pl.BlockSpec((tn, d), lambda i: (i, 0)),
                   pl.BlockSpec((e, tn), lambda i: (0, i))],
        out_shape=[jax.ShapeDtypeStruct((n, d), BF16), jax.ShapeDtypeStruct((e, n), F32)],
        compiler_params=_cparams(("parallel",)),
    )(h, g.reshape(1, d), w_router.T)

    whole = pl.BlockSpec((e * nb, LANES), lambda i: (0, 0))
    pos, cum = pl.pallas_call(
        functools.partial(_select_kernel, e=e, nb=nb, k=cap),
        grid=(1,),
        in_specs=[whole],
        out_specs=[whole, whole],
        out_shape=[jax.ShapeDtypeStruct((e * nb, LANES), I32)] * 2,
        compiler_params=_cparams(("arbitrary",)),
    )(aff.reshape(e * nb, LANES))

    rt = min(256, cap)
    tt = tn
    n_r, n_t = cap // rt, n // tt
    lo = cum[:, 0].reshape(e, nb)[:, ::tt // LANES]
    (at, av, ak, af), (ci, cv, ck, cf) = _schedules(lo, cap, rt, (1, COMBINE_GROUP))

    pos3 = pos.reshape(e * n_t, 1, tt)
    aff3 = aff.reshape(e * n_t, 1, tt)
    row_spec = pl.BlockSpec((None, 1, tt), lambda s, at, av, ak, af: (ak[s] // n_r * n_t + at[s], 0, 0))
    xe, gc = pl.pallas_call(
        functools.partial(_gather_kernel, rt=rt, n_r=n_r),
        grid_spec=pltpu.PrefetchScalarGridSpec(
            num_scalar_prefetch=4,
            grid=(ak.shape[0],),
            in_specs=[pl.BlockSpec((n, d), lambda s, at, av, ak, af: (0, 0), pipeline_mode=pl.Buffered(1)),
                      row_spec, row_spec],
            out_specs=[pl.BlockSpec((None, rt, d), lambda s, at, av, ak, af: (ak[s] // n_r, ak[s] % n_r, 0)),
                       pl.BlockSpec((None, rt, 1), lambda s, at, av, ak, af: (ak[s] // n_r, ak[s] % n_r, 0))],
        ),
        out_shape=[jax.ShapeDtypeStruct((e, cap, d), BF16), jax.ShapeDtypeStruct((e, cap, 1), F32)],
        compiler_params=_cparams(("arbitrary",)),
    )(at, av, ak, af, hn, pos3, aff3)

    tm = min(2048, cap)
    tf = 256
    ye = pl.pallas_call(
        _ffn_kernel,
        grid=(e, cap // tm, f // tf),
        in_specs=[pl.BlockSpec((None, tm, d), lambda x, r, j: (x, r, 0)),
                  pl.BlockSpec((None, None, d, tf), lambda x, r, j: (layer, x, 0, j)),
                  pl.BlockSpec((None, None, d, tf), lambda x, r, j: (layer, x, 0, j)),
                  pl.BlockSpec((None, None, tf, d), lambda x, r, j: (layer, x, j, 0)),
                  pl.BlockSpec((None, tm, 1), lambda x, r, j: (x, r, 0))],
        out_specs=pl.BlockSpec((None, tm, d), lambda x, r, j: (x, r, 0)),
        out_shape=jax.ShapeDtypeStruct((e, cap, d), BF16),
        scratch_shapes=[pltpu.VMEM((min(FFN_ROW_CHUNK, tm), d), F32)] * (tm // min(FFN_ROW_CHUNK, tm)),
        compiler_params=_cparams(("parallel", "parallel", "arbitrary")),
    )(xe, w_gate, w_up, w_down, gc)

    grp = COMBINE_GROUP
    pos_spec = lambda k: pl.BlockSpec(
        (None, 1, tt), lambda s, ci, cv, ck, cf: (ci[s * grp + k] // n_r * n_t + ck[s], 0, 0))
    ye_spec = lambda k: pl.BlockSpec(
        (None, rt, d), lambda s, ci, cv, ck, cf: (ci[s * grp + k] // n_r, ci[s * grp + k] % n_r, 0))
    tile_spec = pl.BlockSpec((tt, d), lambda s, ci, cv, ck, cf: (ck[s], 0))
    return pl.pallas_call(
        functools.partial(_combine_kernel, rt=rt, n_r=n_r, group=grp),
        grid_spec=pltpu.PrefetchScalarGridSpec(
            num_scalar_prefetch=4,
            grid=(ck.shape[0],),
            in_specs=[tile_spec] + [pos_spec(k) for k in range(grp)] + [ye_spec(k) for k in range(grp)],
            out_specs=tile_spec,
        ),
        out_shape=jax.ShapeDtypeStruct((n, d), F32),
        compiler_params=_cparams(("arbitrary",)),
    )(ci, cv, ck, cf, h, *([pos3] * grp), *([ye] * grp))


def _t5_bucket_np(rel):
    nb = N_BUCKETS // 2
    max_exact = nb // 2
    ret = np.where(rel > 0, nb, 0)
    n = np.abs(rel)
    nf = np.maximum(n, 1).astype(np.float64)
    large = max_exact + (np.log(nf / max_exact) / math.log(MAX_DISTANCE / max_exact)
                         * (nb - max_exact)).astype(np.int64)
    large = np.minimum(large, nb - 1)
    return ret + np.where(n < max_exact, n, large)


def _far_distance():
    n = np.arange(1, 4 * MAX_DISTANCE)
    last = _t5_bucket_np(-n)
    return int(n[np.nonzero(last != last[-1])[0][-1]] + 1)


FAR = _far_distance()


SUM_LIMIT = 1e30
LOOKAHEAD = 2
N_BIAS_TILES = 5
assert FAR <= LANES + 1


def _bias_tiles_kernel(near_ref, far_ref, t_ref):
    h = pl.program_id(0)
    d = lax.broadcasted_iota(I32, (LANES, LANES), 0) - lax.broadcasted_iota(I32, (LANES, LANES), 1)
    for idx in range(N_BIAS_TILES):
        rel = d + LANES * (idx - 2)
        tile = jnp.where(rel < 0, far_ref[h, 0], far_ref[h, 1])
        if abs(idx - 2) <= 1:
            tile = lax.fori_loop(
                0, 2 * FAR - 1, lambda r, t: jnp.where(rel == r - (FAR - 1), near_ref[h, r], t), tile)
        t_ref[idx] = tile


def _bias_tiles(rel_bias):
    nh = rel_bias.shape[1]
    near = (rel_bias[_t5_bucket_np(np.arange(-(FAR - 1), FAR))] * LOG2E).T
    far = jnp.stack([rel_bias[N_BUCKETS // 2 - 1], rel_bias[N_BUCKETS - 1]], axis=1) * LOG2E
    smem = pl.BlockSpec(memory_space=pltpu.SMEM)
    tiles = pl.pallas_call(
        _bias_tiles_kernel,
        grid=(nh,),
        in_specs=[smem, smem],
        out_specs=pl.BlockSpec((None, N_BIAS_TILES, LANES, LANES), lambda h: (h, 0, 0, 0)),
        out_shape=jax.ShapeDtypeStruct((nh, N_BIAS_TILES, LANES, LANES), F32),
        compiler_params=_cparams(("arbitrary",)),
    )(near, far)
    return tiles, far


def _qkv_kernel(h_ref, g_ref, w_ref, bd_ref, gq_ref, gk_ref, qt_ref, k_ref, vt_ref, *, d, nh):
    hn = _rms(h_ref[...], g_ref[...]).astype(BF16)
    qkv = _dot(hn, w_ref[...])
    q, k, v = qkv[:, :d], qkv[:, d:2 * d], qkv[:, 2 * d:]

    def head_norm(x, gain):
        ms = _dot((x * x).astype(BF16), bd_ref[...])
        return x * lax.rsqrt(ms + EPS) * gain

    qn = head_norm(q, gq_ref[...]) * (HEAD_DIM ** -0.5 * LOG2E)
    k_ref[...] = head_norm(k, gk_ref[...]).astype(BF16)
    for h in range(nh):
        cols = slice(h * HEAD_BLOCK, (h + 1) * HEAD_BLOCK)
        qt_ref[h] = qn[:, cols].T.astype(BF16)
        vt_ref[h] = v[:, cols].T.astype(BF16)


def _attn_kernel(far_ref, qt_ref, k_ref, vt_ref, tiles_ref, lq1_ref, lk1_ref, lq2_ref, lk2_ref, gs_ref, o_ref,
                 m_ref, l_ref, acc_ref, redo_ref, *, tq, tc, n_chunks, lambda_init):
    h = pl.program_id(1)
    i = pl.program_id(2)
    j = pl.program_id(3)

    qt = qt_ref[...]
    row = lax.broadcasted_iota(I32, qt.shape, 0)
    comps = (jnp.where(row < HEAD_DIM, qt, jnp.zeros_like(qt)),
             jnp.where(row >= HEAD_DIM, qt, jnp.zeros_like(qt)))
    q0 = i * tq
    k_first = j * n_chunks * tc
    ones = jnp.ones((ONES_ROWS, tc), BF16)

    def far_side(k0):
        before = k0 + tc - 1 - q0 <= -FAR
        after = k0 - (q0 + tq - 1) >= FAR
        return before, after

    def bias_rows(k0, n_rows):
        blk = (k0 - q0) // LANES
        mid = N_BIAS_TILES // 2
        take = min(n_rows, LANES)
        return jnp.concatenate(
            [jnp.concatenate([tiles_ref[jnp.clip(blk + a - b, -mid, mid) + mid][:take]
                              for b in range(tq // LANES)], axis=1)
             for a in range(max(n_rows // LANES, 1))], axis=0)

    @pl.when(j == 0)
    def _():
        head = bias_rows(0, ONES_ROWS)
        for c in range(2):
            m_ref[c] = jnp.max(_dot(k_ref[:ONES_ROWS, :], comps[c]) + head, axis=0, keepdims=True)
        l_ref[...] = jnp.zeros(l_ref.shape, F32)
        acc_ref[...] = jnp.zeros(acc_ref.shape, F32)

    def scores(m, with_band):
        jj, c = m // 2, m % 2
        s = _dot(k_ref[jj * tc:(jj + 1) * tc, :], comps[c])
        return s + bias_rows(k_first + jj * tc, tc) if with_band else s

    def run(with_band, exact):
        state = [(m_ref[c], l_ref[c], acc_ref[c]) for c in range(2)]
        n_maps = 2 * n_chunks
        ahead = [scores(m, with_band) for m in range(min(LOOKAHEAD, n_maps))]
        for m in range(n_maps):
            jj, c = m // 2, m % 2
            s = ahead.pop(0)
            if m + LOOKAHEAD < n_maps:
                ahead.append(scores(m + LOOKAHEAD, with_band))
            rows = slice(jj * tc, (jj + 1) * tc)
            if with_band:
                const = 0.0
            else:
                before, _ = far_side(k_first + jj * tc)
                const = jnp.where(before, far_ref[h, 0], far_ref[h, 1])
            vt1 = jnp.concatenate([vt_ref[:, rows], ones], axis=0)
            m_old, l_old, acc_old = state[c]
            if exact:
                m_new = jnp.maximum(m_old, jnp.max(s, axis=0, keepdims=True) + const)
                alpha = jnp.exp2(m_old - m_new)
                pv = _dot(vt1, jnp.exp2(s - (m_new - const)).astype(BF16))
                state[c] = (m_new, alpha * l_old + pv[HEAD_BLOCK:HEAD_BLOCK + 1],
                            alpha * acc_old + pv[:HEAD_BLOCK])
            else:
                pv = _dot(vt1, jnp.exp2(s - (m_old - const)).astype(BF16))
                state[c] = (m_old, l_old + pv[HEAD_BLOCK:HEAD_BLOCK + 1], acc_old + pv[:HEAD_BLOCK])
        return state

    def commit(state):
        for c in range(2):
            m_ref[c], l_ref[c], acc_ref[c] = state[c]

    def streamed(with_band):
        state = run(with_band, exact=False)
        worst = [jnp.maximum(jnp.max(jnp.where(l < SUM_LIMIT, 0.0, 1.0)),
                             jnp.max(jnp.where(jnp.abs(acc) < SUM_LIMIT, 0.0, 1.0))) for _, l, acc in state]
        safe = jnp.maximum(worst[0], worst[1]) == 0.0
        redo_ref[0] = jnp.where(safe, 0, 1)

        @pl.when(safe)
        def _():
            commit(state)

    last_before, _ = far_side(k_first + (n_chunks - 1) * tc)
    _, first_after = far_side(k_first)
    all_far = last_before | first_after

    @pl.when(all_far)
    def _():
        streamed(False)

    @pl.when(jnp.logical_not(all_far))
    def _():
        streamed(True)

    @pl.when(redo_ref[0] == 1)
    def _():
        commit(run(True, exact=True))

    @pl.when(j == pl.num_programs(3) - 1)
    def _():
        lam = (jnp.exp(jnp.sum(lq1_ref[...] * lk1_ref[...], keepdims=True))
               - jnp.exp(jnp.sum(lq2_ref[...] * lk2_ref[...], keepdims=True)) + lambda_init)
        a = acc_ref[0] / l_ref[0] - lam * (acc_ref[1] / l_ref[1])
        a = a * lax.rsqrt(jnp.mean(a * a, axis=0, keepdims=True) + EPS) * gs_ref[...]
        o_ref[...] = (a * (1.0 - lambda_init)).T.astype(BF16)


def _proj_kernel(o_ref, w_ref, h_ref, out_ref):
    out_ref[...] = h_ref[...] + _dot(o_ref[...], w_ref[...])


def _attn_tiles(s):
    tq = min(1024, s)
    tc = min(256, s)
    assert tq % LANES == 0 and tc % LANES == 0
    return tq, tc


def _attention_layer(h, g, w_qkv, w_o, g_q, g_k, lq1, lk1, lq2, lk2, g_sub, rel_bias, lambda_init):
    b, s, d = h.shape
    nh = d // HEAD_BLOCK
    tn = min(512, s)
    hd = np.arange(d) // HEAD_DIM
    bd = jnp.asarray((hd[:, None] == hd[None, :]) / HEAD_DIM, BF16)
    tile_gain = lambda x: jnp.tile(x, d // HEAD_DIM).reshape(1, d)
    const2 = lambda shape: pl.BlockSpec(shape, lambda bi, i: (0,) * len(shape))
    qt, k, vt = pl.pallas_call(
        functools.partial(_qkv_kernel, d=d, nh=nh),
        grid=(b, s // tn),
        in_specs=[pl.BlockSpec((None, tn, d), lambda bi, i: (bi, i, 0)),
                  const2((1, d)), const2((d, 3 * d)), const2((d, d)), const2((1, d)), const2((1, d))],
        out_specs=[pl.BlockSpec((None, nh, HEAD_BLOCK, tn), lambda bi, i: (bi, 0, 0, i)),
                   pl.BlockSpec((None, tn, d), lambda bi, i: (bi, i, 0)),
                   pl.BlockSpec((None, nh, HEAD_BLOCK, tn), lambda bi, i: (bi, 0, 0, i))],
        out_shape=[jax.ShapeDtypeStruct((b, nh, HEAD_BLOCK, s), BF16),
                   jax.ShapeDtypeStruct((b, s, d), BF16),
                   jax.ShapeDtypeStruct((b, nh, HEAD_BLOCK, s), BF16)],
        compiler_params=_cparams(("parallel", "parallel")),
    )(h, g.reshape(1, d), w_qkv.astype(BF16), bd, tile_gain(g_q), tile_gain(g_k))

    tq, tc = _attn_tiles(s)
    tko = min(4096, s)
    n_chunks = tko // tc
    tiles, far = _bias_tiles(rel_bias)
    const4 = lambda shape: pl.BlockSpec(shape, lambda bi, hi, i, j: (0,) * len(shape))
    lam_spec = const4((1, HEAD_DIM))
    o = pl.pallas_call(
        functools.partial(_attn_kernel, tq=tq, tc=tc, n_chunks=n_chunks, lambda_init=lambda_init),
        grid=(b, nh, s // tq, s // tko),
        in_specs=[pl.BlockSpec(memory_space=pltpu.SMEM),
                  pl.BlockSpec((None, None, HEAD_BLOCK, tq), lambda bi, hi, i, j: (bi, hi, 0, i)),
                  pl.BlockSpec((None, tko, HEAD_BLOCK), lambda bi, hi, i, j: (bi, j, hi)),
                  pl.BlockSpec((None, None, HEAD_BLOCK, tko), lambda bi, hi, i, j: (bi, hi, 0, j)),
                  pl.BlockSpec((None, N_BIAS_TILES, LANES, LANES), lambda bi, hi, i, j: (hi, 0, 0, 0)),
                  lam_spec, lam_spec, lam_spec, lam_spec,
                  const4((HEAD_BLOCK, 1))],
        out_specs=pl.BlockSpec((None, tq, HEAD_BLOCK), lambda bi, hi, i, j: (bi, i, hi)),
        out_shape=jax.ShapeDtypeStruct((b, s, d), BF16),
        scratch_shapes=[pltpu.VMEM((2, 1, tq), F32), pltpu.VMEM((2, 1, tq), F32),
                        pltpu.VMEM((2, HEAD_BLOCK, tq), F32), pltpu.SMEM((1,), I32)],
        compiler_params=_cparams(("parallel", "parallel", "parallel", "arbitrary")),
    )(far, qt, k, vt, tiles, lq1.reshape(1, -1), lk1.reshape(1, -1), lq2.reshape(1, -1), lk2.reshape(1, -1),
      g_sub.reshape(HEAD_BLOCK, 1))

    n = b * s
    tp = min(1024, n)
    out = pl.pallas_call(
        _proj_kernel,
        grid=(n // tp,),
        in_specs=[pl.BlockSpec((tp, d), lambda i: (i, 0)),
                  pl.BlockSpec((d, d), lambda i: (0, 0)),
                  pl.BlockSpec((tp, d), lambda i: (i, 0))],
        out_specs=pl.BlockSpec((tp, d), lambda i: (i, 0)),
        out_shape=jax.ShapeDtypeStruct((n, d), F32),
        compiler_params=_cparams(("parallel",)),
    )(o.reshape(n, d), w_o.astype(BF16), h.reshape(n, d))
    return out.reshape(b, s, d)


def _trunk(x, rel_bias, wa, wb, g_mix, g_ffn, w_qkv, w_attn_out, g_q, g_k, lq1, lk1, lq2, lk2, g_sub,
           w_router, w_gate, w_up, w_down):
    b, s, d = x.shape
    moe = lambda h, i: _moe_layer(h.reshape(b * s, d), g_ffn[i], w_router[i], w_gate, w_up, w_down,
                                  i).reshape(b, s, d)
    h = _fourier_layer(x, g_mix[0], wa, wb)
    h = moe(h, 0)
    lambda_init = 0.8 - 0.6 * math.exp(-0.3 * 1)
    h = _attention_layer(h, g_mix[1], w_qkv[0], w_attn_out[0], g_q[0], g_k[0], lq1[0], lk1[0], lq2[0],
                         lk2[0], g_sub[0], rel_bias, lambda_init)
    return moe(h, 1)


def kernel(x_prompt, x_sample, rel_bias, g_mix, g_ffn, w_fourier_out, w_qkv, w_attn_out, g_q, g_k,
           lambda_q1, lambda_k1, lambda_q2, lambda_k2, g_sub, w_router, w_gate, w_up, w_down):
    assert g_mix.shape[0] == 2, "one Fourier layer followed by one attention layer"
    wa, wb = _fold_w(w_fourier_out[0])
    run = lambda x: _trunk(x, rel_bias, wa, wb, g_mix, g_ffn, w_qkv, w_attn_out, g_q, g_k, lambda_q1, lambda_k1,
                           lambda_q2, lambda_k2, g_sub, w_router, w_gate, w_up, w_down)
    return (run(x_prompt), run(x_sample))
```

```python
import functools
import math

import numpy as np
import jax
import jax.numpy as jnp
from jax import lax
from jax.experimental import pallas as pl
from jax.experimental.pallas import tpu as pltpu

EPS = 1e-6
F32 = jnp.float32
BF16 = jnp.bfloat16
I32 = jnp.int32
HIGHEST = lax.Precision.HIGHEST
LOG2E = math.log2(math.e)

LANES = 128
SUBLANES = 8
FOURIER_GROUP_DIM = 128
HEAD_BLOCK = 128
HEAD_DIM = 64
ONES_ROWS = 16
N_EXPERTS = 16
CAPACITY_FACTOR = 2
FFN_ROW_CHUNK = 512
COMBINE_GROUP = 8
STEP_NEXT, STEP_FIRST, STEP_IDLE = 0, 1, 2
N_BUCKETS = 32
MAX_DISTANCE = 128
VMEM_LIMIT = 56 * 1024 * 1024


def _cparams(sem):
    return pltpu.CompilerParams(dimension_semantics=sem, vmem_limit_bytes=VMEM_LIMIT)


def _rms(x, g):
    return x * lax.rsqrt(jnp.mean(x * x, axis=-1, keepdims=True) + EPS) * g


def _dot(a, b):
    return jnp.dot(a, b, preferred_element_type=F32)


def _cos_sin(n):
    k = np.arange(n)
    ang = 2.0 * np.pi * ((k[:, None] * k[None, :]) % n) / n
    return np.cos(ang), np.sin(ang)


def _fold_w_kernel(cc_ref, sc_ref, w_ref, wa_ref, wb_ref):
    w = w_ref[...]
    wa_ref[...] = jnp.dot(cc_ref[...], w, precision=HIGHEST, preferred_element_type=F32).astype(BF16)
    wb_ref[...] = jnp.dot(sc_ref[...], w, precision=HIGHEST, preferred_element_type=F32).astype(BF16)


def _fold_w(w):
    d = w.shape[0]
    gd = FOURIER_GROUP_DIM
    c, s = _cos_sin(gd)
    cc = jnp.asarray(c / math.sqrt(gd), F32)
    sc = jnp.asarray(s / math.sqrt(gd), F32)
    const = pl.BlockSpec((gd, gd), lambda g: (0, 0))
    blk = pl.BlockSpec((gd, d), lambda g: (g, 0))
    return pl.pallas_call(
        _fold_w_kernel,
        grid=(d // gd,),
        in_specs=[const, const, blk],
        out_specs=[blk, blk],
        out_shape=[jax.ShapeDtypeStruct((d, d), BF16)] * 2,
        compiler_params=_cparams(("arbitrary",)),
    )(cc, sc, w)


def _fourier_a_kernel(x_ref, g_ref, f1_ref, twr_ref, twi_ref, yr_ref, yi_ref, *, nb, n1, d):
    j = pl.program_id(1)
    lane = lax.broadcasted_iota(I32, twr_ref.shape, 1)
    for c in range(nb):
        xn = _rms(x_ref[:, c, :], g_ref[...]).astype(BF16)
        y = _dot(f1_ref[...], xn)
        yr, yi = y[:n1], y[n1:]
        sel = lane == (j * nb + c)
        tr = jnp.sum(jnp.where(sel, twr_ref[...], 0.0), axis=1, keepdims=True)
        ti = jnp.sum(jnp.where(sel, twi_ref[...], 0.0), axis=1, keepdims=True)
        yr_ref[:, c, :] = (yr * tr - yi * ti).astype(BF16)
        yi_ref[:, c, :] = (yr * ti + yi * tr).astype(BF16)


def _fourier_b_kernel(yr_ref, yi_ref, f2_ref, wa_ref, wb_ref, x_ref, o_ref, *, kb, n2):
    ars, ais = [], []
    for c in range(kb):
        rows = slice(c * n2, (c + 1) * n2)
        ys = jnp.concatenate([yr_ref[rows, :], yi_ref[rows, :]], axis=0)
        a = _dot(f2_ref[...], ys)
        ars.append(a[:n2].astype(BF16))
        ais.append(a[n2:].astype(BF16))
    ar = jnp.concatenate(ars, axis=0)
    ai = jnp.concatenate(ais, axis=0)
    out = _dot(ar, wa_ref[...]) + _dot(ai, wb_ref[...])
    for c in range(kb):
        o_ref[:, c, :] = x_ref[:, c, :] + out[c * n2:(c + 1) * n2]


def _split_len(s):
    n1 = 1 << (int(math.log2(s)) // 2)
    assert s % n1 == 0
    return n1, s // n1


def _fourier_layer(x, g, wa, wb):
    b, s, d = x.shape
    n1, n2 = _split_len(s)
    c1, s1 = _cos_sin(n1)
    c2, s2 = _cos_sin(n2)
    f1 = jnp.asarray(np.concatenate([c1, -s1], axis=0) / math.sqrt(n1), BF16)
    f2 = jnp.asarray(np.block([[c2, s2], [-s2, c2]]) / math.sqrt(n2), BF16)
    ang = 2.0 * np.pi * ((np.arange(n1)[:, None] * np.arange(n2)[None, :]) % s) / s
    twr = jnp.asarray(np.cos(ang), F32)
    twi = jnp.asarray(-np.sin(ang), F32)

    nb = 2 * SUBLANES
    blk_a = pl.BlockSpec((None, n1, nb, d), lambda bi, j: (bi, 0, j, 0))
    full = lambda shape: pl.BlockSpec(shape, lambda bi, j: (0,) * len(shape))
    yr, yi = pl.pallas_call(
        functools.partial(_fourier_a_kernel, nb=nb, n1=n1, d=d),
        grid=(b, n2 // nb),
        in_specs=[blk_a, full((1, d)), full((2 * n1, n1)), full((n1, n2)), full((n1, n2))],
        out_specs=[blk_a, blk_a],
        out_shape=[jax.ShapeDtypeStruct((b, n1, n2, d), BF16)] * 2,
        compiler_params=_cparams(("parallel", "parallel")),
    )(x.reshape(b, n1, n2, d), g.reshape(1, d), f1, twr, twi)

    kb = SUBLANES
    blk_y = pl.BlockSpec((None, kb * n2, d), lambda bi, i: (bi, i, 0))
    blk_x = pl.BlockSpec((None, n2, kb, d), lambda bi, i: (bi, 0, i, 0))
    out = pl.pallas_call(
        functools.partial(_fourier_b_kernel, kb=kb, n2=n2),
        grid=(b, n1 // kb),
        in_specs=[blk_y, blk_y, full((2 * n2, 2 * n2)), full((d, d)), full((d, d)), blk_x],
        out_specs=blk_x,
        out_shape=jax.ShapeDtypeStruct((b, n2, n1, d), F32),
        compiler_params=_cparams(("parallel", "parallel")),
    )(yr.reshape(b, s, d), yi.reshape(b, s, d), f2, wa, wb, x.reshape(b, n2, n1, d))
    return out.reshape(b, s, d)


def _router_kernel(h_ref, g_ref, wrt_ref, hn_ref, aff_ref):
    hn = _rms(h_ref[...], g_ref[...])
    hn_ref[...] = hn.astype(BF16)
    logits = lax.dot_general(wrt_ref[...], hn, (((1,), (1,)), ((), ())),
                             precision=HIGHEST, preferred_element_type=F32)
    ex = jnp.exp(logits - jnp.max(logits, axis=0, keepdims=True))
    aff_ref[...] = ex / jnp.sum(ex, axis=0, keepdims=True)


def _select_kernel(a_ref, pos_ref, cum_ref, *, e, nb, k):
    keys = lax.bitcast_convert_type(a_ref[...], I32).reshape(e, nb, LANES)

    def count(mask):
        part = jnp.sum(mask.astype(F32), axis=2, keepdims=True)
        return jnp.sum(part, axis=1, keepdims=True)

    def value_step(i, t):
        cand = t | (jnp.int32(1) << (30 - i))
        return jnp.where(count(keys >= cand) >= k, cand, t)

    thr = lax.fori_loop(0, 31, value_step, jnp.zeros((e, 1, 1), I32))
    gt = keys > thr
    eq = keys == thr
    need = k - count(gt)
    tok = (lax.broadcasted_iota(I32, (e, nb, LANES), 1) * LANES
           + lax.broadcasted_iota(I32, (e, nb, LANES), 2))
    nbits = (nb * LANES - 1).bit_length()

    def index_step(i, v):
        cand = v | (jnp.int32(1) << (nbits - 1 - i))
        return jnp.where(count(eq & (tok < cand)) < need, cand, v)

    last = lax.fori_loop(0, nbits, index_step, jnp.zeros((e, 1, 1), I32))
    sel = (gt | (eq & (tok <= last))).reshape(e * nb, LANES)
    self32 = sel.astype(F32)

    r = lax.broadcasted_iota(I32, (LANES, LANES), 0)
    c = lax.broadcasted_iota(I32, (LANES, LANES), 1)
    inc = _dot(self32.astype(BF16), (r <= c).astype(BF16))
    tot = jnp.broadcast_to(inc[:, LANES - 1:LANES], (e * nb, LANES)).astype(BF16)
    rb = lax.broadcasted_iota(I32, (nb, nb), 0)
    cb = lax.broadcasted_iota(I32, (nb, nb), 1)
    lower = (cb < rb).astype(BF16)
    before = jnp.concatenate([_dot(lower, tot[x * nb:(x + 1) * nb]) for x in range(e)], axis=0)
    pos = inc - self32 + before
    pos_ref[...] = jnp.where(sel, pos, -1.0).astype(I32)
    cum_ref[...] = before.astype(I32)


def _one_hot_mask(pos_ref, row_tile, rt):
    tt = pos_ref.shape[-1]
    rows = lax.broadcasted_iota(I32, (rt, tt), 0) + row_tile * rt
    return pos_ref[...] == rows


def _gather_kernel(gt, gv, gk, gf, hn_ref, pos_ref, aff_ref, xe_ref, gc_ref, *, rt, n_r):
    s = pl.program_id(0)
    tt = pos_ref.shape[-1]

    @pl.when(gf[s] != STEP_IDLE)
    def _():
        mask = _one_hot_mask(pos_ref, gk[s] % n_r, rt)
        tokens = hn_ref[pl.ds(pl.multiple_of(gt[s] * tt, tt), tt), :]
        rows = _dot(jnp.where(mask, 1.0, 0.0).astype(BF16), tokens)
        gate = jnp.sum(jnp.where(mask, aff_ref[...], 0.0), axis=1, keepdims=True)

        @pl.when(gf[s] & STEP_FIRST != 0)
        def _():
            xe_ref[...] = rows.astype(BF16)
            gc_ref[...] = gate

        @pl.when(gf[s] & STEP_FIRST == 0)
        def _():
            xe_ref[...] = (xe_ref[...].astype(F32) + rows).astype(BF16)
            gc_ref[...] += gate


def _ffn_kernel(xe_ref, wg_ref, wu_ref, wd_ref, gc_ref, ye_ref, *acc_refs):
    j = pl.program_id(2)
    n_chunks = len(acc_refs)
    rc = acc_refs[0].shape[0]

    @pl.when(j == 0)
    def _():
        for acc_ref in acc_refs:
            acc_ref[...] = jnp.zeros(acc_ref.shape, F32)

    wg = wg_ref[...].astype(BF16)
    wu = wu_ref[...].astype(BF16)
    wd = wd_ref[...].astype(BF16)

    def gate_up(c):
        x = xe_ref[c * rc:(c + 1) * rc, :]
        return _dot(x, wg), _dot(x, wu)

    ahead = gate_up(0)
    for c in range(n_chunks):
        g, u = ahead
        if c + 1 < n_chunks:
            ahead = gate_up(c + 1)
        hid = (g / (1.0 + jnp.exp(-g))) * u
        acc_refs[c][...] += _dot(hid.astype(BF16), wd)

    @pl.when(j == pl.num_programs(2) - 1)
    def _():
        for c, acc_ref in enumerate(acc_refs):
            rows = slice(c * rc, (c + 1) * rc)
            ye_ref[rows, :] = (acc_ref[...] * gc_ref[rows, :]).astype(BF16)


def _combine_kernel(gi, gv, gk, gf, h_ref, *refs, rt, n_r, group):
    pos_refs, ye_refs, o_ref = refs[:group], refs[group:2 * group], refs[2 * group]
    s = pl.program_id(0)

    @pl.when(gf[s] != STEP_IDLE)
    def _():
        masks = [_one_hot_mask(pos_refs[k], gi[s * group + k] % n_r, rt) & (gv[s * group + k] == 1)
                 for k in range(group)]
        onehot = jnp.where(jnp.concatenate(masks, axis=0), 1.0, 0.0).astype(BF16)
        ye = jnp.concatenate([ye_refs[k][...] for k in range(group)], axis=0)
        back = lax.dot_general(onehot, ye, (((0,), (0,)), ((), ())), preferred_element_type=F32)

        @pl.when(gf[s] & STEP_FIRST != 0)
        def _():
            o_ref[...] = h_ref[...] + back

        @pl.when(gf[s] & STEP_FIRST == 0)
        def _():
            o_ref[...] += back


def _schedule_kernel(lo_ref, *out_refs, n_e, n_r, n_t, rt, cap, groups):
    def emitter(item_ref, valid_ref, key_ref, flag_ref, group):
        for i_ref, fill in ((item_ref, 0), (valid_ref, 0), (key_ref, 0), (flag_ref, STEP_IDLE)):
            def clear(i, c, i_ref=i_ref, fill=fill):
                i_ref[i] = fill
                return c
            lax.fori_loop(0, i_ref.shape[0], clear, 0)

        def emit(state, key, item):
            step, cnt, cur = state
            fresh = key != cur
            step = step + (fresh & (cnt > 0)).astype(I32)
            cnt = jnp.where(fresh, 0, cnt)
            item_ref[step * group + cnt] = item
            valid_ref[step * group + cnt] = 1

            @pl.when(cnt == 0)
            def _():
                key_ref[step] = key
                flag_ref[step] = jnp.where(fresh, STEP_FIRST, STEP_NEXT)

            full = cnt + 1 == group
            return step + full.astype(I32), jnp.where(full, 0, cnt + 1), key

        def finish(state):
            step, cnt, cur = state

            def pad(i, c):
                key_ref[i] = cur
                return c
            lax.fori_loop(step + (cnt > 0).astype(I32), key_ref.shape[0], pad, 0)

        return emit, finish

    def rows_of(e, t):
        lo = lo_ref[e, t]
        hi = jnp.where(t + 1 < n_t, lo_ref[e, jnp.minimum(t + 1, n_t - 1)], cap)
        return lo, hi

    start = (jnp.int32(0), jnp.int32(0), jnp.int32(-1))
    emit_g, finish_g = emitter(*out_refs[:4], groups[0])

    def gather_tile(i, state):
        e, t = i // n_t, i % n_t
        lo, hi = rows_of(e, t)
        return lax.fori_loop(lo // rt, jnp.where(hi > lo, (hi - 1) // rt + 1, lo // rt),
                             lambda r, st: emit_g(st, e * n_r + r, t), state)

    finish_g(lax.fori_loop(0, n_e * n_t, gather_tile, start))

    emit_c, finish_c = emitter(*out_refs[4:], groups[1])

    def combine_tile(i, state):
        t, e = i // n_e, i % n_e
        lo, hi = rows_of(e, t)
        first = jnp.minimum(lo // rt, n_r - 1)
        stop = jnp.where(hi > lo, (hi - 1) // rt + 1, jnp.where(e == 0, first + 1, first))
        return lax.fori_loop(first, stop, lambda r, st: emit_c(st, t, e * n_r + r), state)

    finish_c(lax.fori_loop(0, n_t * n_e, combine_tile, start))


def _schedules(lo, cap, rt, groups):
    n_e, n_t = lo.shape
    n_r = cap // rt
    n_true = n_e * (n_r + n_t - 1)
    steps = ((n_true + (groups[0] - 1) * n_e * n_r) // groups[0],
             (n_true + n_t + (groups[1] - 1) * n_t) // groups[1])
    sizes = [steps[0] * groups[0]] * 2 + [steps[0]] * 2 + [steps[1] * groups[1]] * 2 + [steps[1]] * 2
    smem = pl.BlockSpec(memory_space=pltpu.SMEM)
    out = pl.pallas_call(
        functools.partial(_schedule_kernel, n_e=n_e, n_r=n_r, n_t=n_t, rt=rt, cap=cap, groups=groups),
        in_specs=[smem],
        out_specs=[smem] * 8,
        out_shape=[jax.ShapeDtypeStruct((k,), I32) for k in sizes],
    )(lo)
    return out[:4], out[4:]


def _moe_layer(h, g, w_router, w_gate, w_up, w_down, layer):
    n, d = h.shape
    e = N_EXPERTS
    cap = CAPACITY_FACTOR * n // e
    f = w_gate.shape[-1]
    nb = n // LANES
    tn = min(1024, n)

    hn, aff = pl.pallas_call(
        _router_kernel,
        grid=(n // tn,),
        in_specs=[pl.BlockSpec((tn, d), lambda i: (i, 0)),
                  pl.BlockSpec((1, d), lambda i: (0, 0)),
                  pl.BlockSpec((e, d), lambda i: (0, 0))],
        out_specs=[pl.BlockSpec((tn, d), lambda i: (i, 0)),
                   pl.BlockSpec((e, tn), lambda i: (0, i))],
        out_shape=[jax.ShapeDtypeStruct((n, d), BF16), jax.ShapeDtypeStruct((e, n), F32)],
        compiler_params=_cparams(("parallel",)),
    )(h, g.reshape(1, d), w_router.T)

    whole = pl.BlockSpec((e * nb, LANES), lambda i: (0, 0))
    pos, cum = pl.pallas_call(
        functools.partial(_select_kernel, e=e, nb=nb, k=cap),
        grid=(1,),
        in_specs=[whole],
        out_specs=[whole, whole],
        out_shape=[jax.ShapeDtypeStruct((e * nb, LANES), I32)] * 2,
        compiler_params=_cparams(("arbitrary",)),
    )(aff.reshape(e * nb, LANES))

    rt = min(256, cap)
    tt = tn
    n_r, n_t = cap // rt, n // tt
    lo = cum[:, 0].reshape(e, nb)[:, ::tt // LANES]
    (at, av, ak, af), (ci, cv, ck, cf) = _schedules(lo, cap, rt, (1, COMBINE_GROUP))

    pos3 = pos.reshape(e * n_t, 1, tt)
    aff3 = aff.reshape(e * n_t, 1, tt)
    row_spec = pl.BlockSpec((None, 1, tt), lambda s, at, av, ak, af: (ak[s] // n_r * n_t + at[s], 0, 0))
    xe, gc = pl.pallas_call(
        functools.partial(_gather_kernel, rt=rt, n_r=n_r),
        grid_spec=pltpu.PrefetchScalarGridSpec(
            num_scalar_prefetch=4,
            grid=(ak.shape[0],),
            in_specs=[pl.BlockSpec((n, d), lambda s, at, av, ak, af: (0, 0), pipeline_mode=pl.Buffered(1)),
                      row_spec, row_spec],
            out_specs=[pl.BlockSpec((None, rt, d), lambda s, at, av, ak, af: (ak[s] // n_r, ak[s] % n_r, 0)),
                       pl.BlockSpec((None, rt, 1), lambda s, at, av, ak, af: (ak[s] // n_r, ak[s] % n_r, 0))],
        ),
        out_shape=[jax.ShapeDtypeStruct((e, cap, d), BF16), jax.ShapeDtypeStruct((e, cap, 1), F32)],
        compiler_params=_cparams(("arbitrary",)),
    )(at, av, ak, af, hn, pos3, aff3)

    tm = min(2048, cap)
    tf = 256
    ye = pl.pallas_call(
        _ffn_kernel,
        grid=(e, cap // tm, f // tf),
        in_specs=[pl.BlockSpec((None, tm, d), lambda x, r, j: (x, r, 0)),
                  pl.BlockSpec((None, None, d, tf), lambda x, r, j: (layer, x, 0, j)),
                  pl.BlockSpec((None, None, d, tf), lambda x, r, j: (layer, x, 0, j)),
                  pl.BlockSpec((None, None, tf, d), lambda x, r, j: (layer, x, j, 0)),
                  pl.BlockSpec((None, tm, 1), lambda x, r, j: (x, r, 0))],
        out_specs=pl.BlockSpec((None, tm, d), lambda x, r, j: (x, r, 0)),
        out_shape=jax.ShapeDtypeStruct((e, cap, d), BF16),
        scratch_shapes=[pltpu.VMEM((min(FFN_ROW_CHUNK, tm), d), F32)] * (tm // min(FFN_ROW_CHUNK, tm)),
        compiler_params=_cparams(("parallel", "parallel", "arbitrary")),
    )(xe, w_gate, w_up, w_down, gc)

    grp = COMBINE_GROUP
    pos_spec = lambda k: pl.BlockSpec(
        (None, 1, tt), lambda s, ci, cv, ck, cf: (ci[s * grp + k] // n_r * n_t + ck[s], 0, 0))
    ye_spec = lambda k: pl.BlockSpec(
        (None, rt, d), lambda s, ci, cv, ck, cf: (ci[s * grp + k] // n_r, ci[s * grp + k] % n_r, 0))
    tile_spec = pl.BlockSpec((tt, d), lambda s, ci, cv, ck, cf: (ck[s], 0))
    return pl.pallas_call(
        functools.partial(_combine_kernel, rt=rt, n_r=n_r, group=grp),
        grid_spec=pltpu.PrefetchScalarGridSpec(
            num_scalar_prefetch=4,
            grid=(ck.shape[0],),
            in_specs=[tile_spec] + [pos_spec(k) for k in range(grp)] + [ye_spec(k) for k in range(grp)],
            out_specs=tile_spec,
        ),
        out_shape=jax.ShapeDtypeStruct((n, d), F32),
        compiler_params=_cparams(("arbitrary",)),
    )(ci, cv, ck, cf, h, *([pos3] * grp), *([ye] * grp))


def _t5_bucket_np(rel):
    nb = N_BUCKETS // 2
    max_exact = nb // 2
    ret = np.where(rel > 0, nb, 0)
    n = np.abs(rel)
    nf = np.maximum(n, 1).astype(np.float64)
    large = max_exact + (np.log(nf / max_exact) / math.log(MAX_DISTANCE / max_exact)
                         * (nb - max_exact)).astype(np.int64)
    large = np.minimum(large, nb - 1)
    return ret + np.where(n < max_exact, n, large)


def _far_distance():
    n = np.arange(1, 4 * MAX_DISTANCE)
    last = _t5_bucket_np(-n)
    return int(n[np.nonzero(last != last[-1])[0][-1]] + 1)


FAR = _far_distance()


SUM_LIMIT = 1e30
LOOKAHEAD = 2
N_BIAS_TILES = 5
assert FAR <= LANES + 1


def _bias_tiles_kernel(near_ref, far_ref, t_ref):
    h = pl.program_id(0)
    d = lax.broadcasted_iota(I32, (LANES, LANES), 0) - lax.broadcasted_iota(I32, (LANES, LANES), 1)
    for idx in range(N_BIAS_TILES):
        rel = d + LANES * (idx - 2)
        tile = jnp.where(rel < 0, far_ref[h, 0], far_ref[h, 1])
        if abs(idx - 2) <= 1:
            tile = lax.fori_loop(
                0, 2 * FAR - 1, lambda r, t: jnp.where(rel == r - (FAR - 1), near_ref[h, r], t), tile)
        t_ref[idx] = tile


def _bias_tiles(rel_bias):
    nh = rel_bias.shape[1]
    near = (rel_bias[_t5_bucket_np(np.arange(-(FAR - 1), FAR))] * LOG2E).T
    far = jnp.stack([rel_bias[N_BUCKETS // 2 - 1], rel_bias[N_BUCKETS - 1]], axis=1) * LOG2E
    smem = pl.BlockSpec(memory_space=pltpu.SMEM)
    tiles = pl.pallas_call(
        _bias_tiles_kernel,
        grid=(nh,),
        in_specs=[smem, smem],
        out_specs=pl.BlockSpec((None, N_BIAS_TILES, LANES, LANES), lambda h: (h, 0, 0, 0)),
        out_shape=jax.ShapeDtypeStruct((nh, N_BIAS_TILES, LANES, LANES), F32),
        compiler_params=_cparams(("arbitrary",)),
    )(near, far)
    return tiles, far


def _qkv_kernel(h_ref, g_ref, w_ref, bd_ref, gq_ref, gk_ref, qt_ref, k_ref, vt_ref, *, d, nh):
    hn = _rms(h_ref[...], g_ref[...]).astype(BF16)
    qkv = _dot(hn, w_ref[...])
    q, k, v = qkv[:, :d], qkv[:, d:2 * d], qkv[:, 2 * d:]

    def head_norm(x, gain):
        ms = _dot((x * x).astype(BF16), bd_ref[...])
        return x * lax.rsqrt(ms + EPS) * gain

    qn = head_norm(q, gq_ref[...]) * (HEAD_DIM ** -0.5 * LOG2E)
    k_ref[...] = head_norm(k, gk_ref[...]).astype(BF16)
    for h in range(nh):
        cols = slice(h * HEAD_BLOCK, (h + 1) * HEAD_BLOCK)
        qt_ref[h] = qn[:, cols].T.astype(BF16)
        vt_ref[h] = v[:, cols].T.astype(BF16)


def _attn_kernel(far_ref, qt_ref, k_ref, vt_ref, tiles_ref, lq1_ref, lk1_ref, lq2_ref, lk2_ref, gs_ref, o_ref,
                 m_ref, l_ref, acc_ref, redo_ref, *, tq, tc, n_chunks, lambda_init):
    h = pl.program_id(1)
    i = pl.program_id(2)
    j = pl.program_id(3)

    qt = qt_ref[...]
    row = lax.broadcasted_iota(I32, qt.shape, 0)
    comps = (jnp.where(row < HEAD_DIM, qt, jnp.zeros_like(qt)),
             jnp.where(row >= HEAD_DIM, qt, jnp.zeros_like(qt)))
    q0 = i * tq
    k_first = j * n_chunks * tc
    ones = jnp.ones((ONES_ROWS, tc), BF16)

    def far_side(k0):
        before = k0 + tc - 1 - q0 <= -FAR
        after = k0 - (q0 + tq - 1) >= FAR
        return before, after

    def bias_rows(k0, n_rows):
        blk = (k0 - q0) // LANES
        mid = N_BIAS_TILES // 2
        take = min(n_rows, LANES)
        return jnp.concatenate(
            [jnp.concatenate([tiles_ref[jnp.clip(blk + a - b, -mid, mid) + mid][:take]
                              for b in range(tq // LANES)], axis=1)
             for a in range(max(n_rows // LANES, 1))], axis=0)

    @pl.when(j == 0)
    def _():
        head = bias_rows(0, ONES_ROWS)
        for c in range(2):
            m_ref[c] = jnp.max(_dot(k_ref[:ONES_ROWS, :], comps[c]) + head, axis=0, keepdims=True)
        l_ref[...] = jnp.zeros(l_ref.shape, F32)
        acc_ref[...] = jnp.zeros(acc_ref.shape, F32)

    def scores(m, with_band):
        jj, c = m // 2, m % 2
        s = _dot(k_ref[jj * tc:(jj + 1) * tc, :], comps[c])
        return s + bias_rows(k_first + jj * tc, tc) if with_band else s

    def run(with_band, exact):
        state = [(m_ref[c], l_ref[c], acc_ref[c]) for c in range(2)]
        n_maps = 2 * n_chunks
        ahead = [scores(m, with_band) for m in range(min(LOOKAHEAD, n_maps))]
        for m in range(n_maps):
            jj, c = m // 2, m % 2
            s = ahead.pop(0)
            if m + LOOKAHEAD < n_maps:
                ahead.append(scores(m + LOOKAHEAD, with_band))
            rows = slice(jj * tc, (jj + 1) * tc)
            if with_band:
                const = 0.0
            else:
                before, _ = far_side(k_first + jj * tc)
                const = jnp.where(before, far_ref[h, 0], far_ref[h, 1])
            vt1 = jnp.concatenate([vt_ref[:, rows], ones], axis=0)
            m_old, l_old, acc_old = state[c]
            if exact:
                m_new = jnp.maximum(m_old, jnp.max(s, axis=0, keepdims=True) + const)
                alpha = jnp.exp2(m_old - m_new)
                pv = _dot(vt1, jnp.exp2(s - (m_new - const)).astype(BF16))
                state[c] = (m_new, alpha * l_old + pv[HEAD_BLOCK:HEAD_BLOCK + 1],
                            alpha * acc_old + pv[:HEAD_BLOCK])
            else:
                pv = _dot(vt1, jnp.exp2(s - (m_old - const)).astype(BF16))
                state[c] = (m_old, l_old + pv[HEAD_BLOCK:HEAD_BLOCK + 1], acc_old + pv[:HEAD_BLOCK])
        return state

    def commit(state):
        for c in range(2):
            m_ref[c], l_ref[c], acc_ref[c] = state[c]

    def streamed(with_band):
        state = run(with_band, exact=False)
        worst = [jnp.maximum(jnp.max(jnp.where(l < SUM_LIMIT, 0.0, 1.0)),
                             jnp.max(jnp.where(jnp.abs(acc) < SUM_LIMIT, 0.0, 1.0))) for _, l, acc in state]
        safe = jnp.maximum(worst[0], worst[1]) == 0.0
        redo_ref[0] = jnp.where(safe, 0, 1)

        @pl.when(safe)
        def _():
            commit(state)

    last_before, _ = far_side(k_first + (n_chunks - 1) * tc)
    _, first_after = far_side(k_first)
    all_far = last_before | first_after

    @pl.when(all_far)
    def _():
        streamed(False)

    @pl.when(jnp.logical_not(all_far))
    def _():
        streamed(True)

    @pl.when(redo_ref[0] == 1)
    def _():
        commit(run(True, exact=True))

    @pl.when(j == pl.num_programs(3) - 1)
    def _():
        lam = (jnp.exp(jnp.sum(lq1_ref[...] * lk1_ref[...], keepdims=True))
               - jnp.exp(jnp.sum(lq2_ref[...] * lk2_ref[...], keepdims=True)) + lambda_init)
        a = acc_ref[0] / l_ref[0] - lam * (acc_ref[1] / l_ref[1])
        a = a * lax.rsqrt(jnp.mean(a * a, axis=0, keepdims=True) + EPS) * gs_ref[...]
        o_ref[...] = (a * (1.0 - lambda_init)).T.astype(BF16)


def _proj_kernel(o_ref, w_ref, h_ref, out_ref):
    out_ref[...] = h_ref[...] + _dot(o_ref[...], w_ref[...])


def _attn_tiles(s):
    tq = min(1024, s)
    tc = min(512, s)
    assert tq % LANES == 0 and tc % LANES == 0
    return tq, tc


def _attention_layer(h, g, w_qkv, w_o, g_q, g_k, lq1, lk1, lq2, lk2, g_sub, rel_bias, lambda_init):
    b, s, d = h.shape
    nh = d // HEAD_BLOCK
    tn = min(512, s)
    hd = np.arange(d) // HEAD_DIM
    bd = jnp.asarray((hd[:, None] == hd[None, :]) / HEAD_DIM, BF16)
    tile_gain = lambda x: jnp.tile(x, d // HEAD_DIM).reshape(1, d)
    const2 = lambda shape: pl.BlockSpec(shape, lambda bi, i: (0,) * len(shape))
    qt, k, vt = pl.pallas_call(
        functools.partial(_qkv_kernel, d=d, nh=nh),
        grid=(b, s // tn),
        in_specs=[pl.BlockSpec((None, tn, d), lambda bi, i: (bi, i, 0)),
                  const2((1, d)), const2((d, 3 * d)), const2((d, d)), const2((1, d)), const2((1, d))],
        out_specs=[pl.BlockSpec((None, nh, HEAD_BLOCK, tn), lambda bi, i: (bi, 0, 0, i)),
                   pl.BlockSpec((None, tn, d), lambda bi, i: (bi, i, 0)),
                   pl.BlockSpec((None, nh, HEAD_BLOCK, tn), lambda bi, i: (bi, 0, 0, i))],
        out_shape=[jax.ShapeDtypeStruct((b, nh, HEAD_BLOCK, s), BF16),
                   jax.ShapeDtypeStruct((b, s, d), BF16),
                   jax.ShapeDtypeStruct((b, nh, HEAD_BLOCK, s), BF16)],
        compiler_params=_cparams(("parallel", "parallel")),
    )(h, g.reshape(1, d), w_qkv.astype(BF16), bd, tile_gain(g_q), tile_gain(g_k))

    tq, tc = _attn_tiles(s)
    tko = min(4096, s)
    n_chunks = tko // tc
    tiles, far = _bias_tiles(rel_bias)
    const4 = lambda shape: pl.BlockSpec(shape, lambda bi, hi, i, j: (0,) * len(shape))
    lam_spec = const4((1, HEAD_DIM))
    o = pl.pallas_call(
        functools.partial(_attn_kernel, tq=tq, tc=tc, n_chunks=n_chunks, lambda_init=lambda_init),
        grid=(b, nh, s // tq, s // tko),
        in_specs=[pl.BlockSpec(memory_space=pltpu.SMEM),
                  pl.BlockSpec((None, None, HEAD_BLOCK, tq), lambda bi, hi, i, j: (bi, hi, 0, i)),
                  pl.BlockSpec((None, tko, HEAD_BLOCK), lambda bi, hi, i, j: (bi, j, hi)),
                  pl.BlockSpec((None, None, HEAD_BLOCK, tko), lambda bi, hi, i, j: (bi, hi, 0, j)),
                  pl.BlockSpec((None, N_BIAS_TILES, LANES, LANES), lambda bi, hi, i, j: (hi, 0, 0, 0)),
                  lam_spec, lam_spec, lam_spec, lam_spec,
                  const4((HEAD_BLOCK, 1))],
        out_specs=pl.BlockSpec((None, tq, HEAD_BLOCK), lambda bi, hi, i, j: (bi, i, hi)),
        out_shape=jax.ShapeDtypeStruct((b, s, d), BF16),
        scratch_shapes=[pltpu.VMEM((2, 1, tq), F32), pltpu.VMEM((2, 1, tq), F32),
                        pltpu.VMEM((2, HEAD_BLOCK, tq), F32), pltpu.SMEM((1,), I32)],
        compiler_params=_cparams(("parallel", "parallel", "parallel", "arbitrary")),
    )(far, qt, k, vt, tiles, lq1.reshape(1, -1), lk1.reshape(1, -1), lq2.reshape(1, -1), lk2.reshape(1, -1),
      g_sub.reshape(HEAD_BLOCK, 1))

    n = b * s
    tp = min(1024, n)
    out = pl.pallas_call(
        _proj_kernel,
        grid=(n // tp,),
        in_specs=[pl.BlockSpec((tp, d), lambda i: (i, 0)),
                  pl.BlockSpec((d, d), lambda i: (0, 0)),
                  pl.BlockSpec((tp, d), lambda i: (i, 0))],
        out_specs=pl.BlockSpec((tp, d), lambda i: (i, 0)),
        out_shape=jax.ShapeDtypeStruct((n, d), F32),
        compiler_params=_cparams(("parallel",)),
    )(o.reshape(n, d), w_o.astype(BF16), h.reshape(n, d))
    return out.reshape(b, s, d)


def _trunk(x, rel_bias, wa, wb, g_mix, g_ffn, w_qkv, w_attn_out, g_q, g_k, lq1, lk1, lq2, lk2, g_sub,
           w_router, w_gate, w_up, w_down):
    b, s, d = x.shape
    moe = lambda h, i: _moe_layer(h.reshape(b * s, d), g_ffn[i], w_router[i], w_gate, w_up, w_down,
                                  i).reshape(b, s, d)
    h = _fourier_layer(x, g_mix[0], wa, wb)
    h = moe(h, 0)
    lambda_init = 0.8 - 0.6 * math.exp(-0.3 * 1)
    h = _attention_layer(h, g_mix[1], w_qkv[0], w_attn_out[0], g_q[0], g_k[0], lq1[0], lk1[0], lq2[0],
                         lk2[0], g_sub[0], rel_bias, lambda_init)
    return moe(h, 1)


def kernel(x_prompt, x_sample, rel_bias, g_mix, g_ffn, w_fourier_out, w_qkv, w_attn_out, g_q, g_k,
           lambda_q1, lambda_k1, lambda_q2, lambda_k2, g_sub, w_router, w_gate, w_up, w_down):
    assert g_mix.shape[0] == 2, "one Fourier layer followed by one attention layer"
    wa, wb = _fold_w(w_fourier_out[0])
    run = lambda x: _trunk(x, rel_bias, wa, wb, g_mix, g_ffn, w_qkv, w_attn_out, g_q, g_k, lambda_q1, lambda_k1,
                           lambda_q2, lambda_k2, g_sub, w_router, w_gate, w_up, w_down)
    return (run(x_prompt), run(x_sample))
```

```python
import functools
import math

import numpy as np
import jax
import jax.numpy as jnp
from jax import lax
from jax.experimental import pallas as pl
from jax.experimental.pallas import tpu as pltpu

EPS = 1e-6
F32 = jnp.float32
BF16 = jnp.bfloat16
I32 = jnp.int32
HIGHEST = lax.Precision.HIGHEST
LOG2E = math.log2(math.e)

LANES = 128
SUBLANES = 8
FOURIER_GROUP_DIM = 128
HEAD_BLOCK = 128
HEAD_DIM = 64
ONES_ROWS = 16
N_EXPERTS = 16
CAPACITY_FACTOR = 2
FFN_ROW_CHUNK = 512
COMBINE_GROUP = 8
STEP_NEXT, STEP_FIRST, STEP_IDLE = 0, 1, 2
N_BUCKETS = 32
MAX_DISTANCE = 128
VMEM_LIMIT = 56 * 1024 * 1024

TOKEN_TILE = 1024
MOE_ROW_TILE = 256
FFN_ROW_TILE = 2048
FFN_HIDDEN_TILE = 256
QKV_TOKEN_TILE = 512
ATTN_QUERY_TILE = 1024
ATTN_KEY_CHUNK = 512
ATTN_KEY_BLOCK = 4096


def _cparams(sem):
    return pltpu.CompilerParams(dimension_semantics=sem, vmem_limit_bytes=VMEM_LIMIT)


def _rms(x, g):
    return x * lax.rsqrt(jnp.mean(x * x, axis=-1, keepdims=True) + EPS) * g


def _dot(a, b):
    return jnp.dot(a, b, preferred_element_type=F32)


def _cos_sin(n):
    k = np.arange(n)
    ang = 2.0 * np.pi * ((k[:, None] * k[None, :]) % n) / n
    return np.cos(ang), np.sin(ang)


def _fold_w_kernel(cc_ref, sc_ref, w_ref, wa_ref, wb_ref):
    w = w_ref[...]
    wa_ref[...] = jnp.dot(cc_ref[...], w, precision=HIGHEST, preferred_element_type=F32).astype(BF16)
    wb_ref[...] = jnp.dot(sc_ref[...], w, precision=HIGHEST, preferred_element_type=F32).astype(BF16)


def _fold_w(w):
    d = w.shape[0]
    gd = FOURIER_GROUP_DIM
    c, s = _cos_sin(gd)
    cc = jnp.asarray(c / math.sqrt(gd), F32)
    sc = jnp.asarray(s / math.sqrt(gd), F32)
    const = pl.BlockSpec((gd, gd), lambda g: (0, 0))
    blk = pl.BlockSpec((gd, d), lambda g: (g, 0))
    return pl.pallas_call(
        _fold_w_kernel,
        grid=(d // gd,),
        in_specs=[const, const, blk],
        out_specs=[blk, blk],
        out_shape=[jax.ShapeDtypeStruct((d, d), BF16)] * 2,
        compiler_params=_cparams(("arbitrary",)),
    )(cc, sc, w)


def _fourier_a_kernel(x_ref, g_ref, f1_ref, twr_ref, twi_ref, yr_ref, yi_ref, *, nb, n1, d):
    j = pl.program_id(1)
    lane = lax.broadcasted_iota(I32, twr_ref.shape, 1)
    for c in range(nb):
        xn = _rms(x_ref[:, c, :], g_ref[...]).astype(BF16)
        y = _dot(f1_ref[...], xn)
        yr, yi = y[:n1], y[n1:]
        sel = lane == (j * nb + c)
        tr = jnp.sum(jnp.where(sel, twr_ref[...], 0.0), axis=1, keepdims=True)
        ti = jnp.sum(jnp.where(sel, twi_ref[...], 0.0), axis=1, keepdims=True)
        yr_ref[:, c, :] = (yr * tr - yi * ti).astype(BF16)
        yi_ref[:, c, :] = (yr * ti + yi * tr).astype(BF16)


def _fourier_b_kernel(yr_ref, yi_ref, f2_ref, wa_ref, wb_ref, x_ref, o_ref, *, kb, n2):
    ars, ais = [], []
    for c in range(kb):
        rows = slice(c * n2, (c + 1) * n2)
        ys = jnp.concatenate([yr_ref[rows, :], yi_ref[rows, :]], axis=0)
        a = _dot(f2_ref[...], ys)
        ars.append(a[:n2].astype(BF16))
        ais.append(a[n2:].astype(BF16))
    ar = jnp.concatenate(ars, axis=0)
    ai = jnp.concatenate(ais, axis=0)
    out = _dot(ar, wa_ref[...]) + _dot(ai, wb_ref[...])
    for c in range(kb):
        o_ref[:, c, :] = x_ref[:, c, :] + out[c * n2:(c + 1) * n2]


def _split_len(s):
    n1 = 1 << (int(math.log2(s)) // 2)
    assert s % n1 == 0
    return n1, s // n1


def _fourier_layer(x, g, wa, wb):
    b, s, d = x.shape
    n1, n2 = _split_len(s)
    c1, s1 = _cos_sin(n1)
    c2, s2 = _cos_sin(n2)
    f1 = jnp.asarray(np.concatenate([c1, -s1], axis=0) / math.sqrt(n1), BF16)
    f2 = jnp.asarray(np.block([[c2, s2], [-s2, c2]]) / math.sqrt(n2), BF16)
    ang = 2.0 * np.pi * ((np.arange(n1)[:, None] * np.arange(n2)[None, :]) % s) / s
    twr = jnp.asarray(np.cos(ang), F32)
    twi = jnp.asarray(-np.sin(ang), F32)

    nb = 2 * SUBLANES
    blk_a = pl.BlockSpec((None, n1, nb, d), lambda bi, j: (bi, 0, j, 0))
    full = lambda shape: pl.BlockSpec(shape, lambda bi, j: (0,) * len(shape))
    yr, yi = pl.pallas_call(
        functools.partial(_fourier_a_kernel, nb=nb, n1=n1, d=d),
        grid=(b, n2 // nb),
        in_specs=[blk_a, full((1, d)), full((2 * n1, n1)), full((n1, n2)), full((n1, n2))],
        out_specs=[blk_a, blk_a],
        out_shape=[jax.ShapeDtypeStruct((b, n1, n2, d), BF16)] * 2,
        compiler_params=_cparams(("parallel", "parallel")),
    )(x.reshape(b, n1, n2, d), g.reshape(1, d), f1, twr, twi)

    kb = SUBLANES
    blk_y = pl.BlockSpec((None, kb * n2, d), lambda bi, i: (bi, i, 0))
    blk_x = pl.BlockSpec((None, n2, kb, d), lambda bi, i: (bi, 0, i, 0))
    out = pl.pallas_call(
        functools.partial(_fourier_b_kernel, kb=kb, n2=n2),
        grid=(b, n1 // kb),
        in_specs=[blk_y, blk_y, full((2 * n2, 2 * n2)), full((d, d)), full((d, d)), blk_x],
        out_specs=blk_x,
        out_shape=jax.ShapeDtypeStruct((b, n2, n1, d), F32),
        compiler_params=_cparams(("parallel", "parallel")),
    )(yr.reshape(b, s, d), yi.reshape(b, s, d), f2, wa, wb, x.reshape(b, n2, n1, d))
    return out.reshape(b, s, d)


def _router_kernel(h_ref, g_ref, wrt_ref, hn_ref, aff_ref):
    hn = _rms(h_ref[...], g_ref[...])
    hn_ref[...] = hn.astype(BF16)
    logits = lax.dot_general(wrt_ref[...], hn, (((1,), (1,)), ((), ())),
                             precision=HIGHEST, preferred_element_type=F32)
    ex = jnp.exp(logits - jnp.max(logits, axis=0, keepdims=True))
    aff_ref[...] = ex / jnp.sum(ex, axis=0, keepdims=True)


def _select_kernel(a_ref, pos_ref, cum_ref, *, e, nb, k):
    keys = lax.bitcast_convert_type(a_ref[...], I32).reshape(e, nb, LANES)

    def count(mask):
        part = jnp.sum(mask.astype(F32), axis=2, keepdims=True)
        return jnp.sum(part, axis=1, keepdims=True)

    def value_step(i, t):
        cand = t | (jnp.int32(1) << (30 - i))
        return jnp.where(count(keys >= cand) >= k, cand, t)

    thr = lax.fori_loop(0, 31, value_step, jnp.zeros((e, 1, 1), I32))
    gt = keys > thr
    eq = keys == thr
    need = k - count(gt)
    tok = (lax.broadcasted_iota(I32, (e, nb, LANES), 1) * LANES
           + lax.broadcasted_iota(I32, (e, nb, LANES), 2))
    nbits = (nb * LANES - 1).bit_length()

    def index_step(i, v):
        cand = v | (jnp.int32(1) << (nbits - 1 - i))
        return jnp.where(count(eq & (tok < cand)) < need, cand, v)

    last = lax.fori_loop(0, nbits, index_step, jnp.zeros((e, 1, 1), I32))
    sel = (gt | (eq & (tok <= last))).reshape(e * nb, LANES)
    self32 = sel.astype(F32)

    r = lax.broadcasted_iota(I32, (LANES, LANES), 0)
    c = lax.broadcasted_iota(I32, (LANES, LANES), 1)
    inc = _dot(self32.astype(BF16), (r <= c).astype(BF16))
    tot = jnp.broadcast_to(inc[:, LANES - 1:LANES], (e * nb, LANES)).astype(BF16)
    rb = lax.broadcasted_iota(I32, (nb, nb), 0)
    cb = lax.broadcasted_iota(I32, (nb, nb), 1)
    lower = (cb < rb).astype(BF16)
    before = jnp.concatenate([_dot(lower, tot[x * nb:(x + 1) * nb]) for x in range(e)], axis=0)
    pos = inc - self32 + before
    pos_ref[...] = jnp.where(sel, pos, -1.0).astype(I32)
    cum_ref[...] = before.astype(I32)


def _one_hot_mask(pos_ref, row_tile, rt):
    tt = pos_ref.shape[-1]
    rows = lax.broadcasted_iota(I32, (rt, tt), 0) + row_tile * rt
    return pos_ref[...] == rows


def _gather_kernel(gt, gv, gk, gf, hn_ref, pos_ref, aff_ref, xe_ref, gc_ref, *, rt, n_r):
    s = pl.program_id(0)
    tt = pos_ref.shape[-1]

    @pl.when(gf[s] != STEP_IDLE)
    def _():
        mask = _one_hot_mask(pos_ref, gk[s] % n_r, rt)
        tokens = hn_ref[pl.ds(pl.multiple_of(gt[s] * tt, tt), tt), :]
        rows = _dot(jnp.where(mask, 1.0, 0.0).astype(BF16), tokens)
        gate = jnp.sum(jnp.where(mask, aff_ref[...], 0.0), axis=1, keepdims=True)

        @pl.when(gf[s] & STEP_FIRST != 0)
        def _():
            xe_ref[...] = rows.astype(BF16)
            gc_ref[...] = gate

        @pl.when(gf[s] & STEP_FIRST == 0)
        def _():
            xe_ref[...] = (xe_ref[...].astype(F32) + rows).astype(BF16)
            gc_ref[...] += gate


def _ffn_kernel(xe_ref, wg_ref, wu_ref, wd_ref, gc_ref, ye_ref, *acc_refs):
    j = pl.program_id(2)
    n_chunks = len(acc_refs)
    rc = acc_refs[0].shape[0]

    @pl.when(j == 0)
    def _():
        for acc_ref in acc_refs:
            acc_ref[...] = jnp.zeros(acc_ref.shape, F32)

    wg = wg_ref[...].astype(BF16)
    wu = wu_ref[...].astype(BF16)
    wd = wd_ref[...].astype(BF16)

    def gate_up(c):
        x = xe_ref[c * rc:(c + 1) * rc, :]
        return _dot(x, wg), _dot(x, wu)

    ahead = gate_up(0)
    for c in range(n_chunks):
        g, u = ahead
        if c + 1 < n_chunks:
            ahead = gate_up(c + 1)
        hid = (g / (1.0 + jnp.exp(-g))) * u
        acc_refs[c][...] += _dot(hid.astype(BF16), wd)

    @pl.when(j == pl.num_programs(2) - 1)
    def _():
        for c, acc_ref in enumerate(acc_refs):
            rows = slice(c * rc, (c + 1) * rc)
            ye_ref[rows, :] = (acc_ref[...] * gc_ref[rows, :]).astype(BF16)


def _combine_kernel(gi, gv, gk, gf, h_ref, *refs, rt, n_r, group):
    pos_refs, ye_refs, o_ref = refs[:group], refs[group:2 * group], refs[2 * group]
    s = pl.program_id(0)

    @pl.when(gf[s] != STEP_IDLE)
    def _():
        masks = [_one_hot_mask(pos_refs[k], gi[s * group + k] % n_r, rt) & (gv[s * group + k] == 1)
                 for k in range(group)]
        onehot = jnp.where(jnp.concatenate(masks, axis=0), 1.0, 0.0).astype(BF16)
        ye = jnp.concatenate([ye_refs[k][...] for k in range(group)], axis=0)
        back = lax.dot_general(onehot, ye, (((0,), (0,)), ((), ())), preferred_element_type=F32)

        @pl.when(gf[s] & STEP_FIRST != 0)
        def _():
            o_ref[...] = h_ref[...] + back

        @pl.when(gf[s] & STEP_FIRST == 0)
        def _():
            o_ref[...] += back


def _schedule_kernel(lo_ref, *out_refs, n_e, n_r, n_t, rt, cap, groups):
    def emitter(item_ref, valid_ref, key_ref, flag_ref, group):
        for i_ref, fill in ((item_ref, 0), (valid_ref, 0), (key_ref, 0), (flag_ref, STEP_IDLE)):
            def clear(i, c, i_ref=i_ref, fill=fill):
                i_ref[i] = fill
                return c
            lax.fori_loop(0, i_ref.shape[0], clear, 0)

        def emit(state, key, item):
            step, cnt, cur = state
            fresh = key != cur
            step = step + (fresh & (cnt > 0)).astype(I32)
            cnt = jnp.where(fresh, 0, cnt)
            item_ref[step * group + cnt] = item
            valid_ref[step * group + cnt] = 1

            @pl.when(cnt == 0)
            def _():
                key_ref[step] = key
                flag_ref[step] = jnp.where(fresh, STEP_FIRST, STEP_NEXT)

            full = cnt + 1 == group
            return step + full.astype(I32), jnp.where(full, 0, cnt + 1), key

        def finish(state):
            step, cnt, cur = state

            def pad(i, c):
                key_ref[i] = cur
                return c
            lax.fori_loop(step + (cnt > 0).astype(I32), key_ref.shape[0], pad, 0)

        return emit, finish

    def rows_of(e, t):
        lo = lo_ref[e, t]
        hi = jnp.where(t + 1 < n_t, lo_ref[e, jnp.minimum(t + 1, n_t - 1)], cap)
        return lo, hi

    start = (jnp.int32(0), jnp.int32(0), jnp.int32(-1))
    emit_g, finish_g = emitter(*out_refs[:4], groups[0])

    def gather_tile(i, state):
        e, t = i // n_t, i % n_t
        lo, hi = rows_of(e, t)
        return lax.fori_loop(lo // rt, jnp.where(hi > lo, (hi - 1) // rt + 1, lo // rt),
                             lambda r, st: emit_g(st, e * n_r + r, t), state)

    finish_g(lax.fori_loop(0, n_e * n_t, gather_tile, start))

    emit_c, finish_c = emitter(*out_refs[4:], groups[1])

    def combine_tile(i, state):
        t, e = i // n_e, i % n_e
        lo, hi = rows_of(e, t)
        first = jnp.minimum(lo // rt, n_r - 1)
        stop = jnp.where(hi > lo, (hi - 1) // rt + 1, jnp.where(e == 0, first + 1, first))
        return lax.fori_loop(first, stop, lambda r, st: emit_c(st, t, e * n_r + r), state)

    finish_c(lax.fori_loop(0, n_t * n_e, combine_tile, start))


def _schedules(lo, cap, rt, groups):
    n_e, n_t = lo.shape
    n_r = cap // rt
    n_true = n_e * (n_r + n_t - 1)
    steps = ((n_true + (groups[0] - 1) * n_e * n_r) // groups[0],
             (n_true + n_t + (groups[1] - 1) * n_t) // groups[1])
    sizes = [steps[0] * groups[0]] * 2 + [steps[0]] * 2 + [steps[1] * groups[1]] * 2 + [steps[1]] * 2
    smem = pl.BlockSpec(memory_space=pltpu.SMEM)
    out = pl.pallas_call(
        functools.partial(_schedule_kernel, n_e=n_e, n_r=n_r, n_t=n_t, rt=rt, cap=cap, groups=groups),
        in_specs=[smem],
        out_specs=[smem] * 8,
        out_shape=[jax.ShapeDtypeStruct((k,), I32) for k in sizes],
    )(lo)
    return out[:4], out[4:]


def _moe_layer(h, g, w_router, w_gate, w_up, w_down, layer):
    n, d = h.shape
    e = N_EXPERTS
    cap = CAPACITY_FACTOR * n // e
    f = w_gate.shape[-1]
    nb = n // LANES
    tn = min(TOKEN_TILE, n)

    hn, aff = pl.pallas_call(
        _router_kernel,
        grid=(n // tn,),
        in_specs=[pl.BlockSpec((tn, d), lambda i: (i, 0)),
                  pl.BlockSpec((1, d), lambda i: (0, 0)),
                  pl.BlockSpec((e, d), lambda i: (0, 0))],
        out_specs=[pl.BlockSpec((tn, d), lambda i: (i, 0)),
                   pl.BlockSpec((e, tn), lambda i: (0, i))],
        out_shape=[jax.ShapeDtypeStruct((n, d), BF16), jax.ShapeDtypeStruct((e, n), F32)],
        compiler_params=_cparams(("parallel",)),
    )(h, g.reshape(1, d), w_router.T)

    whole = pl.BlockSpec((e * nb, LANES), lambda i: (0, 0))
    pos, cum = pl.pallas_call(
        functools.partial(_select_kernel, e=e, nb=nb, k=cap),
        grid=(1,),
        in_specs=[whole],
        out_specs=[whole, whole],
        out_shape=[jax.ShapeDtypeStruct((e * nb, LANES), I32)] * 2,
        compiler_params=_cparams(("arbitrary",)),
    )(aff.reshape(e * nb, LANES))

    rt = min(MOE_ROW_TILE, cap)
    tt = tn
    n_r, n_t = cap // rt, n // tt
    lo = cum[:, 0].reshape(e, nb)[:, ::tt // LANES]
    (at, av, ak, af), (ci, cv, ck, cf) = _schedules(lo, cap, rt, (1, COMBINE_GROUP))

    pos3 = pos.reshape(e * n_t, 1, tt)
    aff3 = aff.reshape(e * n_t, 1, tt)
    row_spec = pl.BlockSpec((None, 1, tt), lambda s, at, av, ak, af: (ak[s] // n_r * n_t + at[s], 0, 0))
    xe, gc = pl.pallas_call(
        functools.partial(_gather_kernel, rt=rt, n_r=n_r),
        grid_spec=pltpu.PrefetchScalarGridSpec(
            num_scalar_prefetch=4,
            grid=(ak.shape[0],),
            in_specs=[pl.BlockSpec((n, d), lambda s, at, av, ak, af: (0, 0), pipeline_mode=pl.Buffered(1)),
                      row_spec, row_spec],
            out_specs=[pl.BlockSpec((None, rt, d), lambda s, at, av, ak, af: (ak[s] // n_r, ak[s] % n_r, 0)),
                       pl.BlockSpec((None, rt, 1), lambda s, at, av, ak, af: (ak[s] // n_r, ak[s] % n_r, 0))],
        ),
        out_shape=[jax.ShapeDtypeStruct((e, cap, d), BF16), jax.ShapeDtypeStruct((e, cap, 1), F32)],
        compiler_params=_cparams(("arbitrary",)),
    )(at, av, ak, af, hn, pos3, aff3)

    tm = min(FFN_ROW_TILE, cap)
    tf = FFN_HIDDEN_TILE
    ye = pl.pallas_call(
        _ffn_kernel,
        grid=(e, cap // tm, f // tf),
        in_specs=[pl.BlockSpec((None, tm, d), lambda x, r, j: (x, r, 0)),
                  pl.BlockSpec((None, None, d, tf), lambda x, r, j: (layer, x, 0, j)),
                  pl.BlockSpec((None, None, d, tf), lambda x, r, j: (layer, x, 0, j)),
                  pl.BlockSpec((None, None, tf, d), lambda x, r, j: (layer, x, j, 0)),
                  pl.BlockSpec((None, tm, 1), lambda x, r, j: (x, r, 0))],
        out_specs=pl.BlockSpec((None, tm, d), lambda x, r, j: (x, r, 0)),
        out_shape=jax.ShapeDtypeStruct((e, cap, d), BF16),
        scratch_shapes=[pltpu.VMEM((min(FFN_ROW_CHUNK, tm), d), F32)] * (tm // min(FFN_ROW_CHUNK, tm)),
        compiler_params=_cparams(("parallel", "parallel", "arbitrary")),
    )(xe, w_gate, w_up, w_down, gc)

    grp = COMBINE_GROUP
    pos_spec = lambda k: pl.BlockSpec(
        (None, 1, tt), lambda s, ci, cv, ck, cf: (ci[s * grp + k] // n_r * n_t + ck[s], 0, 0))
    ye_spec = lambda k: pl.BlockSpec(
        (None, rt, d), lambda s, ci, cv, ck, cf: (ci[s * grp + k] // n_r, ci[s * grp + k] % n_r, 0))
    tile_spec = pl.BlockSpec((tt, d), lambda s, ci, cv, ck, cf: (ck[s], 0))
    return pl.pallas_call(
        functools.partial(_combine_kernel, rt=rt, n_r=n_r, group=grp),
        grid_spec=pltpu.PrefetchScalarGridSpec(
            num_scalar_prefetch=4,
            grid=(ck.shape[0],),
            in_specs=[tile_spec] + [pos_spec(k) for k in range(grp)] + [ye_spec(k) for k in range(grp)],
            out_specs=tile_spec,
        ),
        out_shape=jax.ShapeDtypeStruct((n, d), F32),
        compiler_params=_cparams(("arbitrary",)),
    )(ci, cv, ck, cf, h, *([pos3] * grp), *([ye] * grp))


def _t5_bucket_np(rel):
    nb = N_BUCKETS // 2
    max_exact = nb // 2
    ret = np.where(rel > 0, nb, 0)
    n = np.abs(rel)
    nf = np.maximum(n, 1).astype(np.float64)
    large = max_exact + (np.log(nf / max_exact) / math.log(MAX_DISTANCE / max_exact)
                         * (nb - max_exact)).astype(np.int64)
    large = np.minimum(large, nb - 1)
    return ret + np.where(n < max_exact, n, large)


def _far_distance():
    n = np.arange(1, 4 * MAX_DISTANCE)
    last = _t5_bucket_np(-n)
    return int(n[np.nonzero(last != last[-1])[0][-1]] + 1)


FAR = _far_distance()


SUM_LIMIT = 1e30
LOOKAHEAD = 2
N_BIAS_TILES = 5
assert FAR <= LANES + 1


def _bias_tiles_kernel(near_ref, far_ref, t_ref):
    h = pl.program_id(0)
    d = lax.broadcasted_iota(I32, (LANES, LANES), 0) - lax.broadcasted_iota(I32, (LANES, LANES), 1)
    for idx in range(N_BIAS_TILES):
        rel = d + LANES * (idx - 2)
        tile = jnp.where(rel < 0, far_ref[h, 0], far_ref[h, 1])
        if abs(idx - 2) <= 1:
            tile = lax.fori_loop(
                0, 2 * FAR - 1, lambda r, t: jnp.where(rel == r - (FAR - 1), near_ref[h, r], t), tile)
        t_ref[idx] = tile


def _bias_tiles(rel_bias):
    nh = rel_bias.shape[1]
    near = (rel_bias[_t5_bucket_np(np.arange(-(FAR - 1), FAR))] * LOG2E).T
    far = jnp.stack([rel_bias[N_BUCKETS // 2 - 1], rel_bias[N_BUCKETS - 1]], axis=1) * LOG2E
    smem = pl.BlockSpec(memory_space=pltpu.SMEM)
    tiles = pl.pallas_call(
        _bias_tiles_kernel,
        grid=(nh,),
        in_specs=[smem, smem],
        out_specs=pl.BlockSpec((None, N_BIAS_TILES, LANES, LANES), lambda h: (h, 0, 0, 0)),
        out_shape=jax.ShapeDtypeStruct((nh, N_BIAS_TILES, LANES, LANES), F32),
        compiler_params=_cparams(("arbitrary",)),
    )(near, far)
    return tiles, far


def _qkv_kernel(h_ref, g_ref, w_ref, bd_ref, gq_ref, gk_ref, qt_ref, k_ref, vt_ref, *, d, nh):
    hn = _rms(h_ref[...], g_ref[...]).astype(BF16)
    qkv = _dot(hn, w_ref[...])
    q, k, v = qkv[:, :d], qkv[:, d:2 * d], qkv[:, 2 * d:]

    def head_norm(x, gain):
        ms = _dot((x * x).astype(BF16), bd_ref[...])
        return x * lax.rsqrt(ms + EPS) * gain

    qn = head_norm(q, gq_ref[...]) * (HEAD_DIM ** -0.5 * LOG2E)
    k_ref[...] = head_norm(k, gk_ref[...]).astype(BF16)
    for h in range(nh):
        cols = slice(h * HEAD_BLOCK, (h + 1) * HEAD_BLOCK)
        qt_ref[h] = qn[:, cols].T.astype(BF16)
        vt_ref[h] = v[:, cols].T.astype(BF16)


def _attn_kernel(far_ref, qt_ref, k_ref, vt_ref, tiles_ref, lq1_ref, lk1_ref, lq2_ref, lk2_ref, gs_ref, o_ref,
                 m_ref, l_ref, acc_ref, redo_ref, *, tq, tc, n_chunks, lambda_init):
    h = pl.program_id(1)
    i = pl.program_id(2)
    j = pl.program_id(3)

    qt = qt_ref[...]
    row = lax.broadcasted_iota(I32, qt.shape, 0)
    comps = (jnp.where(row < HEAD_DIM, qt, jnp.zeros_like(qt)),
             jnp.where(row >= HEAD_DIM, qt, jnp.zeros_like(qt)))
    q0 = i * tq
    k_first = j * n_chunks * tc
    ones = jnp.ones((ONES_ROWS, tc), BF16)

    def far_side(k0):
        before = k0 + tc - 1 - q0 <= -FAR
        after = k0 - (q0 + tq - 1) >= FAR
        return before, after

    def bias_rows(k0, n_rows):
        blk = (k0 - q0) // LANES
        mid = N_BIAS_TILES // 2
        take = min(n_rows, LANES)
        return jnp.concatenate(
            [jnp.concatenate([tiles_ref[jnp.clip(blk + a - b, -mid, mid) + mid][:take]
                              for b in range(tq // LANES)], axis=1)
             for a in range(max(n_rows // LANES, 1))], axis=0)

    @pl.when(j == 0)
    def _():
        head = bias_rows(0, ONES_ROWS)
        for c in range(2):
            m_ref[c] = jnp.max(_dot(k_ref[:ONES_ROWS, :], comps[c]) + head, axis=0, keepdims=True)
        l_ref[...] = jnp.zeros(l_ref.shape, F32)
        acc_ref[...] = jnp.zeros(acc_ref.shape, F32)

    def scores(m, with_band):
        jj, c = m // 2, m % 2
        s = _dot(k_ref[jj * tc:(jj + 1) * tc, :], comps[c])
        return s + bias_rows(k_first + jj * tc, tc) if with_band else s

    def run(with_band, exact):
        state = [(m_ref[c], l_ref[c], acc_ref[c]) for c in range(2)]
        n_maps = 2 * n_chunks
        ahead = [scores(m, with_band) for m in range(min(LOOKAHEAD, n_maps))]
        for m in range(n_maps):
            jj, c = m // 2, m % 2
            s = ahead.pop(0)
            if m + LOOKAHEAD < n_maps:
                ahead.append(scores(m + LOOKAHEAD, with_band))
            rows = slice(jj * tc, (jj + 1) * tc)
            if with_band:
                const = 0.0
            else:
                before, _ = far_side(k_first + jj * tc)
                const = jnp.where(before, far_ref[h, 0], far_ref[h, 1])
            vt1 = jnp.concatenate([vt_ref[:, rows], ones], axis=0)
            m_old, l_old, acc_old = state[c]
            if exact:
                m_new = jnp.maximum(m_old, jnp.max(s, axis=0, keepdims=True) + const)
                alpha = jnp.exp2(m_old - m_new)
                pv = _dot(vt1, jnp.exp2(s - (m_new - const)).astype(BF16))
                state[c] = (m_new, alpha * l_old + pv[HEAD_BLOCK:HEAD_BLOCK + 1],
                            alpha * acc_old + pv[:HEAD_BLOCK])
            else:
                pv = _dot(vt1, jnp.exp2(s - (m_old - const)).astype(BF16))
                state[c] = (m_old, l_old + pv[HEAD_BLOCK:HEAD_BLOCK + 1], acc_old + pv[:HEAD_BLOCK])
        return state

    def commit(state):
        for c in range(2):
            m_ref[c], l_ref[c], acc_ref[c] = state[c]

    def streamed(with_band):
        state = run(with_band, exact=False)
        worst = [jnp.maximum(jnp.max(jnp.where(l < SUM_LIMIT, 0.0, 1.0)),
                             jnp.max(jnp.where(jnp.abs(acc) < SUM_LIMIT, 0.0, 1.0))) for _, l, acc in state]
        safe = jnp.maximum(worst[0], worst[1]) == 0.0
        redo_ref[0] = jnp.where(safe, 0, 1)

        @pl.when(safe)
        def _():
            commit(state)

    last_before, _ = far_side(k_first + (n_chunks - 1) * tc)
    _, first_after = far_side(k_first)
    all_far = last_before | first_after

    @pl.when(all_far)
    def _():
        streamed(False)

    @pl.when(jnp.logical_not(all_far))
    def _():
        streamed(True)

    @pl.when(redo_ref[0] == 1)
    def _():
        commit(run(True, exact=True))

    @pl.when(j == pl.num_programs(3) - 1)
    def _():
        lam = (jnp.exp(jnp.sum(lq1_ref[...] * lk1_ref[...], keepdims=True))
               - jnp.exp(jnp.sum(lq2_ref[...] * lk2_ref[...], keepdims=True)) + lambda_init)
        a = acc_ref[0] / l_ref[0] - lam * (acc_ref[1] / l_ref[1])
        a = a * lax.rsqrt(jnp.mean(a * a, axis=0, keepdims=True) + EPS) * gs_ref[...]
        o_ref[...] = (a * (1.0 - lambda_init)).T.astype(BF16)


def _proj_kernel(o_ref, w_ref, h_ref, out_ref):
    out_ref[...] = h_ref[...] + _dot(o_ref[...], w_ref[...])


def _attn_tiles(s):
    tq = min(ATTN_QUERY_TILE, s)
    tc = min(ATTN_KEY_CHUNK, s)
    assert tq % LANES == 0 and tc % LANES == 0
    return tq, tc


def _attention_layer(h, g, w_qkv, w_o, g_q, g_k, lq1, lk1, lq2, lk2, g_sub, rel_bias, lambda_init):
    b, s, d = h.shape
    nh = d // HEAD_BLOCK
    tn = min(QKV_TOKEN_TILE, s)
    hd = np.arange(d) // HEAD_DIM
    bd = jnp.asarray((hd[:, None] == hd[None, :]) / HEAD_DIM, BF16)
    tile_gain = lambda x: jnp.tile(x, d // HEAD_DIM).reshape(1, d)
    const2 = lambda shape: pl.BlockSpec(shape, lambda bi, i: (0,) * len(shape))
    qt, k, vt = pl.pallas_call(
        functools.partial(_qkv_kernel, d=d, nh=nh),
        grid=(b, s // tn),
        in_specs=[pl.BlockSpec((None, tn, d), lambda bi, i: (bi, i, 0)),
                  const2((1, d)), const2((d, 3 * d)), const2((d, d)), const2((1, d)), const2((1, d))],
        out_specs=[pl.BlockSpec((None, nh, HEAD_BLOCK, tn), lambda bi, i: (bi, 0, 0, i)),
                   pl.BlockSpec((None, tn, d), lambda bi, i: (bi, i, 0)),
                   pl.BlockSpec((None, nh, HEAD_BLOCK, tn), lambda bi, i: (bi, 0, 0, i))],
        out_shape=[jax.ShapeDtypeStruct((b, nh, HEAD_BLOCK, s), BF16),
                   jax.ShapeDtypeStruct((b, s, d), BF16),
                   jax.ShapeDtypeStruct((b, nh, HEAD_BLOCK, s), BF16)],
        compiler_params=_cparams(("parallel", "parallel")),
    )(h, g.reshape(1, d), w_qkv.astype(BF16), bd, tile_gain(g_q), tile_gain(g_k))

    tq, tc = _attn_tiles(s)
    tko = min(ATTN_KEY_BLOCK, s)
    n_chunks = tko // tc
    tiles, far = _bias_tiles(rel_bias)
    const4 = lambda shape: pl.BlockSpec(shape, lambda bi, hi, i, j: (0,) * len(shape))
    lam_spec = const4((1, HEAD_DIM))
    o = pl.pallas_call(
        functools.partial(_attn_kernel, tq=tq, tc=tc, n_chunks=n_chunks, lambda_init=lambda_init),
        grid=(b, nh, s // tq, s // tko),
        in_specs=[pl.BlockSpec(memory_space=pltpu.SMEM),
                  pl.BlockSpec((None, None, HEAD_BLOCK, tq), lambda bi, hi, i, j: (bi, hi, 0, i)),
                  pl.BlockSpec((None, tko, HEAD_BLOCK), lambda bi, hi, i, j: (bi, j, hi)),
                  pl.BlockSpec((None, None, HEAD_BLOCK, tko), lambda bi, hi, i, j: (bi, hi, 0, j)),
                  pl.BlockSpec((None, N_BIAS_TILES, LANES, LANES), lambda bi, hi, i, j: (hi, 0, 0, 0)),
                  lam_spec, lam_spec, lam_spec, lam_spec,
                  const4((HEAD_BLOCK, 1))],
        out_specs=pl.BlockSpec((None, tq, HEAD_BLOCK), lambda bi, hi, i, j: (bi, i, hi)),
        out_shape=jax.ShapeDtypeStruct((b, s, d), BF16),
        scratch_shapes=[pltpu.VMEM((2, 1, tq), F32), pltpu.VMEM((2, 1, tq), F32),
                        pltpu.VMEM((2, HEAD_BLOCK, tq), F32), pltpu.SMEM((1,), I32)],
        compiler_params=_cparams(("parallel", "parallel", "parallel", "arbitrary")),
    )(far, qt, k, vt, tiles, lq1.reshape(1, -1), lk1.reshape(1, -1), lq2.reshape(1, -1), lk2.reshape(1, -1),
      g_sub.reshape(HEAD_BLOCK, 1))

    n = b * s
    tp = min(TOKEN_TILE, n)
    out = pl.pallas_call(
        _proj_kernel,
        grid=(n // tp,),
        in_specs=[pl.BlockSpec((tp, d), lambda i: (i, 0)),
                  pl.BlockSpec((d, d), lambda i: (0, 0)),
                  pl.BlockSpec((tp, d), lambda i: (i, 0))],
        out_specs=pl.BlockSpec((tp, d), lambda i: (i, 0)),
        out_shape=jax.ShapeDtypeStruct((n, d), F32),
        compiler_params=_cparams(("parallel",)),
    )(o.reshape(n, d), w_o.astype(BF16), h.reshape(n, d))
    return out.reshape(b, s, d)


def _trunk(x, rel_bias, wa, wb, g_mix, g_ffn, w_qkv, w_attn_out, g_q, g_k, lq1, lk1, lq2, lk2, g_sub,
           w_router, w_gate, w_up, w_down):
    b, s, d = x.shape
    moe = lambda h, i: _moe_layer(h.reshape(b * s, d), g_ffn[i], w_router[i], w_gate, w_up, w_down,
                                  i).reshape(b, s, d)
    h = _fourier_layer(x, g_mix[0], wa, wb)
    h = moe(h, 0)
    lambda_init = 0.8 - 0.6 * math.exp(-0.3 * 1)
    h = _attention_layer(h, g_mix[1], w_qkv[0], w_attn_out[0], g_q[0], g_k[0], lq1[0], lk1[0], lq2[0],
                         lk2[0], g_sub[0], rel_bias, lambda_init)
    return moe(h, 1)


def kernel(x_prompt, x_sample, rel_bias, g_mix, g_ffn, w_fourier_out, w_qkv, w_attn_out, g_q, g_k,
           lambda_q1, lambda_k1, lambda_q2, lambda_k2, g_sub, w_router, w_gate, w_up, w_down):
    assert g_mix.shape[0] == 2, "one Fourier layer followed by one attention layer"
    wa, wb = _fold_w(w_fourier_out[0])
    run = lambda x: _trunk(x, rel_bias, wa, wb, g_mix, g_ffn, w_qkv, w_attn_out, g_q, g_k, lambda_q1, lambda_k1,
                           lambda_q2, lambda_k2, g_sub, w_router, w_gate, w_up, w_down)
    return (run(x_prompt), run(x_sample))
```
